```python
import jax, jax.numpy as jnp
from jax import lax
import numpy as np

D_MODEL = 1024
BATCH = 16
SEQ = 256
DEPTH = 2
DEC_BATCH = 8
DEC_SEQ = 1024
PAST_LEN = 512

GRID_W = 64
NORM_EPS = 1e-6
ATTN_HEADS = 8
ATTN_KV_HEADS = 2
HEAD_DIM = 64
ATTN_WIDTH = ATTN_HEADS * HEAD_DIM
KV_WIDTH = ATTN_KV_HEADS * HEAD_DIM
Q_BLOCK = 128
ROPE_THETA = 10000.0
ROPE_AXIS_DIM = HEAD_DIM // 2
RET_HEADS = 4
RET_DK = 64
RET_DV = 64
RET_QK_WIDTH = RET_HEADS * RET_DK
RET_WIDTH = RET_HEADS * RET_DV
RET_CHUNK = 128
RET_DECAY_EXP_FWD = 5.0
RET_DECAY_EXP_BWD = 5.5
FOURIER_GROUPS = 4
FOURIER_DIM = 64
FOURIER_WIDTH = FOURIER_GROUPS * FOURIER_DIM
MIX_WIDTH = ATTN_WIDTH + RET_WIDTH + FOURIER_WIDTH
IN_SPLITS = (ATTN_WIDTH, KV_WIDTH, KV_WIDTH, RET_QK_WIDTH, RET_QK_WIDTH, RET_WIDTH, RET_WIDTH, RET_WIDTH, FOURIER_WIDTH)
IN_WIDTH = 2304
D_FF = 2816
N_MOD = 6

kernel_name = "hybrid_diffusion_retention_fourier_gqa_step"


def _rms_norm(x, w):
    x32 = x.astype(jnp.float32)
    y = x32 * lax.rsqrt(jnp.mean(x32 * x32, axis=-1, keepdims=True) + NORM_EPS)
    return (y * w.astype(jnp.float32)).astype(x.dtype)


def _head_group_norm(o):
    mu = jnp.mean(o, axis=-1, keepdims=True)
    var = jnp.mean(jnp.square(o - mu), axis=-1, keepdims=True)
    return (o - mu) * lax.rsqrt(var + NORM_EPS)


def _axial_rope_angles(n_tokens):
    rows = n_tokens // GRID_W
    row = jnp.repeat(jnp.arange(rows), GRID_W).astype(jnp.float32)
    col = jnp.tile(jnp.arange(GRID_W), rows).astype(jnp.float32)
    n_freq = ROPE_AXIS_DIM // 2
    freqs = ROPE_THETA ** (-jnp.arange(n_freq, dtype=jnp.float32) / n_freq)
    ang = jnp.stack([row[:, None] * freqs, col[:, None] * freqs], axis=1)
    return jnp.cos(ang), jnp.sin(ang)


def _apply_axial_rope(x, cos, sin):
    b, l, h, _ = x.shape
    xr = x.astype(jnp.float32).reshape(b, l, h, 2, 2, ROPE_AXIS_DIM // 2)
    x_a, x_b = xr[..., 0, :], xr[..., 1, :]
    cs, sn = cos[None, :, None], sin[None, :, None]
    out = jnp.stack([x_a * cs - x_b * sn, x_a * sn + x_b * cs], axis=-2)
    return out.reshape(b, l, h, HEAD_DIM).astype(x.dtype)


def _blocked_gqa(q, k, v):
    b, l, h, dh = q.shape
    g = k.shape[2]
    r = h // g
    qb = q.reshape(b, l // Q_BLOCK, Q_BLOCK, g, r, dh).transpose(1, 0, 2, 3, 4, 5)
    scale = dh ** -0.5

    def block(q_blk):
        s = jnp.einsum('bqgrd,bkgd->bgrqk', q_blk, k).astype(jnp.float32) * scale
        p = jax.nn.softmax(s, axis=-1).astype(v.dtype)
        return jnp.einsum('bgrqk,bkgd->bqgrd', p, v)

    o = lax.map(block, qb)
    return o.transpose(1, 0, 2, 3, 4, 5).reshape(b, l, h * dh)


def _ret_log_gamma(exp0):
    return jnp.log1p(-jnp.exp2(-(exp0 + jnp.arange(RET_HEADS, dtype=jnp.float32))))


def _retention_chunkwise(q, k, v, log_gamma, s0):
    b, l, h, dk = q.shape
    n = l // RET_CHUNK

    def chunks(t):
        return t.astype(jnp.float32).reshape(b, n, RET_CHUNK, h, t.shape[-1]).transpose(1, 0, 3, 2, 4)

    qc, kc, vc = chunks(q), chunks(k.astype(jnp.float32) * (dk ** -0.5)), chunks(v)
    idx = jnp.arange(RET_CHUNK, dtype=jnp.float32)
    diff = idx[:, None] - idx[None, :]
    lg = log_gamma[:, None, None]
    inner_decay = jnp.where(diff[None] >= 0, jnp.exp(jnp.maximum(diff, 0.0)[None] * lg), 0.0)
    q_decay = jnp.exp((idx + 1.0)[None, :] * log_gamma[:, None])[..., None]
    k_decay = jnp.exp((RET_CHUNK - 1.0 - idx)[None, :] * log_gamma[:, None])[..., None]
    chunk_decay = jnp.exp(RET_CHUNK * log_gamma)[:, None, None]

    def step(state, inp):
        qb, kb, vb = inp
        a = jnp.einsum('bhid,bhjd->bhij', qb, kb) * inner_decay
        o = jnp.einsum('bhij,bhjv->bhiv', a, vb) + jnp.einsum('bhid,bhdv->bhiv', qb * q_decay, state)
        state = chunk_decay * state + jnp.einsum('bhjd,bhjv->bhdv', kb * k_decay, vb)
        return state, o

    s_fin, o = lax.scan(step, s0.astype(jnp.float32), (qc, kc, vc))
    o = o.transpose(1, 0, 3, 2, 4).reshape(b, l, h, v.shape[-1])
    return o, s_fin


def _fourier_mix(f):
    b, l, _ = f.shape
    fg = f.astype(jnp.float32).reshape(b, l, FOURIER_GROUPS, FOURIER_DIM)
    out = jnp.real(jnp.fft.fft2(fg, axes=(1, 3), norm='ortho'))
    return out.reshape(b, l, FOURIER_WIDTH).astype(f.dtype)


def _conv_ffn(h, w_up, conv_w, conv_b, w_down):
    u = h @ w_up
    up = jnp.pad(u, ((0, 0), (1, 1), (0, 0)))
    u = up[:, :-2] * conv_w[0] + up[:, 1:-1] * conv_w[1] + up[:, 2:] * conv_w[2] + conv_b
    gate, val = jnp.split(u, 2, axis=-1)
    return (jax.nn.silu(gate) * val) @ w_down


def _layer(x, mod, lp, ctx=None):
    norm1_w, w_in, q_norm_w, k_norm_w, w_out, norm2_w, w_up, conv_w, conv_b, w_down = lp
    sh1, sc1, g1, sh2, sc2, g2 = jnp.split(mod, N_MOD, axis=-1)
    b, l, _ = x.shape
    h = _rms_norm(x, norm1_w) * (1.0 + sc1) + sh1
    proj = h @ w_in
    split_pts = [int(p) for p in np.cumsum(IN_SPLITS)[:-1]]
    q, k, v, rq, rk, rv, gf, gb, fx = jnp.split(proj, split_pts, axis=-1)
    q = _rms_norm(q.reshape(b, l, ATTN_HEADS, HEAD_DIM), q_norm_w)
    k = _rms_norm(k.reshape(b, l, ATTN_KV_HEADS, HEAD_DIM), k_norm_w)
    v = v.reshape(b, l, ATTN_KV_HEADS, HEAD_DIM)
    rq = rq.reshape(b, l, RET_HEADS, RET_DK)
    rk = rk.reshape(b, l, RET_HEADS, RET_DK)
    rv = rv.reshape(b, l, RET_HEADS, RET_DV)
    if ctx is None:
        k_all, v_all = k, v
        s0_f = jnp.zeros((b, RET_HEADS, RET_DK, RET_DV), jnp.float32)
        s0_b = s0_f
    else:
        ctx_k, ctx_v, ctx_s = ctx
        cos, sin = _axial_rope_angles(l)
        q = _apply_axial_rope(q, cos, sin)
        k_lat = _apply_axial_rope(k, cos, sin)
        k_all = jnp.concatenate([ctx_k.astype(k.dtype), k_lat], axis=1)
        v_all = jnp.concatenate([ctx_v.astype(v.dtype), v], axis=1)
        s0_f, s0_b = ctx_s[:, 0], ctx_s[:, 1]
    attn = _blocked_gqa(q, k_all, v_all)
    o_f, s_f = _retention_chunkwise(rq, rk, rv, _ret_log_gamma(RET_DECAY_EXP_FWD), s0_f)
    o_b, s_b = _retention_chunkwise(rq[:, ::-1], rk[:, ::-1], rv[:, ::-1], _ret_log_gamma(RET_DECAY_EXP_BWD), s0_b)
    o_b = o_b[:, ::-1]
    ret = (jax.nn.silu(gf.astype(jnp.float32)) * _head_group_norm(o_f).reshape(b, l, RET_WIDTH)
           + jax.nn.silu(gb.astype(jnp.float32)) * _head_group_norm(o_b).reshape(b, l, RET_WIDTH)).astype(x.dtype)
    four = _fourier_mix(fx)
    mixed = jnp.concatenate([attn.astype(x.dtype), ret, four], axis=-1) @ w_out
    x = x + g1 * mixed
    h2 = _rms_norm(x, norm2_w) * (1.0 + sc2) + sh2
    x = x + g2 * _conv_ffn(h2, w_up, conv_w, conv_b, w_down)
    if ctx is None:
        return x, (k, v, jnp.stack([s_f, s_b], axis=1))
    return x, None


def setup_inputs(seed: int = 0) -> dict:
    key = jax.random.key(seed)
    ks = jax.random.split(key, 22)
    nrm = jax.random.normal
    f32 = jnp.float32
    d = D_MODEL
    return {
        'x_prompt': nrm(ks[0], (BATCH, SEQ, d), f32),
        'x_sample': nrm(ks[1], (DEC_BATCH, DEC_SEQ, d), f32),
        'c': nrm(ks[2], (DEC_BATCH, d), f32),
        'cache_attn_k': nrm(ks[3], (DEC_BATCH, DEPTH, PAST_LEN, ATTN_KV_HEADS, HEAD_DIM), f32),
        'cache_attn_v': nrm(ks[4], (DEC_BATCH, DEPTH, PAST_LEN, ATTN_KV_HEADS, HEAD_DIM), f32),
        'state_ret': 0.5 * nrm(ks[5], (DEC_BATCH, DEPTH, 2, RET_HEADS, RET_DK, RET_DV), f32),
        'c_ctx': nrm(ks[6], (d,), f32),
        'w_mod': 0.5 * d ** -0.5 * nrm(ks[7], (DEPTH, d, N_MOD * d), f32),
        'b_mod': 0.02 * nrm(ks[8], (DEPTH, N_MOD * d), f32),
        'norm1_w': 1.0 + 0.02 * nrm(ks[9], (DEPTH, d), f32),
        'w_in': d ** -0.5 * nrm(ks[10], (DEPTH, d, IN_WIDTH), f32),
        'q_norm_w': 1.0 + 0.02 * nrm(ks[11], (DEPTH, HEAD_DIM), f32),
        'k_norm_w': 1.0 + 0.02 * nrm(ks[12], (DEPTH, HEAD_DIM), f32),
        'w_out': MIX_WIDTH ** -0.5 * nrm(ks[13], (DEPTH, MIX_WIDTH, d), f32),
        'norm2_w': 1.0 + 0.02 * nrm(ks[14], (DEPTH, d), f32),
        'w_up': d ** -0.5 * nrm(ks[15], (DEPTH, d, 2 * D_FF), f32),
        'conv_w': 3.0 ** -0.5 * nrm(ks[16], (DEPTH, 3, 2 * D_FF), f32),
        'conv_b': 0.02 * nrm(ks[17], (DEPTH, 2 * D_FF), f32),
        'w_down': D_FF ** -0.5 * nrm(ks[18], (DEPTH, D_FF, d), f32),
        'final_norm_w': 1.0 + 0.02 * nrm(ks[19], (d,), f32),
    }


def reference(x_prompt, x_sample, c, cache_attn_k, cache_attn_v, state_ret, c_ctx, w_mod, b_mod,
              norm1_w, w_in, q_norm_w, k_norm_w, w_out, norm2_w, w_up, conv_w, conv_b, w_down,
              final_norm_w):
    xp, xs = x_prompt, x_sample
    new_k, new_v, new_s = [], [], []
    for layer in range(DEPTH):
        lp = (norm1_w[layer], w_in[layer], q_norm_w[layer], k_norm_w[layer], w_out[layer],
              norm2_w[layer], w_up[layer], conv_w[layer], conv_b[layer], w_down[layer])
        mod_ctx = (jax.nn.silu(c_ctx) @ w_mod[layer] + b_mod[layer])[None, None, :]
        mod_lat = (jax.nn.silu(c) @ w_mod[layer] + b_mod[layer])[:, None, :]
        xp, (k_l, v_l, s_l) = _layer(xp, mod_ctx, lp)
        new_k.append(k_l)
        new_v.append(v_l)
        new_s.append(s_l.astype(x_prompt.dtype))
        xs, _ = _layer(xs, mod_lat, lp, ctx=(cache_attn_k[:, layer], cache_attn_v[:, layer], state_ret[:, layer]))
    y_prompt = _rms_norm(xp, final_norm_w)
    y_sample = _rms_norm(xs, final_norm_w)
    new_cache_attn_k = jnp.stack(new_k, axis=1)
    new_cache_attn_v = jnp.stack(new_v, axis=1)
    new_state_ret = jnp.stack(new_s, axis=1)
    return (y_prompt, y_sample, new_cache_attn_k, new_cache_attn_v, new_state_ret)
```

```python
import functools

import jax
import jax.numpy as jnp
import numpy as np
from jax import lax
from jax.experimental import pallas as pl
from jax.experimental.pallas import tpu as pltpu

F32 = jnp.float32
BF16 = jnp.bfloat16

D_MODEL = 1024
DEPTH = 2
GRID_W = 64
NORM_EPS = 1e-6
ATTN_HEADS = 8
ATTN_KV_HEADS = 2
HEAD_DIM = 64
ATTN_WIDTH = ATTN_HEADS * HEAD_DIM
KV_WIDTH = ATTN_KV_HEADS * HEAD_DIM
HEADS_PER_KV = ATTN_HEADS // ATTN_KV_HEADS
GROUP_WIDTH = HEADS_PER_KV * HEAD_DIM
ROPE_THETA = 10000.0
ROPE_AXIS_DIM = HEAD_DIM // 2
ROPE_HALF = ROPE_AXIS_DIM // 2
RET_HEADS = 4
RET_DK = 64
RET_DV = 64
RET_WIDTH = RET_HEADS * RET_DV
RET_CHUNK = 128
RET_DECAY_EXP_FWD = 5.0
RET_DECAY_EXP_BWD = 5.5
FOURIER_GROUPS = 4
FOURIER_DIM = 64
FOURIER_WIDTH = FOURIER_GROUPS * FOURIER_DIM
D_FF = 2816
N_MOD = 6

OFF_Q = 0
OFF_K = OFF_Q + ATTN_WIDTH
OFF_V = OFF_K + KV_WIDTH
OFF_RQ = OFF_V + KV_WIDTH
OFF_RK = OFF_RQ + RET_WIDTH
OFF_RV = OFF_RK + RET_WIDTH
OFF_GF = OFF_RV + RET_WIDTH
OFF_GB = OFF_GF + RET_WIDTH
OFF_FX = OFF_GB + RET_WIDTH
IN_WIDTH = OFF_FX + FOURIER_WIDTH

V7X_VMEM_BYTES = 64 * 1024 * 1024
VMEM_LIMIT = V7X_VMEM_BYTES - 12 * 1024 * 1024

MOD_ROWS = 16
MOD_COLS = 1536
INPROJ_ROWS = 512
ATTN_Q_ROWS = 256
FFN_ROWS = 1024
FFN_COLS = 256


def _params(*semantics):
    return pltpu.CompilerParams(dimension_semantics=semantics, vmem_limit_bytes=VMEM_LIMIT)


def _dot(a, b):
    return jnp.dot(a, b, preferred_element_type=F32)


def _dot_nt(a, b):
    return lax.dot_general(a, b, (((1,), (1,)), ((), ())), preferred_element_type=F32)


def _sigmoid(x):
    return 1.0 / (1.0 + jnp.exp(-x))


def _silu(x):
    return x * _sigmoid(x)


def _rms_rows(x, w):
    ms = jnp.mean(x * x, axis=-1, keepdims=True)
    return x * lax.rsqrt(ms + NORM_EPS) * w


def _lane_block(shape, width):
    return lax.broadcasted_iota(jnp.int32, shape, len(shape) - 1) // width


def _group_mean_matrix(width, group):
    idx = np.arange(width) // group
    return jnp.asarray((idx[:, None] == idx[None, :]).astype(np.float32) / group, dtype=BF16)


def _rope_tables(n_tokens):
    pos = np.arange(n_tokens)
    row = (pos // GRID_W).astype(np.float64)
    col = (pos % GRID_W).astype(np.float64)
    freqs = ROPE_THETA ** (-np.arange(ROPE_HALF, dtype=np.float64) / ROPE_HALF)
    d = np.arange(HEAD_DIM)
    coord = np.where((d // ROPE_AXIS_DIM)[None, :] == 0, row[:, None], col[:, None])
    ang = coord * freqs[d % ROPE_HALF][None, :]
    sign = np.where((d & ROPE_HALF) == 0, -1.0, 1.0)[None, :]
    cos = np.tile(np.cos(ang), (1, ATTN_HEADS))
    sin = np.tile(np.sin(ang) * sign, (1, ATTN_HEADS))
    return jnp.asarray(cos, F32), jnp.asarray(sin, F32)


def _dft_tables(seq):
    n = np.arange(seq)
    ang = 2.0 * np.pi * ((n[:, None] * n[None, :]) % seq) / seq
    scale = 1.0 / np.sqrt(seq * FOURIER_DIM)
    c = np.arange(FOURIER_WIDTH)
    same = (c[:, None] // FOURIER_DIM) == (c[None, :] // FOURIER_DIM)
    angc = 2.0 * np.pi * (((c % FOURIER_DIM)[:, None] * (c % FOURIER_DIM)[None, :]) % FOURIER_DIM) / FOURIER_DIM
    as_bf16 = lambda a: jnp.asarray(a, F32).astype(BF16)
    return (as_bf16(np.cos(ang) * scale), as_bf16(-np.sin(ang) * scale),
            as_bf16(np.where(same, np.cos(angc), 0.0)), as_bf16(np.where(same, np.sin(angc), 0.0)))


def _retention_tables():
    heads = jnp.arange(RET_HEADS, dtype=F32)
    idx = jnp.arange(RET_CHUNK, dtype=F32)
    diff = idx[:, None] - idx[None, :]
    out = []
    for exp0, backward in ((RET_DECAY_EXP_FWD, False), (RET_DECAY_EXP_BWD, True)):
        lg = jnp.log1p(-jnp.exp2(-(exp0 + heads)))
        dd = -diff if backward else diff
        inner = jnp.where(dd[None] >= 0, jnp.exp(jnp.maximum(dd, 0.0)[None] * lg[:, None, None]), 0.0)
        inner = inner.transpose(1, 0, 2).reshape(RET_CHUNK, RET_HEADS * RET_CHUNK)
        q_pow = (RET_CHUNK - idx) if backward else (idx + 1.0)
        k_pow = idx if backward else (RET_CHUNK - 1.0 - idx)
        spread = lambda p: jnp.repeat(jnp.exp(p[:, None] * lg[None, :]), RET_DK, axis=1)
        q_decay, k_decay = spread(q_pow), spread(k_pow)
        chunk_decay = jnp.repeat(jnp.exp(RET_CHUNK * lg), RET_DV)[None, :]
        out.append((inner, q_decay, k_decay, chunk_decay))
    return out


def _mod_kernel(c_ref, w_ref, b_ref, o_ref):
    act = _silu(c_ref[...]).astype(BF16)
    o_ref[...] = _dot(act, w_ref[...].astype(BF16)) + b_ref[...]


def _modulation(cvec, w_mod, b_mod):
    n_cols = w_mod.shape[-1]
    return pl.pallas_call(
        _mod_kernel,
        out_shape=jax.ShapeDtypeStruct((DEPTH, MOD_ROWS, n_cols), F32),
        grid=(DEPTH, n_cols // MOD_COLS),
        in_specs=[
            pl.BlockSpec((MOD_ROWS, D_MODEL), lambda l, j: (0, 0)),
            pl.BlockSpec((None, D_MODEL, MOD_COLS), lambda l, j: (l, 0, j)),
            pl.BlockSpec((None, 1, MOD_COLS), lambda l, j: (l, 0, j)),
        ],
        out_specs=pl.BlockSpec((None, MOD_ROWS, MOD_COLS), lambda l, j: (l, 0, j)),
        compiler_params=_params("parallel", "parallel"),
        name="modulation",
    )(cvec, w_mod, b_mod.reshape(DEPTH, 1, n_cols))


def _swap_rotary_pairs(x):
    width = x.shape[-1]
    lane = lax.broadcasted_iota(jnp.int32, x.shape, 1)
    from_below = pltpu.roll(x, ROPE_HALF, 1)
    from_above = pltpu.roll(x, width - ROPE_HALF, 1)
    return jnp.where((lane & ROPE_HALF) != 0, from_below, from_above)


def _inproj_kernel(*refs, latent):
    x_ref, mod_ref, n1_ref, win_ref, qnw_ref, knw_ref, gm_ref = refs[:7]
    refs = refs[7:]
    if latent:
        cos_ref, sin_ref = refs[:2]
        refs = refs[2:]
        q_ref, k_ref, v_ref, rq_ref, rk_ref, rv_ref, g_ref, fx_ref = refs
    else:
        q_ref, k_ref, v_ref, rq_ref, rk_ref, rv_ref, g_ref, fx_ref, k32_ref, v32_ref = refs

    shift, scale = mod_ref[0:1, :], mod_ref[1:2, :]
    h = _rms_rows(x_ref[...], n1_ref[...]) * (1.0 + scale) + shift
    hb = h.astype(BF16)
    proj = lambda off, width: _dot(hb, win_ref[:, off:off + width])

    q = proj(OFF_Q, ATTN_WIDTH)
    q = q * lax.rsqrt(_dot((q * q).astype(BF16), gm_ref[...]) + NORM_EPS) * qnw_ref[...]
    k = proj(OFF_K, KV_WIDTH)
    k = k * lax.rsqrt(_dot((k * k).astype(BF16), gm_ref[0:KV_WIDTH, 0:KV_WIDTH]) + NORM_EPS) * knw_ref[...]
    v = proj(OFF_V, KV_WIDTH)
    if latent:
        cos, sin = cos_ref[...], sin_ref[...]
        q = q * cos + _swap_rotary_pairs(q) * sin
        k = k * cos[:, 0:KV_WIDTH] + _swap_rotary_pairs(k) * sin[:, 0:KV_WIDTH]
    else:
        k32_ref[...] = k
        v32_ref[...] = v
    q_ref[...] = (q * HEAD_DIM ** -0.5).astype(BF16)
    k_ref[...] = k.astype(BF16)
    v_ref[...] = v.astype(BF16)
    rq_ref[...] = proj(OFF_RQ, RET_WIDTH).astype(BF16)
    rk_ref[...] = (proj(OFF_RK, RET_WIDTH) * RET_DK ** -0.5).astype(BF16)
    rv_ref[...] = proj(OFF_RV, RET_WIDTH).astype(BF16)
    g_ref[...] = proj(OFF_GF, 2 * RET_WIDTH)
    fx_ref[...] = proj(OFF_FX, FOURIER_WIDTH).astype(BF16)


def _in_projection(x, mod, norm1_w, w_in, q_norm_w, k_norm_w, group_mean, rope, seq):
    n = x.shape[0]
    rows = INPROJ_ROWS
    latent = rope is not None
    row_blk = lambda width: pl.BlockSpec((rows, width), lambda i: (i, 0))
    whole = lambda shape: pl.BlockSpec(shape, lambda i: (0,) * len(shape))
    mod_idx = (lambda i: (i * rows // seq, 0, 0)) if latent else (lambda i: (0, 0, 0))
    in_specs = [
        row_blk(D_MODEL),
        pl.BlockSpec((None, N_MOD, D_MODEL), mod_idx),
        whole((1, D_MODEL)),
        whole((D_MODEL, IN_WIDTH)),
        whole((1, ATTN_WIDTH)),
        whole((1, KV_WIDTH)),
        whole((ATTN_WIDTH, ATTN_WIDTH)),
    ]
    args = [x, mod, norm1_w, w_in, q_norm_w, k_norm_w, group_mean]
    out_widths = [(ATTN_WIDTH, BF16), (KV_WIDTH, BF16), (KV_WIDTH, BF16), (RET_WIDTH, BF16), (RET_WIDTH, BF16),
                  (RET_WIDTH, BF16), (2 * RET_WIDTH, F32), (FOURIER_WIDTH, BF16)]
    if latent:
        pos_blk = pl.BlockSpec((rows, ATTN_WIDTH), lambda i: (i % (seq // rows), 0))
        in_specs += [pos_blk, pos_blk]
        args += list(rope)
    else:
        out_widths += [(KV_WIDTH, F32), (KV_WIDTH, F32)]
    return pl.pallas_call(
        functools.partial(_inproj_kernel, latent=latent),
        out_shape=[jax.ShapeDtypeStruct((n, w), dt) for w, dt in out_widths],
        grid=(n // rows,),
        in_specs=in_specs,
        out_specs=[row_blk(w) for w, _ in out_widths],
        compiler_params=_params("parallel"),
        name="in_projection_latent" if latent else "in_projection_context",
    )(*args)


def _spread_kv(x, group):
    lane = lax.broadcasted_iota(jnp.int32, x.shape, 1)
    other = pltpu.roll(x, HEAD_DIM, 1)
    own = (lane // HEAD_DIM) == group
    pair = jnp.where(own, x, other).astype(BF16)
    return jnp.concatenate([pair, pair], axis=1)


def _attention_kernel(*refs, past):
    if past:
        q_ref, k_ref, v_ref, ck_ref, cv_ref, o_ref, kt_ref, vt_ref = refs
    else:
        q_ref, k_ref, v_ref, o_ref, kt_ref, vt_ref = refs

    @pl.when(pl.program_id(1) == 0)
    def _():
        for g in range(ATTN_KV_HEADS):
            if past:
                kt_ref[g, 0:past, :] = _spread_kv(ck_ref[...], g)
                vt_ref[g, 0:past, :] = _spread_kv(cv_ref[...], g)
            kt_ref[g, past:, :] = _spread_kv(k_ref[...].astype(F32), g)
            vt_ref[g, past:, :] = _spread_kv(v_ref[...].astype(F32), g)

    rows = q_ref.shape[0]
    block = _lane_block((rows, GROUP_WIDTH), HEAD_DIM)
    for g in range(ATTN_KV_HEADS):
        qg = q_ref[:, g * GROUP_WIDTH:(g + 1) * GROUP_WIDTH]
        out = jnp.zeros((rows, GROUP_WIDTH), F32)
        for h in range(HEADS_PER_KV):
            qh = jnp.where(block == h, qg, jnp.zeros_like(qg))
            s = _dot_nt(qh, kt_ref[g])
            p = jnp.exp(s - jnp.max(s, axis=-1, keepdims=True))
            denom = jnp.sum(p, axis=-1, keepdims=True)
            o = _dot(p.astype(BF16), vt_ref[g])
            out = jnp.where(block == h, o / denom, out)
        o_ref[:, g * GROUP_WIDTH:(g + 1) * GROUP_WIDTH] = out.astype(BF16)


def _attention(q, k, v, cache, layer, batch, seq):
    n = q.shape[0]
    tq = ATTN_Q_ROWS
    nq = seq // tq
    past = 0 if cache is None else cache[0].shape[2]
    own_kv = pl.BlockSpec((seq, KV_WIDTH), lambda b, i: (b, 0))
    in_specs = [pl.BlockSpec((tq, ATTN_WIDTH), lambda b, i: (b * nq + i, 0)), own_kv, own_kv]
    args = [q, k, v]
    if past:
        cached_kv = pl.BlockSpec((None, None, past, KV_WIDTH), lambda b, i: (b, layer, 0, 0))
        in_specs += [cached_kv, cached_kv]
        args += list(cache)
    return pl.pallas_call(
        functools.partial(_attention_kernel, past=past),
        out_shape=jax.ShapeDtypeStruct((n, ATTN_WIDTH), BF16),
        grid=(batch, nq),
        in_specs=in_specs,
        out_specs=pl.BlockSpec((tq, ATTN_WIDTH), lambda b, i: (b * nq + i, 0)),
        scratch_shapes=[pltpu.VMEM((ATTN_KV_HEADS, past + seq, GROUP_WIDTH), BF16)] * 2,
        compiler_params=_params("parallel", "arbitrary"),
        name="attention_latent" if past else "attention_context",
    )(*args)


def _stack_heads(x):
    block = _lane_block(x.shape, RET_DK)
    zero = jnp.zeros_like(x)
    return jnp.concatenate([jnp.where(block == h, x, zero) for h in range(RET_HEADS)], axis=0)


def _head_norm(o, gm):
    mu = _dot(o.astype(BF16), gm)
    d = o - mu
    var = _dot((d * d).astype(BF16), gm)
    return d * lax.rsqrt(var + NORM_EPS)


def _retention_kernel(*refs, has_state, n_chunks):
    rq_ref, rk_ref, rv_ref, g_ref, gm_ref = refs[:5]
    (df_ref, qdf_ref, kdf_ref, cdf_ref, db_ref, qdb_ref, kdb_ref, cdb_ref) = refs[5:13]
    refs = refs[13:]
    if has_state:
        s0_ref, o_ref, part_ref, intra_ref, upd_ref = refs
    else:
        o_ref, sfin_ref, part_ref, intra_ref, upd_ref = refs

    width = RET_WIDTH
    diag = (lax.broadcasted_iota(jnp.int32, (width, width), 0) // RET_DK
            == lax.broadcasted_iota(jnp.int32, (width, width), 1) // RET_DV)
    gm = gm_ref[...]

    def load_state(direction):
        if not has_state:
            return jnp.zeros((width, width), F32)
        s = s0_ref[direction].reshape(width, RET_DV)
        return jnp.where(diag, jnp.concatenate([s] * RET_HEADS, axis=1), 0.0)

    def store_state(direction, s):
        folded = s[:, 0:width // 2] + s[:, width // 2:]
        sfin_ref[direction] = folded[:, 0:RET_DV] + folded[:, RET_DV:]

    state = load_state(0)
    for c in range(n_chunks):
        rows = pl.ds(c * RET_CHUNK, RET_CHUNK)
        qc, kc, vc = rq_ref[rows, :], rk_ref[rows, :], rv_ref[rows, :]
        q32, k32 = qc.astype(F32), kc.astype(F32)
        scores = _dot_nt(qc, _stack_heads(kc))
        v_heads = _stack_heads(vc)
        intra_ref[rows, :] = _dot((scores * db_ref[...]).astype(BF16), v_heads)
        upd_ref[c] = jnp.where(diag, _dot((k32 * kdb_ref[...]).T.astype(BF16), vc), 0.0)
        o = (_dot((scores * df_ref[...]).astype(BF16), v_heads)
             + _dot((q32 * qdf_ref[...]).astype(BF16), state.astype(BF16)))
        state = cdf_ref[...] * state + jnp.where(diag, _dot((k32 * kdf_ref[...]).T.astype(BF16), vc), 0.0)
        part_ref[rows, :] = _silu(g_ref[rows, 0:width]) * _head_norm(o, gm)
    if not has_state:
        store_state(0, state)

    state = load_state(1)
    for c in reversed(range(n_chunks)):
        rows = pl.ds(c * RET_CHUNK, RET_CHUNK)
        q32 = rq_ref[rows, :].astype(F32)
        o = intra_ref[rows, :] + _dot((q32 * qdb_ref[...]).astype(BF16), state.astype(BF16))
        state = cdb_ref[...] * state + upd_ref[c]
        o_ref[rows, :] = (part_ref[rows, :] + _silu(g_ref[rows, width:]) * _head_norm(o, gm)).astype(BF16)
    if not has_state:
        store_state(1, state)


def _retention(rq, rk, rv, gates, group_mean, tables, state, layer, batch, seq):
    n = rq.shape[0]
    has_state = state is not None
    n_chunks = seq // RET_CHUNK
    seq_blk = lambda width: pl.BlockSpec((seq, width), lambda b: (b, 0))
    whole = lambda a: pl.BlockSpec(a.shape, lambda b: (0,) * a.ndim)
    flat_tables = [t for direction in tables for t in direction]
    in_specs = [seq_blk(RET_WIDTH)] * 3 + [seq_blk(2 * RET_WIDTH), whole(group_mean)] + [whole(t) for t in flat_tables]
    args = [rq, rk, rv, gates, group_mean] + flat_tables
    out_shape = [jax.ShapeDtypeStruct((n, RET_WIDTH), BF16)]
    out_specs = [seq_blk(RET_WIDTH)]
    if has_state:
        in_specs.append(pl.BlockSpec((None, None, 2, RET_HEADS, RET_DK, RET_DV), lambda b: (b, layer, 0, 0, 0, 0)))
        args.append(state)
    else:
        out_shape.append(jax.ShapeDtypeStruct((batch, 2, RET_HEADS * RET_DK, RET_DV), F32))
        out_specs.append(pl.BlockSpec((None, 2, RET_HEADS * RET_DK, RET_DV), lambda b: (b, 0, 0, 0)))
    return pl.pallas_call(
        functools.partial(_retention_kernel, has_state=has_state, n_chunks=n_chunks),
        out_shape=out_shape,
        grid=(batch,),
        in_specs=in_specs,
        out_specs=out_specs,
        scratch_shapes=[pltpu.VMEM((seq, RET_WIDTH), F32), pltpu.VMEM((seq, RET_WIDTH), F32),
                        pltpu.VMEM((n_chunks, RET_WIDTH, RET_WIDTH), F32)],
        compiler_params=_params("parallel"),
        name="retention_latent" if has_state else "retention_context",
    )(*args)


def _fourier_kernel(f_ref, cos_ref, nsin_ref, ccos_ref, csin_ref, o_ref):
    f = f_ref[...]
    along_c = _dot(f, ccos_ref[...]).astype(BF16)
    along_s = _dot(f, csin_ref[...]).astype(BF16)
    o_ref[...] = (_dot(cos_ref[...], along_c) + _dot(nsin_ref[...], along_s)).astype(BF16)


def _fourier(fx, tables, batch, seq):
    n = fx.shape[0]
    whole = lambda a: pl.BlockSpec(a.shape, lambda b: (0,) * a.ndim)
    seq_blk = pl.BlockSpec((seq, FOURIER_WIDTH), lambda b: (b, 0))
    return pl.pallas_call(
        _fourier_kernel,
        out_shape=jax.ShapeDtypeStruct((n, FOURIER_WIDTH), BF16),
        grid=(batch,),
        in_specs=[seq_blk] + [whole(t) for t in tables],
        out_specs=seq_blk,
        compiler_params=_params("parallel"),
        name="fourier_seq%d" % seq,
    )(fx, *tables)


def _shift_rows(u, seq):
    rows = u.shape[0]
    pos = lax.broadcasted_iota(jnp.int32, u.shape, 0) % seq
    prev = jnp.where(pos == 0, 0.0, pltpu.roll(u, 1, 0))
    nxt = jnp.where(pos == seq - 1, 0.0, pltpu.roll(u, rows - 1, 0))
    return prev, nxt


def _ffn_kernel(x_ref, attn_ref, ret_ref, four_ref, mod_ref, wout_ref, n2_ref, wg_ref, wv_ref,
                cwg_ref, cwv_ref, cbg_ref, cbv_ref, wd_ref, fin_ref, o_ref, x1_ref, h2_ref, acc_ref,
                *, seq, final):
    j = pl.program_id(1)

    @pl.when(j == 0)
    def _():
        mixed = (_dot(attn_ref[...], wout_ref[0:ATTN_WIDTH, :])
                 + _dot(ret_ref[...], wout_ref[ATTN_WIDTH:ATTN_WIDTH + RET_WIDTH, :])
                 + _dot(four_ref[...], wout_ref[ATTN_WIDTH + RET_WIDTH:, :]))
        x1 = x_ref[...] + mod_ref[2:3, :] * mixed
        x1_ref[...] = x1
        h2 = _rms_rows(x1, n2_ref[...]) * (1.0 + mod_ref[4:5, :]) + mod_ref[3:4, :]
        h2_ref[...] = h2.astype(BF16)
        acc_ref[...] = jnp.zeros_like(acc_ref)

    def conv(u, cw_ref, cb_ref):
        prev, nxt = _shift_rows(u, seq)
        return prev * cw_ref[0:1, :] + u * cw_ref[1:2, :] + nxt * cw_ref[2:3, :] + cb_ref[...]

    h2 = h2_ref[...]
    gate = conv(_dot(h2, wg_ref[...]), cwg_ref, cbg_ref)
    val = conv(_dot(h2, wv_ref[...]), cwv_ref, cbv_ref)
    acc_ref[...] += _dot((_silu(gate) * val).astype(BF16), wd_ref[...])

    @pl.when(j == pl.num_programs(1) - 1)
    def _():
        y = x1_ref[...] + mod_ref[5:6, :] * acc_ref[...]
        o_ref[...] = _rms_rows(y, fin_ref[...]) if final else y


def _out_ffn(x, attn, ret, four, mod, w_out, norm2_w, w_up, conv_w, conv_b, w_down, final_w, seq, per_seq_mod, final):
    n = x.shape[0]
    rows, cols = FFN_ROWS, FFN_COLS
    n_col_blocks = D_FF // cols
    row_blk = lambda width: pl.BlockSpec((rows, width), lambda i, j: (i, 0))
    whole = lambda shape: pl.BlockSpec(shape, lambda i, j: (0,) * len(shape))
    mod_idx = (lambda i, j: (i * rows // seq, 0, 0)) if per_seq_mod else (lambda i, j: (0, 0, 0))
    gate_cols = lambda r: pl.BlockSpec((r, cols), lambda i, j: (0, j))
    val_cols = lambda r: pl.BlockSpec((r, cols), lambda i, j: (0, n_col_blocks + j))
    in_specs = [
        row_blk(D_MODEL), row_blk(ATTN_WIDTH), row_blk(RET_WIDTH), row_blk(FOURIER_WIDTH),
        pl.BlockSpec((None, N_MOD, D_MODEL), mod_idx),
        whole((D_MODEL, D_MODEL)), whole((1, D_MODEL)),
        gate_cols(D_MODEL), val_cols(D_MODEL),
        gate_cols(3), val_cols(3), gate_cols(1), val_cols(1),
        pl.BlockSpec((cols, D_MODEL), lambda i, j: (j, 0)),
        whole((1, D_MODEL)),
    ]
    return pl.pallas_call(
        functools.partial(_ffn_kernel, seq=seq, final=final),
        out_shape=jax.ShapeDtypeStruct((n, D_MODEL), F32),
        grid=(n // rows, n_col_blocks),
        in_specs=in_specs,
        out_specs=row_blk(D_MODEL),
        scratch_shapes=[pltpu.VMEM((rows, D_MODEL), F32), pltpu.VMEM((rows, D_MODEL), BF16),
                        pltpu.VMEM((rows, D_MODEL), F32)],
        compiler_params=_params("parallel", "arbitrary"),
        name="out_ffn_seq%d" % seq,
    )(x, attn, ret, four, mod, w_out, norm2_w, w_up, w_up, conv_w, conv_w, conv_b, conv_b, w_down, final_w)


def _layer(x, mod, lp, consts, layer, batch, seq, ctx, final):
    latent = ctx is not None
    outs = _in_projection(x, mod, lp["norm1_w"], lp["w_in"], lp["q_norm_w"], lp["k_norm_w"], consts["gm_attn"],
                          consts["rope"] if latent else None, seq)
    q, k, v, rq, rk, rv, gates, fx = outs[:8]
    attn = _attention(q, k, v, ctx[:2] if latent else None, layer, batch, seq)
    ret_out = _retention(rq, rk, rv, gates, consts["gm_ret"], consts["ret_tables"], ctx[2] if latent else None,
                         layer, batch, seq)
    four = _fourier(fx, consts["dft"][seq], batch, seq)
    x = _out_ffn(x, attn, ret_out[0], four, mod, lp["w_out"], lp["norm2_w"], lp["w_up"], lp["conv_w"], lp["conv_b"],
                 lp["w_down"], consts["final_w"], seq, latent, final)
    if latent:
        return x, None
    return x, (outs[8], outs[9], ret_out[1])


def kernel(x_prompt, x_sample, c, cache_attn_k, cache_attn_v, state_ret, c_ctx, w_mod, b_mod, norm1_w, w_in,
           q_norm_w, k_norm_w, w_out, norm2_w, w_up, conv_w, conv_b, w_down, final_norm_w):
    batch, seq, d = x_prompt.shape
    dec_batch, dec_seq, _ = x_sample.shape
    past = cache_attn_k.shape[2]
    assert d == D_MODEL and w_in.shape == (DEPTH, D_MODEL, IN_WIDTH) and w_up.shape == (DEPTH, D_MODEL, 2 * D_FF)
    assert (batch * seq) % FFN_ROWS == 0 and FFN_ROWS % seq == 0 and dec_seq % FFN_ROWS == 0
    assert seq % ATTN_Q_ROWS == 0 and dec_seq % ATTN_Q_ROWS == 0
    assert INPROJ_ROWS % seq == 0 and dec_seq % INPROJ_ROWS == 0 and seq % RET_CHUNK == 0 and dec_seq % RET_CHUNK == 0
    assert dec_batch + 1 <= MOD_ROWS and dec_seq % GRID_W == 0

    consts = {
        "gm_attn": _group_mean_matrix(ATTN_WIDTH, HEAD_DIM),
        "gm_ret": _group_mean_matrix(RET_WIDTH, RET_DV),
        "rope": _rope_tables(dec_seq),
        "ret_tables": _retention_tables(),
        "dft": {s: _dft_tables(s) for s in {seq, dec_seq}},
        "final_w": final_norm_w.reshape(1, D_MODEL),
    }

    cvec = jnp.zeros((MOD_ROWS, D_MODEL), F32).at[0].set(c_ctx).at[1:1 + dec_batch].set(c)
    mod = _modulation(cvec, w_mod, b_mod).reshape(DEPTH, MOD_ROWS, N_MOD, D_MODEL)

    cache_k = cache_attn_k.reshape(dec_batch, DEPTH, past, KV_WIDTH)
    cache_v = cache_attn_v.reshape(dec_batch, DEPTH, past, KV_WIDTH)

    xp = x_prompt.reshape(batch * seq, D_MODEL)
    xs = x_sample.reshape(dec_batch * dec_seq, D_MODEL)
    new_k, new_v, new_s = [], [], []
    for layer in range(DEPTH):
        lp = {
            "norm1_w": norm1_w[layer].reshape(1, D_MODEL),
            "w_in": w_in[layer].astype(BF16),
            "q_norm_w": jnp.tile(q_norm_w[layer], ATTN_HEADS).reshape(1, ATTN_WIDTH),
            "k_norm_w": jnp.tile(k_norm_w[layer], ATTN_KV_HEADS).reshape(1, KV_WIDTH),
            "w_out": w_out[layer].astype(BF16),
            "norm2_w": norm2_w[layer].reshape(1, D_MODEL),
            "w_up": w_up[layer].astype(BF16),
            "conv_w": conv_w[layer],
            "conv_b": conv_b[layer].reshape(1, 2 * D_FF),
            "w_down": w_down[layer].astype(BF16),
        }
        final = layer == DEPTH - 1
        xp, (k_l, v_l, s_l) = _layer(xp, mod[layer, 0:1], lp, consts, layer, batch, seq, None, final)
        new_k.append(k_l.reshape(batch, seq, ATTN_KV_HEADS, HEAD_DIM))
        new_v.append(v_l.reshape(batch, seq, ATTN_KV_HEADS, HEAD_DIM))
        new_s.append(s_l.reshape(batch, 2, RET_HEADS, RET_DK, RET_DV))
        xs, _ = _layer(xs, mod[layer, 1:1 + dec_batch], lp, consts, layer, dec_batch, dec_seq,
                       (cache_k, cache_v, state_ret), final)
    return (xp.reshape(batch, seq, D_MODEL), xs.reshape(dec_batch, dec_seq, D_MODEL),
            jnp.stack(new_k, axis=1), jnp.stack(new_v, axis=1), jnp.stack(new_s, axis=1))
```

```python
import functools

import jax
import jax.numpy as jnp
import numpy as np
from jax import lax
from jax.experimental import pallas as pl
from jax.experimental.pallas import tpu as pltpu

F32 = jnp.float32
BF16 = jnp.bfloat16

D_MODEL = 1024
DEPTH = 2
GRID_W = 64
NORM_EPS = 1e-6
ATTN_HEADS = 8
ATTN_KV_HEADS = 2
HEAD_DIM = 64
ATTN_WIDTH = ATTN_HEADS * HEAD_DIM
KV_WIDTH = ATTN_KV_HEADS * HEAD_DIM
HEADS_PER_KV = ATTN_HEADS // ATTN_KV_HEADS
GROUP_WIDTH = HEADS_PER_KV * HEAD_DIM
ROPE_THETA = 10000.0
ROPE_AXIS_DIM = HEAD_DIM // 2
ROPE_HALF = ROPE_AXIS_DIM // 2
RET_HEADS = 4
RET_DK = 64
RET_DV = 64
RET_WIDTH = RET_HEADS * RET_DV
RET_CHUNK = 128
RET_DECAY_EXP_FWD = 5.0
RET_DECAY_EXP_BWD = 5.5
FOURIER_GROUPS = 4
FOURIER_DIM = 64
FOURIER_WIDTH = FOURIER_GROUPS * FOURIER_DIM
D_FF = 2816
N_MOD = 6

OFF_Q = 0
OFF_K = OFF_Q + ATTN_WIDTH
OFF_V = OFF_K + KV_WIDTH
OFF_RQ = OFF_V + KV_WIDTH
OFF_RK = OFF_RQ + RET_WIDTH
OFF_RV = OFF_RK + RET_WIDTH
OFF_GF = OFF_RV + RET_WIDTH
OFF_GB = OFF_GF + RET_WIDTH
OFF_FX = OFF_GB + RET_WIDTH
IN_WIDTH = OFF_FX + FOURIER_WIDTH

V7X_VMEM_BYTES = 64 * 1024 * 1024
VMEM_LIMIT = V7X_VMEM_BYTES - 12 * 1024 * 1024

MOD_ROWS = 16
MOD_COLS = 1536
INPROJ_ROWS = 512
ATTN_Q_ROWS = 256
FFN_ROWS = 1024
FFN_COLS = 256


def _params(*semantics):
    return pltpu.CompilerParams(dimension_semantics=semantics, vmem_limit_bytes=VMEM_LIMIT)


def _dot(a, b):
    return jnp.dot(a, b, preferred_element_type=F32)


def _dot_nt(a, b):
    return lax.dot_general(a, b, (((1,), (1,)), ((), ())), preferred_element_type=F32)


def _sigmoid(x):
    return 1.0 / (1.0 + jnp.exp(-x))


def _silu(x):
    return x * _sigmoid(x)


def _rms_rows(x, w):
    ms = jnp.mean(x * x, axis=-1, keepdims=True)
    return x * lax.rsqrt(ms + NORM_EPS) * w


def _lane_block(shape, width):
    return lax.broadcasted_iota(jnp.int32, shape, len(shape) - 1) // width


def _group_mean_matrix(width, group):
    idx = np.arange(width) // group
    return jnp.asarray((idx[:, None] == idx[None, :]).astype(np.float32) / group, dtype=BF16)


def _rope_tables(n_tokens):
    pos = np.arange(n_tokens)
    row = (pos // GRID_W).astype(np.float64)
    col = (pos % GRID_W).astype(np.float64)
    freqs = ROPE_THETA ** (-np.arange(ROPE_HALF, dtype=np.float64) / ROPE_HALF)
    d = np.arange(HEAD_DIM)
    coord = np.where((d // ROPE_AXIS_DIM)[None, :] == 0, row[:, None], col[:, None])
    ang = coord * freqs[d % ROPE_HALF][None, :]
    sign = np.where((d & ROPE_HALF) == 0, -1.0, 1.0)[None, :]
    cos = np.tile(np.cos(ang), (1, ATTN_HEADS))
    sin = np.tile(np.sin(ang) * sign, (1, ATTN_HEADS))
    return jnp.asarray(cos, F32), jnp.asarray(sin, F32)


def _dft_tables(seq):
    n = np.arange(seq)
    ang = 2.0 * np.pi * ((n[:, None] * n[None, :]) % seq) / seq
    scale = 1.0 / np.sqrt(seq * FOURIER_DIM)
    c = np.arange(FOURIER_WIDTH)
    same = (c[:, None] // FOURIER_DIM) == (c[None, :] // FOURIER_DIM)
    angc = 2.0 * np.pi * (((c % FOURIER_DIM)[:, None] * (c % FOURIER_DIM)[None, :]) % FOURIER_DIM) / FOURIER_DIM
    as_bf16 = lambda a: jnp.asarray(a, F32).astype(BF16)
    return (as_bf16(np.cos(ang) * scale), as_bf16(-np.sin(ang) * scale),
            as_bf16(np.where(same, np.cos(angc), 0.0)), as_bf16(np.where(same, np.sin(angc), 0.0)))


def _retention_tables():
    heads = jnp.arange(RET_HEADS, dtype=F32)
    idx = jnp.arange(RET_CHUNK, dtype=F32)
    diff = idx[:, None] - idx[None, :]
    out = []
    for exp0, backward in ((RET_DECAY_EXP_FWD, False), (RET_DECAY_EXP_BWD, True)):
        lg = jnp.log1p(-jnp.exp2(-(exp0 + heads)))
        dd = -diff if backward else diff
        inner = jnp.where(dd[None] >= 0, jnp.exp(jnp.maximum(dd, 0.0)[None] * lg[:, None, None]), 0.0)
        inner = inner.transpose(1, 0, 2).reshape(RET_CHUNK, RET_HEADS * RET_CHUNK)
        q_pow = (RET_CHUNK - idx) if backward else (idx + 1.0)
        k_pow = idx if backward else (RET_CHUNK - 1.0 - idx)
        spread = lambda p: jnp.repeat(jnp.exp(p[:, None] * lg[None, :]), RET_DK, axis=1)
        q_decay, k_decay = spread(q_pow), spread(k_pow)
        chunk_decay = jnp.repeat(jnp.exp(RET_CHUNK * lg), RET_DV)[None, :]
        out.append((inner, q_decay, k_decay, chunk_decay))
    return out


def _mod_kernel(c_ref, w_ref, b_ref, o_ref):
    act = _silu(c_ref[...]).astype(BF16)
    o_ref[...] = _dot(act, w_ref[...].astype(BF16)) + b_ref[...]


def _modulation(cvec, w_mod, b_mod):
    n_cols = w_mod.shape[-1]
    return pl.pallas_call(
        _mod_kernel,
        out_shape=jax.ShapeDtypeStruct((DEPTH, MOD_ROWS, n_cols), F32),
        grid=(DEPTH, n_cols // MOD_COLS),
        in_specs=[
            pl.BlockSpec((MOD_ROWS, D_MODEL), lambda l, j: (0, 0)),
            pl.BlockSpec((None, D_MODEL, MOD_COLS), lambda l, j: (l, 0, j)),
            pl.BlockSpec((None, 1, MOD_COLS), lambda l, j: (l, 0, j)),
        ],
        out_specs=pl.BlockSpec((None, MOD_ROWS, MOD_COLS), lambda l, j: (l, 0, j)),
        compiler_params=_params("parallel", "parallel"),
        name="modulation",
    )(cvec, w_mod, b_mod.reshape(DEPTH, 1, n_cols))


def _swap_rotary_pairs(x):
    width = x.shape[-1]
    lane = lax.broadcasted_iota(jnp.int32, x.shape, 1)
    from_below = pltpu.roll(x, ROPE_HALF, 1)
    from_above = pltpu.roll(x, width - ROPE_HALF, 1)
    return jnp.where((lane & ROPE_HALF) != 0, from_below, from_above)


def _inproj_kernel(*refs, latent):
    x_ref, mod_ref, n1_ref, win_ref, qnw_ref, knw_ref, gm_ref = refs[:7]
    refs = refs[7:]
    if latent:
        cos_ref, sin_ref = refs[:2]
        refs = refs[2:]
        q_ref, k_ref, v_ref, rq_ref, rk_ref, rv_ref, g_ref, fx_ref = refs
    else:
        q_ref, k_ref, v_ref, rq_ref, rk_ref, rv_ref, g_ref, fx_ref, k32_ref, v32_ref = refs

    shift, scale = mod_ref[0:1, :], mod_ref[1:2, :]
    h = _rms_rows(x_ref[...], n1_ref[...]) * (1.0 + scale) + shift
    hb = h.astype(BF16)
    proj = lambda off, width: _dot(hb, win_ref[:, off:off + width])

    q = proj(OFF_Q, ATTN_WIDTH)
    q = q * lax.rsqrt(_dot((q * q).astype(BF16), gm_ref[...]) + NORM_EPS) * qnw_ref[...]
    k = proj(OFF_K, KV_WIDTH)
    k = k * lax.rsqrt(_dot((k * k).astype(BF16), gm_ref[0:KV_WIDTH, 0:KV_WIDTH]) + NORM_EPS) * knw_ref[...]
    v = proj(OFF_V, KV_WIDTH)
    if latent:
        cos, sin = cos_ref[...], sin_ref[...]
        q = q * cos + _swap_rotary_pairs(q) * sin
        k = k * cos[:, 0:KV_WIDTH] + _swap_rotary_pairs(k) * sin[:, 0:KV_WIDTH]
    else:
        k32_ref[...] = k
        v32_ref[...] = v
    q_ref[...] = (q * HEAD_DIM ** -0.5).astype(BF16)
    k_ref[...] = k.astype(BF16)
    v_ref[...] = v.astype(BF16)
    rq_ref[...] = proj(OFF_RQ, RET_WIDTH).astype(BF16)
    rk_ref[...] = (proj(OFF_RK, RET_WIDTH) * RET_DK ** -0.5).astype(BF16)
    rv_ref[...] = proj(OFF_RV, RET_WIDTH).astype(BF16)
    g_ref[...] = proj(OFF_GF, 2 * RET_WIDTH)
    fx_ref[...] = proj(OFF_FX, FOURIER_WIDTH).astype(BF16)


def _in_projection(x, mod, norm1_w, w_in, q_norm_w, k_norm_w, group_mean, rope, seq):
    n = x.shape[0]
    rows = INPROJ_ROWS
    latent = rope is not None
    row_blk = lambda width: pl.BlockSpec((rows, width), lambda i: (i, 0))
    whole = lambda shape: pl.BlockSpec(shape, lambda i: (0,) * len(shape))
    mod_idx = (lambda i: (i * rows // seq, 0, 0)) if latent else (lambda i: (0, 0, 0))
    in_specs = [
        row_blk(D_MODEL),
        pl.BlockSpec((None, N_MOD, D_MODEL), mod_idx),
        whole((1, D_MODEL)),
        whole((D_MODEL, IN_WIDTH)),
        whole((1, ATTN_WIDTH)),
        whole((1, KV_WIDTH)),
        whole((ATTN_WIDTH, ATTN_WIDTH)),
    ]
    args = [x, mod, norm1_w, w_in, q_norm_w, k_norm_w, group_mean]
    out_widths = [(ATTN_WIDTH, BF16), (KV_WIDTH, BF16), (KV_WIDTH, BF16), (RET_WIDTH, BF16), (RET_WIDTH, BF16),
                  (RET_WIDTH, BF16), (2 * RET_WIDTH, F32), (FOURIER_WIDTH, BF16)]
    if latent:
        pos_blk = pl.BlockSpec((rows, ATTN_WIDTH), lambda i: (i % (seq // rows), 0))
        in_specs += [pos_blk, pos_blk]
        args += list(rope)
    else:
        out_widths += [(KV_WIDTH, F32), (KV_WIDTH, F32)]
    return pl.pallas_call(
        functools.partial(_inproj_kernel, latent=latent),
        out_shape=[jax.ShapeDtypeStruct((n, w), dt) for w, dt in out_widths],
        grid=(n // rows,),
        in_specs=in_specs,
        out_specs=[row_blk(w) for w, _ in out_widths],
        compiler_params=_params("parallel"),
        name="in_projection_latent" if latent else "in_projection_context",
    )(*args)


def _spread_kv(x, group):
    lane = lax.broadcasted_iota(jnp.int32, x.shape, 1)
    other = pltpu.roll(x, HEAD_DIM, 1)
    own = (lane // HEAD_DIM) == group
    pair = jnp.where(own, x, other).astype(BF16)
    return jnp.concatenate([pair, pair], axis=1)


def _attention_kernel(*refs, past):
    if past:
        q_ref, k_ref, v_ref, ck_ref, cv_ref, o_ref, kt_ref, vt_ref = refs
    else:
        q_ref, k_ref, v_ref, o_ref, kt_ref, vt_ref = refs

    @pl.when(pl.program_id(1) == 0)
    def _():
        for g in range(ATTN_KV_HEADS):
            if past:
                kt_ref[g, 0:past, :] = _spread_kv(ck_ref[...], g)
                vt_ref[g, 0:past, :] = _spread_kv(cv_ref[...], g)
            kt_ref[g, past:, :] = _spread_kv(k_ref[...].astype(F32), g)
            vt_ref[g, past:, :] = _spread_kv(v_ref[...].astype(F32), g)

    rows = q_ref.shape[0]
    block = _lane_block((rows, GROUP_WIDTH), HEAD_DIM)
    for g in range(ATTN_KV_HEADS):
        qg = q_ref[:, g * GROUP_WIDTH:(g + 1) * GROUP_WIDTH]
        out = jnp.zeros((rows, GROUP_WIDTH), F32)
        for h in range(HEADS_PER_KV):
            qh = jnp.where(block == h, qg, jnp.zeros_like(qg))
            s = _dot_nt(qh, kt_ref[g])
            p = jnp.exp(s - jnp.max(s, axis=-1, keepdims=True))
            denom = jnp.sum(p, axis=-1, keepdims=True)
            o = _dot(p.astype(BF16), vt_ref[g])
            out = jnp.where(block == h, o / denom, out)
        o_ref[:, g * GROUP_WIDTH:(g + 1) * GROUP_WIDTH] = out.astype(BF16)


def _attention(q, k, v, cache, layer, batch, seq):
    n = q.shape[0]
    tq = ATTN_Q_ROWS
    nq = seq // tq
    past = 0 if cache is None else cache[0].shape[2]
    own_kv = pl.BlockSpec((seq, KV_WIDTH), lambda b, i: (b, 0))
    in_specs = [pl.BlockSpec((tq, ATTN_WIDTH), lambda b, i: (b * nq + i, 0)), own_kv, own_kv]
    args = [q, k, v]
    if past:
        cached_kv = pl.BlockSpec((None, None, past, KV_WIDTH), lambda b, i: (b, layer, 0, 0))
        in_specs += [cached_kv, cached_kv]
        args += list(cache)
    return pl.pallas_call(
        functools.partial(_attention_kernel, past=past),
        out_shape=jax.ShapeDtypeStruct((n, ATTN_WIDTH), BF16),
        grid=(batch, nq),
        in_specs=in_specs,
        out_specs=pl.BlockSpec((tq, ATTN_WIDTH), lambda b, i: (b * nq + i, 0)),
        scratch_shapes=[pltpu.VMEM((ATTN_KV_HEADS, past + seq, GROUP_WIDTH), BF16)] * 2,
        compiler_params=_params("parallel", "arbitrary"),
        name="attention_latent" if past else "attention_context",
    )(*args)


def _stack_heads(x):
    block = _lane_block(x.shape, RET_DK)
    zero = jnp.zeros_like(x)
    return jnp.concatenate([jnp.where(block == h, x, zero) for h in range(RET_HEADS)], axis=0)


def _head_norm(o, gm):
    mu = _dot(o.astype(BF16), gm)
    d = o - mu
    var = _dot((d * d).astype(BF16), gm)
    return d * lax.rsqrt(var + NORM_EPS)


def _retention_kernel(*refs, has_state, n_chunks):
    rq_ref, rk_ref, rv_ref, g_ref, gm_ref = refs[:5]
    (df_ref, qdf_ref, kdf_ref, cdf_ref, db_ref, qdb_ref, kdb_ref, cdb_ref) = refs[5:13]
    refs = refs[13:]
    if has_state:
        s0_ref, o_ref, part_ref, intra_ref, upd_ref = refs
    else:
        o_ref, sfin_ref, part_ref, intra_ref, upd_ref = refs

    width = RET_WIDTH
    diag = (lax.broadcasted_iota(jnp.int32, (width, width), 0) // RET_DK
            == lax.broadcasted_iota(jnp.int32, (width, width), 1) // RET_DV)
    gm = gm_ref[...]

    def load_state(direction):
        if not has_state:
            return jnp.zeros((width, width), F32)
        s = s0_ref[direction].reshape(width, RET_DV)
        return jnp.where(diag, jnp.concatenate([s] * RET_HEADS, axis=1), 0.0)

    def store_state(direction, s):
        folded = s[:, 0:width // 2] + s[:, width // 2:]
        sfin_ref[direction] = folded[:, 0:RET_DV] + folded[:, RET_DV:]

    state = load_state(0)
    for c in range(n_chunks):
        rows = pl.ds(c * RET_CHUNK, RET_CHUNK)
        qc, kc, vc = rq_ref[rows, :], rk_ref[rows, :], rv_ref[rows, :]
        q32, k32 = qc.astype(F32), kc.astype(F32)
        scores = _dot_nt(qc, _stack_heads(kc))
        v_heads = _stack_heads(vc)
        intra_ref[rows, :] = _dot((scores * db_ref[...]).astype(BF16), v_heads)
        upd_ref[c] = jnp.where(diag, _dot((k32 * kdb_ref[...]).T.astype(BF16), vc), 0.0)
        o = (_dot((scores * df_ref[...]).astype(BF16), v_heads)
             + _dot((q32 * qdf_ref[...]).astype(BF16), state.astype(BF16)))
        state = cdf_ref[...] * state + jnp.where(diag, _dot((k32 * kdf_ref[...]).T.astype(BF16), vc), 0.0)
        part_ref[rows, :] = _silu(g_ref[rows, 0:width]) * _head_norm(o, gm)
    if not has_state:
        store_state(0, state)

    state = load_state(1)
    for c in reversed(range(n_chunks)):
        rows = pl.ds(c * RET_CHUNK, RET_CHUNK)
        q32 = rq_ref[rows, :].astype(F32)
        o = intra_ref[rows, :] + _dot((q32 * qdb_ref[...]).astype(BF16), state.astype(BF16))
        state = cdb_ref[...] * state + upd_ref[c]
        o_ref[rows, :] = (part_ref[rows, :] + _silu(g_ref[rows, width:]) * _head_norm(o, gm)).astype(BF16)
    if not has_state:
        store_state(1, state)


def _retention(rq, rk, rv, gates, group_mean, tables, state, layer, batch, seq):
    n = rq.shape[0]
    has_state = state is not None
    n_chunks = seq // RET_CHUNK
    seq_blk = lambda width: pl.BlockSpec((seq, width), lambda b: (b, 0))
    whole = lambda a: pl.BlockSpec(a.shape, lambda b: (0,) * a.ndim)
    flat_tables = [t for direction in tables for t in direction]
    in_specs = [seq_blk(RET_WIDTH)] * 3 + [seq_blk(2 * RET_WIDTH), whole(group_mean)] + [whole(t) for t in flat_tables]
    args = [rq, rk, rv, gates, group_mean] + flat_tables
    out_shape = [jax.ShapeDtypeStruct((n, RET_WIDTH), BF16)]
    out_specs = [seq_blk(RET_WIDTH)]
    if has_state:
        in_specs.append(pl.BlockSpec((None, None, 2, RET_HEADS, RET_DK, RET_DV), lambda b: (b, layer, 0, 0, 0, 0)))
        args.append(state)
    else:
        out_shape.append(jax.ShapeDtypeStruct((batch, 2, RET_HEADS * RET_DK, RET_DV), F32))
        out_specs.append(pl.BlockSpec((None, 2, RET_HEADS * RET_DK, RET_DV), lambda b: (b, 0, 0, 0)))
    return pl.pallas_call(
        functools.partial(_retention_kernel, has_state=has_state, n_chunks=n_chunks),
        out_shape=out_shape,
        grid=(batch,),
        in_specs=in_specs,
        out_specs=out_specs,
        scratch_shapes=[pltpu.VMEM((seq, RET_WIDTH), F32), pltpu.VMEM((seq, RET_WIDTH), F32),
                        pltpu.VMEM((n_chunks, RET_WIDTH, RET_WIDTH), F32)],
        compiler_params=_params("parallel"),
        name="retention_latent" if has_state else "retention_context",
    )(*args)


def _fourier_kernel(f_ref, cos_ref, nsin_ref, ccos_ref, csin_ref, o_ref):
    f = f_ref[...]
    along_c = _dot(f, ccos_ref[...]).astype(BF16)
    along_s = _dot(f, csin_ref[...]).astype(BF16)
    o_ref[...] = (_dot(cos_ref[...], along_c) + _dot(nsin_ref[...], along_s)).astype(BF16)


def _fourier(fx, tables, batch, seq):
    n = fx.shape[0]
    whole = lambda a: pl.BlockSpec(a.shape, lambda b: (0,) * a.ndim)
    seq_blk = pl.BlockSpec((seq, FOURIER_WIDTH), lambda b: (b, 0))
    return pl.pallas_call(
        _fourier_kernel,
        out_shape=jax.ShapeDtypeStruct((n, FOURIER_WIDTH), BF16),
        grid=(batch,),
        in_specs=[seq_blk] + [whole(t) for t in tables],
        out_specs=seq_blk,
        compiler_params=_params("parallel"),
        name="fourier_seq%d" % seq,
    )(fx, *tables)


def _shift_rows(u, seq):
    rows = u.shape[0]
    pos = lax.broadcasted_iota(jnp.int32, u.shape, 0) % seq
    prev = jnp.where(pos == 0, 0.0, pltpu.roll(u, 1, 0))
    nxt = jnp.where(pos == seq - 1, 0.0, pltpu.roll(u, rows - 1, 0))
    return prev, nxt


def _ffn_kernel(x_ref, attn_ref, ret_ref, four_ref, mod_ref, wout_ref, n2_ref, wup_ref, cw_ref, cb_ref, wd_ref, fin_ref,
                o_ref, h2_ref, acc_ref, *, seq, final):
    j = pl.program_id(1)

    @pl.when(j == 0)
    def _():
        mixed = (_dot(attn_ref[...], wout_ref[0:ATTN_WIDTH, :])
                 + _dot(ret_ref[...], wout_ref[ATTN_WIDTH:ATTN_WIDTH + RET_WIDTH, :])
                 + _dot(four_ref[...], wout_ref[ATTN_WIDTH + RET_WIDTH:, :]))
        x1 = x_ref[...] + mod_ref[2:3, :] * mixed
        o_ref[...] = x1
        h2 = _rms_rows(x1, n2_ref[...]) * (1.0 + mod_ref[4:5, :]) + mod_ref[3:4, :]
        h2_ref[...] = h2.astype(BF16)
        acc_ref[...] = jnp.zeros_like(acc_ref)

    u = _dot(h2_ref[...], wup_ref[...])
    prev, nxt = _shift_rows(u, seq)
    u = prev * cw_ref[0:1, :] + u * cw_ref[1:2, :] + nxt * cw_ref[2:3, :] + cb_ref[...]
    act = _silu(u[:, 0:FFN_COLS]) * u[:, FFN_COLS:]
    acc_ref[...] += _dot(act.astype(BF16), wd_ref[...])

    @pl.when(j == pl.num_programs(1) - 1)
    def _():
        y = o_ref[...] + mod_ref[5:6, :] * acc_ref[...]
        o_ref[...] = _rms_rows(y, fin_ref[...]) if final else y


def _gate_value_blocks(a):
    lead = a.shape[:-1]
    a = a.reshape(lead + (2, D_FF // FFN_COLS, FFN_COLS))
    a = jnp.moveaxis(a, -2, 0)
    return a.reshape((D_FF // FFN_COLS,) + lead + (2 * FFN_COLS,))


def _out_ffn(x, attn, ret, four, mod, w_out, norm2_w, w_up, conv_w, conv_b, w_down, final_w, seq, per_seq_mod, final):
    n = x.shape[0]
    rows, cols = FFN_ROWS, FFN_COLS
    n_col_blocks = D_FF // cols
    row_blk = lambda width: pl.BlockSpec((rows, width), lambda i, j: (i, 0))
    whole = lambda shape: pl.BlockSpec(shape, lambda i, j: (0,) * len(shape))
    mod_idx = (lambda i, j: (i * rows // seq, 0, 0)) if per_seq_mod else (lambda i, j: (0, 0, 0))
    col_blk = lambda r: pl.BlockSpec((None, r, 2 * cols), lambda i, j: (j, 0, 0))
    in_specs = [
        row_blk(D_MODEL), row_blk(ATTN_WIDTH), row_blk(RET_WIDTH), row_blk(FOURIER_WIDTH),
        pl.BlockSpec((None, N_MOD, D_MODEL), mod_idx),
        whole((D_MODEL, D_MODEL)), whole((1, D_MODEL)),
        col_blk(D_MODEL), col_blk(3), col_blk(1),
        pl.BlockSpec((cols, D_MODEL), lambda i, j: (j, 0)),
        whole((1, D_MODEL)),
    ]
    return pl.pallas_call(
        functools.partial(_ffn_kernel, seq=seq, final=final),
        out_shape=jax.ShapeDtypeStruct((n, D_MODEL), F32),
        grid=(n // rows, n_col_blocks),
        in_specs=in_specs,
        out_specs=row_blk(D_MODEL),
        scratch_shapes=[pltpu.VMEM((rows, D_MODEL), BF16), pltpu.VMEM((rows, D_MODEL), F32)],
        compiler_params=_params("parallel", "arbitrary"),
        name="out_ffn_seq%d" % seq,
    )(x, attn, ret, four, mod, w_out, norm2_w, w_up, conv_w, conv_b, w_down, final_w)


def _layer(x, mod, lp, consts, layer, batch, seq, ctx, final):
    latent = ctx is not None
    outs = _in_projection(x, mod, lp["norm1_w"], lp["w_in"], lp["q_norm_w"], lp["k_norm_w"], consts["gm_attn"],
                          consts["rope"] if latent else None, seq)
    q, k, v, rq, rk, rv, gates, fx = outs[:8]
    attn = _attention(q, k, v, ctx[:2] if latent else None, layer, batch, seq)
    ret_out = _retention(rq, rk, rv, gates, consts["gm_ret"], consts["ret_tables"], ctx[2] if latent else None,
                         layer, batch, seq)
    four = _fourier(fx, consts["dft"][seq], batch, seq)
    x = _out_ffn(x, attn, ret_out[0], four, mod, lp["w_out"], lp["norm2_w"], lp["w_up"], lp["conv_w"], lp["conv_b"],
                 lp["w_down"], consts["final_w"], seq, latent, final)
    if latent:
        return x, None
    return x, (outs[8], outs[9], ret_out[1])


def kernel(x_prompt, x_sample, c, cache_attn_k, cache_attn_v, state_ret, c_ctx, w_mod, b_mod, norm1_w, w_in,
           q_norm_w, k_norm_w, w_out, norm2_w, w_up, conv_w, conv_b, w_down, final_norm_w):
    batch, seq, d = x_prompt.shape
    dec_batch, dec_seq, _ = x_sample.shape
    past = cache_attn_k.shape[2]
    assert d == D_MODEL and w_in.shape == (DEPTH, D_MODEL, IN_WIDTH) and w_up.shape == (DEPTH, D_MODEL, 2 * D_FF)
    assert (batch * seq) % FFN_ROWS == 0 and FFN_ROWS % seq == 0 and dec_seq % FFN_ROWS == 0
    assert seq % ATTN_Q_ROWS == 0 and dec_seq % ATTN_Q_ROWS == 0
    assert INPROJ_ROWS % seq == 0 and dec_seq % INPROJ_ROWS == 0 and seq % RET_CHUNK == 0 and dec_seq % RET_CHUNK == 0
    assert dec_batch + 1 <= MOD_ROWS and dec_seq % GRID_W == 0

    consts = {
        "gm_attn": _group_mean_matrix(ATTN_WIDTH, HEAD_DIM),
        "gm_ret": _group_mean_matrix(RET_WIDTH, RET_DV),
        "rope": _rope_tables(dec_seq),
        "ret_tables": _retention_tables(),
        "dft": {s: _dft_tables(s) for s in {seq, dec_seq}},
        "final_w": final_norm_w.reshape(1, D_MODEL),
    }

    cvec = jnp.zeros((MOD_ROWS, D_MODEL), F32).at[0].set(c_ctx).at[1:1 + dec_batch].set(c)
    mod = _modulation(cvec, w_mod, b_mod).reshape(DEPTH, MOD_ROWS, N_MOD, D_MODEL)

    cache_k = cache_attn_k.reshape(dec_batch, DEPTH, past, KV_WIDTH)
    cache_v = cache_attn_v.reshape(dec_batch, DEPTH, past, KV_WIDTH)

    xp = x_prompt.reshape(batch * seq, D_MODEL)
    xs = x_sample.reshape(dec_batch * dec_seq, D_MODEL)
    new_k, new_v, new_s = [], [], []
    for layer in range(DEPTH):
        lp = {
            "norm1_w": norm1_w[layer].reshape(1, D_MODEL),
            "w_in": w_in[layer].astype(BF16),
            "q_norm_w": jnp.tile(q_norm_w[layer], ATTN_HEADS).reshape(1, ATTN_WIDTH),
            "k_norm_w": jnp.tile(k_norm_w[layer], ATTN_KV_HEADS).reshape(1, KV_WIDTH),
            "w_out": w_out[layer].astype(BF16),
            "norm2_w": norm2_w[layer].reshape(1, D_MODEL),
            "w_up": _gate_value_blocks(w_up[layer].astype(BF16)),
            "conv_w": _gate_value_blocks(conv_w[layer]),
            "conv_b": _gate_value_blocks(conv_b[layer].reshape(1, 2 * D_FF)),
            "w_down": w_down[layer].astype(BF16),
        }
        final = layer == DEPTH - 1
        xp, (k_l, v_l, s_l) = _layer(xp, mod[layer, 0:1], lp, consts, layer, batch, seq, None, final)
        new_k.append(k_l.reshape(batch, seq, ATTN_KV_HEADS, HEAD_DIM))
        new_v.append(v_l.reshape(batch, seq, ATTN_KV_HEADS, HEAD_DIM))
        new_s.append(s_l.reshape(batch, 2, RET_HEADS, RET_DK, RET_DV))
        xs, _ = _layer(xs, mod[layer, 1:1 + dec_batch], lp, consts, layer, dec_batch, dec_seq,
                       (cache_k, cache_v, state_ret), final)
    return (xp.reshape(batch, seq, D_MODEL), xs.reshape(dec_batch, dec_seq, D_MODEL),
            jnp.stack(new_k, axis=1), jnp.stack(new_v, axis=1), jnp.stack(new_s, axis=1))
```

```python
import functools

import jax
import jax.numpy as jnp
import numpy as np
from jax import lax
from jax.experimental import pallas as pl
from jax.experimental.pallas import tpu as pltpu

F32 = jnp.float32
BF16 = jnp.bfloat16

D_MODEL = 1024
DEPTH = 2
GRID_W = 64
NORM_EPS = 1e-6
ATTN_HEADS = 8
ATTN_KV_HEADS = 2
HEAD_DIM = 64
ATTN_WIDTH = ATTN_HEADS * HEAD_DIM
KV_WIDTH = ATTN_KV_HEADS * HEAD_DIM
HEADS_PER_KV = ATTN_HEADS // ATTN_KV_HEADS
GROUP_WIDTH = HEADS_PER_KV * HEAD_DIM
ROPE_THETA = 10000.0
ROPE_AXIS_DIM = HEAD_DIM // 2
ROPE_HALF = ROPE_AXIS_DIM // 2
RET_HEADS = 4
RET_DK = 64
RET_DV = 64
RET_WIDTH = RET_HEADS * RET_DV
RET_CHUNK = 128
RET_DECAY_EXP_FWD = 5.0
RET_DECAY_EXP_BWD = 5.5
FOURIER_GROUPS = 4
FOURIER_DIM = 64
FOURIER_WIDTH = FOURIER_GROUPS * FOURIER_DIM
D_FF = 2816
N_MOD = 6

OFF_Q = 0
OFF_K = OFF_Q + ATTN_WIDTH
OFF_V = OFF_K + KV_WIDTH
OFF_RQ = OFF_V + KV_WIDTH
OFF_RK = OFF_RQ + RET_WIDTH
OFF_RV = OFF_RK + RET_WIDTH
OFF_GF = OFF_RV + RET_WIDTH
OFF_GB = OFF_GF + RET_WIDTH
OFF_FX = OFF_GB + RET_WIDTH
IN_WIDTH = OFF_FX + FOURIER_WIDTH

V7X_VMEM_BYTES = 64 * 1024 * 1024
VMEM_LIMIT = V7X_VMEM_BYTES - 12 * 1024 * 1024

MOD_ROWS = 16
MOD_COLS = 1536
INPROJ_ROWS = 512
ATTN_Q_ROWS = 256
OUTPROJ_ROWS = 1024
FFN_ROWS = 512
F32_SUBLANES = 8


def _params(*semantics):
    return pltpu.CompilerParams(dimension_semantics=semantics, vmem_limit_bytes=VMEM_LIMIT)


def _dot(a, b):
    return jnp.dot(a, b, preferred_element_type=F32)


def _dot_nt(a, b):
    return lax.dot_general(a, b, (((1,), (1,)), ((), ())), preferred_element_type=F32)


def _sigmoid(x):
    return 1.0 / (1.0 + jnp.exp(-x))


def _silu(x):
    return x * _sigmoid(x)


def _rms_rows(x, w):
    ms = jnp.mean(x * x, axis=-1, keepdims=True)
    return x * lax.rsqrt(ms + NORM_EPS) * w


def _lane_block(shape, width):
    return lax.broadcasted_iota(jnp.int32, shape, len(shape) - 1) // width


def _group_mean_matrix(width, group):
    idx = np.arange(width) // group
    return jnp.asarray((idx[:, None] == idx[None, :]).astype(np.float32) / group, dtype=BF16)


def _rope_tables(n_tokens):
    pos = np.arange(n_tokens)
    row = (pos // GRID_W).astype(np.float64)
    col = (pos % GRID_W).astype(np.float64)
    freqs = ROPE_THETA ** (-np.arange(ROPE_HALF, dtype=np.float64) / ROPE_HALF)
    d = np.arange(HEAD_DIM)
    coord = np.where((d // ROPE_AXIS_DIM)[None, :] == 0, row[:, None], col[:, None])
    ang = coord * freqs[d % ROPE_HALF][None, :]
    sign = np.where((d & ROPE_HALF) == 0, -1.0, 1.0)[None, :]
    cos = np.tile(np.cos(ang), (1, ATTN_HEADS))
    sin = np.tile(np.sin(ang) * sign, (1, ATTN_HEADS))
    return jnp.asarray(cos, F32), jnp.asarray(sin, F32)


def _dft_tables(seq):
    n = np.arange(seq)
    ang = 2.0 * np.pi * ((n[:, None] * n[None, :]) % seq) / seq
    scale = 1.0 / np.sqrt(seq * FOURIER_DIM)
    c = np.arange(FOURIER_WIDTH)
    same = (c[:, None] // FOURIER_DIM) == (c[None, :] // FOURIER_DIM)
    angc = 2.0 * np.pi * (((c % FOURIER_DIM)[:, None] * (c % FOURIER_DIM)[None, :]) % FOURIER_DIM) / FOURIER_DIM
    as_bf16 = lambda a: jnp.asarray(a, F32).astype(BF16)
    return (as_bf16(np.cos(ang) * scale), as_bf16(-np.sin(ang) * scale),
            as_bf16(np.where(same, np.cos(angc), 0.0)), as_bf16(np.where(same, np.sin(angc), 0.0)))


def _retention_tables():
    heads = jnp.arange(RET_HEADS, dtype=F32)
    idx = jnp.arange(RET_CHUNK, dtype=F32)
    diff = idx[:, None] - idx[None, :]
    out = []
    for exp0, backward in ((RET_DECAY_EXP_FWD, False), (RET_DECAY_EXP_BWD, True)):
        lg = jnp.log1p(-jnp.exp2(-(exp0 + heads)))
        dd = -diff if backward else diff
        inner = jnp.where(dd[None] >= 0, jnp.exp(jnp.maximum(dd, 0.0)[None] * lg[:, None, None]), 0.0)
        inner = inner.transpose(1, 0, 2).reshape(RET_CHUNK, RET_HEADS * RET_CHUNK)
        q_pow = (RET_CHUNK - idx) if backward else (idx + 1.0)
        k_pow = idx if backward else (RET_CHUNK - 1.0 - idx)
        spread = lambda p: jnp.repeat(jnp.exp(p[:, None] * lg[None, :]), RET_DK, axis=1)
        q_decay, k_decay = spread(q_pow), spread(k_pow)
        chunk_decay = jnp.repeat(jnp.exp(RET_CHUNK * lg), RET_DV)[None, :]
        out.append((inner, q_decay, k_decay, chunk_decay))
    return out


def _mod_kernel(c_ref, w_ref, b_ref, o_ref):
    act = _silu(c_ref[...]).astype(BF16)
    o_ref[...] = _dot(act, w_ref[...].astype(BF16)) + b_ref[...]


def _modulation(cvec, w_mod, b_mod):
    n_cols = w_mod.shape[-1]
    return pl.pallas_call(
        _mod_kernel,
        out_shape=jax.ShapeDtypeStruct((DEPTH, MOD_ROWS, n_cols), F32),
        grid=(DEPTH, n_cols // MOD_COLS),
        in_specs=[
            pl.BlockSpec((MOD_ROWS, D_MODEL), lambda l, j: (0, 0)),
            pl.BlockSpec((None, D_MODEL, MOD_COLS), lambda l, j: (l, 0, j)),
            pl.BlockSpec((None, 1, MOD_COLS), lambda l, j: (l, 0, j)),
        ],
        out_specs=pl.BlockSpec((None, MOD_ROWS, MOD_COLS), lambda l, j: (l, 0, j)),
        compiler_params=_params("parallel", "parallel"),
        name="modulation",
    )(cvec, w_mod, b_mod.reshape(DEPTH, 1, n_cols))


def _swap_rotary_pairs(x):
    width = x.shape[-1]
    lane = lax.broadcasted_iota(jnp.int32, x.shape, 1)
    from_below = pltpu.roll(x, ROPE_HALF, 1)
    from_above = pltpu.roll(x, width - ROPE_HALF, 1)
    return jnp.where((lane & ROPE_HALF) != 0, from_below, from_above)


def _inproj_kernel(*refs, latent):
    x_ref, mod_ref, n1_ref, win_ref, qnw_ref, knw_ref, gm_ref = refs[:7]
    refs = refs[7:]
    if latent:
        cos_ref, sin_ref = refs[:2]
        refs = refs[2:]
        q_ref, k_ref, v_ref, rq_ref, rk_ref, rv_ref, g_ref, fx_ref = refs
    else:
        q_ref, k_ref, v_ref, rq_ref, rk_ref, rv_ref, g_ref, fx_ref, k32_ref, v32_ref = refs

    shift, scale = mod_ref[0:1, :], mod_ref[1:2, :]
    h = _rms_rows(x_ref[...], n1_ref[...]) * (1.0 + scale) + shift
    hb = h.astype(BF16)
    proj = lambda off, width: _dot(hb, win_ref[:, off:off + width])

    q = proj(OFF_Q, ATTN_WIDTH)
    q = q * lax.rsqrt(_dot((q * q).astype(BF16), gm_ref[...]) + NORM_EPS) * qnw_ref[...]
    k = proj(OFF_K, KV_WIDTH)
    k = k * lax.rsqrt(_dot((k * k).astype(BF16), gm_ref[0:KV_WIDTH, 0:KV_WIDTH]) + NORM_EPS) * knw_ref[...]
    v = proj(OFF_V, KV_WIDTH)
    if latent:
        cos, sin = cos_ref[...], sin_ref[...]
        q = q * cos + _swap_rotary_pairs(q) * sin
        k = k * cos[:, 0:KV_WIDTH] + _swap_rotary_pairs(k) * sin[:, 0:KV_WIDTH]
    else:
        k32_ref[...] = k
        v32_ref[...] = v
    q_ref[...] = (q * HEAD_DIM ** -0.5).astype(BF16)
    k_ref[...] = k.astype(BF16)
    v_ref[...] = v.astype(BF16)
    rq_ref[...] = proj(OFF_RQ, RET_WIDTH).astype(BF16)
    rk_ref[...] = (proj(OFF_RK, RET_WIDTH) * RET_DK ** -0.5).astype(BF16)
    rv_ref[...] = proj(OFF_RV, RET_WIDTH).astype(BF16)
    g_ref[...] = proj(OFF_GF, 2 * RET_WIDTH)
    fx_ref[...] = proj(OFF_FX, FOURIER_WIDTH).astype(BF16)


def _in_projection(x, mod, norm1_w, w_in, q_norm_w, k_norm_w, group_mean, rope, layer, seq):
    n = x.shape[0]
    rows = INPROJ_ROWS
    latent = rope is not None
    row_blk = lambda width: pl.BlockSpec((rows, width), lambda i: (i, 0))
    whole = lambda shape: pl.BlockSpec(shape, lambda i: (0,) * len(shape))
    mod_idx = (lambda i: (i * rows // seq, 0, 0)) if latent else (lambda i: (0, 0, 0))
    in_specs = [
        row_blk(D_MODEL),
        pl.BlockSpec((None, N_MOD, D_MODEL), mod_idx),
        whole((1, D_MODEL)),
        pl.BlockSpec((None, D_MODEL, IN_WIDTH), lambda i: (layer, 0, 0)),
        whole((1, ATTN_WIDTH)),
        whole((1, KV_WIDTH)),
        whole((ATTN_WIDTH, ATTN_WIDTH)),
    ]
    args = [x, mod, norm1_w, w_in, q_norm_w, k_norm_w, group_mean]
    out_widths = [(ATTN_WIDTH, BF16), (KV_WIDTH, BF16), (KV_WIDTH, BF16), (RET_WIDTH, BF16), (RET_WIDTH, BF16),
                  (RET_WIDTH, BF16), (2 * RET_WIDTH, F32), (FOURIER_WIDTH, BF16)]
    if latent:
        pos_blk = pl.BlockSpec((rows, ATTN_WIDTH), lambda i: (i % (seq // rows), 0))
        in_specs += [pos_blk, pos_blk]
        args += list(rope)
    else:
        out_widths += [(KV_WIDTH, F32), (KV_WIDTH, F32)]
    return pl.pallas_call(
        functools.partial(_inproj_kernel, latent=latent),
        out_shape=[jax.ShapeDtypeStruct((n, w), dt) for w, dt in out_widths],
        grid=(n // rows,),
        in_specs=in_specs,
        out_specs=[row_blk(w) for w, _ in out_widths],
        compiler_params=_params("parallel"),
        name="in_projection_latent" if latent else "in_projection_context",
    )(*args)


def _spread_kv(x, group):
    lane = lax.broadcasted_iota(jnp.int32, x.shape, 1)
    other = pltpu.roll(x, HEAD_DIM, 1)
    own = (lane // HEAD_DIM) == group
    pair = jnp.where(own, x, other).astype(BF16)
    return jnp.concatenate([pair, pair], axis=1)


def _attention_kernel(*refs, past):
    if past:
        q_ref, k_ref, v_ref, ck_ref, cv_ref, o_ref, kt_ref, vt_ref = refs
    else:
        q_ref, k_ref, v_ref, o_ref, kt_ref, vt_ref = refs

    @pl.when(pl.program_id(1) == 0)
    def _():
        for g in range(ATTN_KV_HEADS):
            if past:
                kt_ref[g, 0:past, :] = _spread_kv(ck_ref[...], g)
                vt_ref[g, 0:past, :] = _spread_kv(cv_ref[...], g)
            kt_ref[g, past:, :] = _spread_kv(k_ref[...].astype(F32), g)
            vt_ref[g, past:, :] = _spread_kv(v_ref[...].astype(F32), g)

    rows = q_ref.shape[0]
    block = _lane_block((rows, GROUP_WIDTH), HEAD_DIM)
    for g in range(ATTN_KV_HEADS):
        qg = q_ref[:, g * GROUP_WIDTH:(g + 1) * GROUP_WIDTH]
        out = jnp.zeros((rows, GROUP_WIDTH), F32)
        for h in range(HEADS_PER_KV):
            qh = jnp.where(block == h, qg, jnp.zeros_like(qg))
            s = _dot_nt(qh, kt_ref[g])
            p = jnp.exp(s - jnp.max(s, axis=-1, keepdims=True))
            denom = jnp.sum(p, axis=-1, keepdims=True)
            o = _dot(p.astype(BF16), vt_ref[g])
            out = jnp.where(block == h, o / denom, out)
        o_ref[:, g * GROUP_WIDTH:(g + 1) * GROUP_WIDTH] = out.astype(BF16)


def _attention(q, k, v, cache, layer, batch, seq):
    n = q.shape[0]
    tq = ATTN_Q_ROWS
    nq = seq // tq
    past = 0 if cache is None else cache[0].shape[2]
    own_kv = pl.BlockSpec((seq, KV_WIDTH), lambda b, i: (b, 0))
    in_specs = [pl.BlockSpec((tq, ATTN_WIDTH), lambda b, i: (b * nq + i, 0)), own_kv, own_kv]
    args = [q, k, v]
    if past:
        cached_kv = pl.BlockSpec((None, None, past, KV_WIDTH), lambda b, i: (b, layer, 0, 0))
        in_specs += [cached_kv, cached_kv]
        args += list(cache)
    return pl.pallas_call(
        functools.partial(_attention_kernel, past=past),
        out_shape=jax.ShapeDtypeStruct((n, ATTN_WIDTH), BF16),
        grid=(batch, nq),
        in_specs=in_specs,
        out_specs=pl.BlockSpec((tq, ATTN_WIDTH), lambda b, i: (b * nq + i, 0)),
        scratch_shapes=[pltpu.VMEM((ATTN_KV_HEADS, past + seq, GROUP_WIDTH), BF16)] * 2,
        compiler_params=_params("parallel", "arbitrary"),
        name="attention_latent" if past else "attention_context",
    )(*args)


def _stack_heads(x):
    block = _lane_block(x.shape, RET_DK)
    zero = jnp.zeros_like(x)
    return jnp.concatenate([jnp.where(block == h, x, zero) for h in range(RET_HEADS)], axis=0)


def _head_norm(o, gm):
    mu = _dot(o.astype(BF16), gm)
    d = o - mu
    var = _dot((d * d).astype(BF16), gm)
    return d * lax.rsqrt(var + NORM_EPS)


def _retention_kernel(*refs, has_state, n_chunks):
    rq_ref, rk_ref, rv_ref, g_ref, gm_ref = refs[:5]
    (df_ref, qdf_ref, kdf_ref, cdf_ref, db_ref, qdb_ref, kdb_ref, cdb_ref) = refs[5:13]
    refs = refs[13:]
    if has_state:
        s0_ref, o_ref, part_ref, intra_ref, upd_ref = refs
    else:
        o_ref, sfin_ref, part_ref, intra_ref, upd_ref = refs

    width = RET_WIDTH
    diag = (lax.broadcasted_iota(jnp.int32, (width, width), 0) // RET_DK
            == lax.broadcasted_iota(jnp.int32, (width, width), 1) // RET_DV)
    gm = gm_ref[...]

    def load_state(direction):
        if not has_state:
            return jnp.zeros((width, width), F32)
        s = s0_ref[direction].reshape(width, RET_DV)
        return jnp.where(diag, jnp.concatenate([s] * RET_HEADS, axis=1), 0.0)

    def store_state(direction, s):
        folded = s[:, 0:width // 2] + s[:, width // 2:]
        sfin_ref[direction] = folded[:, 0:RET_DV] + folded[:, RET_DV:]

    state = load_state(0)
    for c in range(n_chunks):
        rows = pl.ds(c * RET_CHUNK, RET_CHUNK)
        qc, kc, vc = rq_ref[rows, :], rk_ref[rows, :], rv_ref[rows, :]
        q32, k32 = qc.astype(F32), kc.astype(F32)
        scores = _dot_nt(qc, _stack_heads(kc))
        v_heads = _stack_heads(vc)
        intra_ref[rows, :] = _dot((scores * db_ref[...]).astype(BF16), v_heads)
        upd_ref[c] = jnp.where(diag, _dot((k32 * kdb_ref[...]).T.astype(BF16), vc), 0.0)
        o = (_dot((scores * df_ref[...]).astype(BF16), v_heads)
             + _dot((q32 * qdf_ref[...]).astype(BF16), state.astype(BF16)))
        state = cdf_ref[...] * state + jnp.where(diag, _dot((k32 * kdf_ref[...]).T.astype(BF16), vc), 0.0)
        part_ref[rows, :] = _silu(g_ref[rows, 0:width]) * _head_norm(o, gm)
    if not has_state:
        store_state(0, state)

    state = load_state(1)
    for c in reversed(range(n_chunks)):
        rows = pl.ds(c * RET_CHUNK, RET_CHUNK)
        q32 = rq_ref[rows, :].astype(F32)
        o = intra_ref[rows, :] + _dot((q32 * qdb_ref[...]).astype(BF16), state.astype(BF16))
        state = cdb_ref[...] * state + upd_ref[c]
        o_ref[rows, :] = (part_ref[rows, :] + _silu(g_ref[rows, width:]) * _head_norm(o, gm)).astype(BF16)
    if not has_state:
        store_state(1, state)


def _retention(rq, rk, rv, gates, group_mean, tables, state, layer, batch, seq):
    n = rq.shape[0]
    has_state = state is not None
    n_chunks = seq // RET_CHUNK
    seq_blk = lambda width: pl.BlockSpec((seq, width), lambda b: (b, 0))
    whole = lambda a: pl.BlockSpec(a.shape, lambda b: (0,) * a.ndim)
    flat_tables = [t for direction in tables for t in direction]
    in_specs = [seq_blk(RET_WIDTH)] * 3 + [seq_blk(2 * RET_WIDTH), whole(group_mean)] + [whole(t) for t in flat_tables]
    args = [rq, rk, rv, gates, group_mean] + flat_tables
    out_shape = [jax.ShapeDtypeStruct((n, RET_WIDTH), BF16)]
    out_specs = [seq_blk(RET_WIDTH)]
    if has_state:
        in_specs.append(pl.BlockSpec((None, None, 2, RET_HEADS, RET_DK, RET_DV), lambda b: (b, layer, 0, 0, 0, 0)))
        args.append(state)
    else:
        out_shape.append(jax.ShapeDtypeStruct((batch, 2, RET_HEADS * RET_DK, RET_DV), F32))
        out_specs.append(pl.BlockSpec((None, 2, RET_HEADS * RET_DK, RET_DV), lambda b: (b, 0, 0, 0)))
    return pl.pallas_call(
        functools.partial(_retention_kernel, has_state=has_state, n_chunks=n_chunks),
        out_shape=out_shape,
        grid=(batch,),
        in_specs=in_specs,
        out_specs=out_specs,
        scratch_shapes=[pltpu.VMEM((seq, RET_WIDTH), F32), pltpu.VMEM((seq, RET_WIDTH), F32),
                        pltpu.VMEM((n_chunks, RET_WIDTH, RET_WIDTH), F32)],
        compiler_params=_params("parallel"),
        name="retention_latent" if has_state else "retention_context",
    )(*args)


def _fourier_kernel(f_ref, cos_ref, nsin_ref, ccos_ref, csin_ref, o_ref):
    f = f_ref[...]
    along_c = _dot(f, ccos_ref[...]).astype(BF16)
    along_s = _dot(f, csin_ref[...]).astype(BF16)
    o_ref[...] = (_dot(cos_ref[...], along_c) + _dot(nsin_ref[...], along_s)).astype(BF16)


def _fourier(fx, tables, batch, seq):
    n = fx.shape[0]
    whole = lambda a: pl.BlockSpec(a.shape, lambda b: (0,) * a.ndim)
    seq_blk = pl.BlockSpec((seq, FOURIER_WIDTH), lambda b: (b, 0))
    return pl.pallas_call(
        _fourier_kernel,
        out_shape=jax.ShapeDtypeStruct((n, FOURIER_WIDTH), BF16),
        grid=(batch,),
        in_specs=[seq_blk] + [whole(t) for t in tables],
        out_specs=seq_blk,
        compiler_params=_params("parallel"),
        name="fourier_seq%d" % seq,
    )(fx, *tables)


def _shift_rows(u, seq):
    rows = u.shape[0]
    pos = lax.broadcasted_iota(jnp.int32, u.shape, 0) % seq
    prev = jnp.where(pos == 0, 0.0, pltpu.roll(u, 1, 0))
    nxt = jnp.where(pos == seq - 1, 0.0, pltpu.roll(u, rows - 1, 0))
    return prev, nxt


def _outproj_kernel(x_ref, attn_ref, ret_ref, four_ref, mod_ref, wout_ref, o_ref):
    mixed = (_dot(attn_ref[...], wout_ref[0:ATTN_WIDTH, :])
             + _dot(ret_ref[...], wout_ref[ATTN_WIDTH:ATTN_WIDTH + RET_WIDTH, :])
             + _dot(four_ref[...], wout_ref[ATTN_WIDTH + RET_WIDTH:, :]))
    o_ref[...] = x_ref[...] + mod_ref[2:3, :] * mixed


def _out_projection(x, attn, ret, four, mod, w_out, layer, seq, per_seq_mod):
    n = x.shape[0]
    rows = OUTPROJ_ROWS
    row_blk = lambda width: pl.BlockSpec((rows, width), lambda i: (i, 0))
    mod_idx = (lambda i: (i * rows // seq, 0, 0)) if per_seq_mod else (lambda i: (0, 0, 0))
    return pl.pallas_call(
        _outproj_kernel,
        out_shape=jax.ShapeDtypeStruct((n, D_MODEL), F32),
        grid=(n // rows,),
        in_specs=[row_blk(D_MODEL), row_blk(ATTN_WIDTH), row_blk(RET_WIDTH), row_blk(FOURIER_WIDTH),
                  pl.BlockSpec((None, N_MOD, D_MODEL), mod_idx),
                  pl.BlockSpec((None, D_MODEL, D_MODEL), lambda i: (layer, 0, 0))],
        out_specs=row_blk(D_MODEL),
        compiler_params=_params("parallel"),
        name="out_projection_seq%d" % seq,
    )(x, attn, ret, four, mod, w_out)


def _ffn_kernel(*refs, seq, halo, final):
    if halo:
        x_ref, top_ref, bot_ref, mod_ref, n2_ref, wup_ref, cw_ref, cb_ref, wd_ref, fin_ref, o_ref = refs
    else:
        x_ref, mod_ref, n2_ref, wup_ref, cw_ref, cb_ref, wd_ref, fin_ref, o_ref = refs
    rows = x_ref.shape[0]
    x1 = x_ref[...]
    modulate = lambda x: _rms_rows(x, n2_ref[...]) * (1.0 + mod_ref[4:5, :]) + mod_ref[3:4, :]
    if halo:
        pad = top_ref.shape[0]
        tiles_per_seq = seq // rows
        place = pl.program_id(0) % tiles_per_seq
        h2 = modulate(jnp.concatenate([top_ref[...], x1, bot_ref[...]], axis=0))
        row = lax.broadcasted_iota(jnp.int32, h2.shape, 0)
        first_kept = jnp.where(place == 0, pad, 0)
        end_kept = jnp.where(place == tiles_per_seq - 1, pad + rows, pad + rows + pad)
        h2 = jnp.where((row >= first_kept) & (row < end_kept), h2, 0.0)
        u = _dot(h2.astype(BF16), wup_ref[...])
        prev, nxt = pltpu.roll(u, 1, 0), pltpu.roll(u, u.shape[0] - 1, 0)
    else:
        u = _dot(modulate(x1).astype(BF16), wup_ref[...])
        prev, nxt = _shift_rows(u, seq)
    u = prev * cw_ref[0:1, :] + u * cw_ref[1:2, :] + nxt * cw_ref[2:3, :] + cb_ref[...]
    if halo:
        u = u[pad:pad + rows, :]
    act = _silu(u[:, 0:D_FF]) * u[:, D_FF:]
    y = x1 + mod_ref[5:6, :] * _dot(act.astype(BF16), wd_ref[...])
    o_ref[...] = _rms_rows(y, fin_ref[...]) if final else y


def _ffn(x1, mod, norm2_w, w_up, conv_w, conv_b, w_down, final_w, layer, seq, per_seq_mod, final):
    n = x1.shape[0]
    rows, pad = FFN_ROWS, F32_SUBLANES
    halo = seq > rows
    assert (seq % rows == 0) if halo else (rows % seq == 0)
    row_blk = pl.BlockSpec((rows, D_MODEL), lambda i: (i, 0))
    whole = lambda shape: pl.BlockSpec(shape, lambda i: (0,) * len(shape))
    resident = lambda r, c: pl.BlockSpec((None, r, c), lambda i: (layer, 0, 0), pipeline_mode=pl.Buffered(1))
    mod_idx = (lambda i: (i * rows // seq, 0, 0)) if per_seq_mod else (lambda i: (0, 0, 0))
    in_specs, args = [row_blk], [x1]
    if halo:
        pads_per_tile = rows // pad
        last = n // pad - 1
        in_specs += [pl.BlockSpec((pad, D_MODEL), lambda i: (jnp.maximum(i * pads_per_tile - 1, 0), 0)),
                     pl.BlockSpec((pad, D_MODEL), lambda i: (jnp.minimum((i + 1) * pads_per_tile, last), 0))]
        args += [x1, x1]
    in_specs += [pl.BlockSpec((None, N_MOD, D_MODEL), mod_idx), whole((1, D_MODEL)),
                 resident(D_MODEL, 2 * D_FF), whole((3, 2 * D_FF)), whole((1, 2 * D_FF)),
                 resident(D_FF, D_MODEL), whole((1, D_MODEL))]
    args += [mod, norm2_w, w_up, conv_w, conv_b, w_down, final_w]
    return pl.pallas_call(
        functools.partial(_ffn_kernel, seq=seq, halo=halo, final=final),
        out_shape=jax.ShapeDtypeStruct((n, D_MODEL), F32),
        grid=(n // rows,),
        in_specs=in_specs,
        out_specs=row_blk,
        compiler_params=_params("parallel"),
        name="ffn_seq%d" % seq,
    )(*args)


def _layer(x, mod, lp, consts, layer, batch, seq, ctx, final):
    latent = ctx is not None
    outs = _in_projection(x, mod, lp["norm1_w"], consts["w_in"], lp["q_norm_w"], lp["k_norm_w"], consts["gm_attn"],
                          consts["rope"] if latent else None, layer, seq)
    q, k, v, rq, rk, rv, gates, fx = outs[:8]
    attn = _attention(q, k, v, ctx[:2] if latent else None, layer, batch, seq)
    ret_out = _retention(rq, rk, rv, gates, consts["gm_ret"], consts["ret_tables"], ctx[2] if latent else None,
                         layer, batch, seq)
    four = _fourier(fx, consts["dft"][seq], batch, seq)
    x = _out_projection(x, attn, ret_out[0], four, mod, consts["w_out"], layer, seq, latent)
    x = _ffn(x, mod, lp["norm2_w"], consts["w_up"], lp["conv_w"], lp["conv_b"], consts["w_down"], consts["final_w"],
             layer, seq, latent, final)
    if latent:
        return x, None
    return x, (outs[8], outs[9], ret_out[1])


def kernel(x_prompt, x_sample, c, cache_attn_k, cache_attn_v, state_ret, c_ctx, w_mod, b_mod, norm1_w, w_in,
           q_norm_w, k_norm_w, w_out, norm2_w, w_up, conv_w, conv_b, w_down, final_norm_w):
    batch, seq, d = x_prompt.shape
    dec_batch, dec_seq, _ = x_sample.shape
    past = cache_attn_k.shape[2]
    assert d == D_MODEL and w_in.shape == (DEPTH, D_MODEL, IN_WIDTH) and w_up.shape == (DEPTH, D_MODEL, 2 * D_FF)
    assert (batch * seq) % OUTPROJ_ROWS == 0 and OUTPROJ_ROWS % seq == 0 and dec_seq % OUTPROJ_ROWS == 0
    assert (batch * seq) % FFN_ROWS == 0 and dec_seq % FFN_ROWS == 0
    assert seq % ATTN_Q_ROWS == 0 and dec_seq % ATTN_Q_ROWS == 0
    assert INPROJ_ROWS % seq == 0 and dec_seq % INPROJ_ROWS == 0 and seq % RET_CHUNK == 0 and dec_seq % RET_CHUNK == 0
    assert dec_batch + 1 <= MOD_ROWS and dec_seq % GRID_W == 0

    consts = {
        "gm_attn": _group_mean_matrix(ATTN_WIDTH, HEAD_DIM),
        "gm_ret": _group_mean_matrix(RET_WIDTH, RET_DV),
        "rope": _rope_tables(dec_seq),
        "ret_tables": _retention_tables(),
        "dft": {s: _dft_tables(s) for s in {seq, dec_seq}},
        "final_w": final_norm_w.reshape(1, D_MODEL),
        "w_in": w_in.astype(BF16),
        "w_out": w_out.astype(BF16),
        "w_up": w_up.astype(BF16),
        "w_down": w_down.astype(BF16),
    }

    cvec = jnp.zeros((MOD_ROWS, D_MODEL), F32).at[0].set(c_ctx).at[1:1 + dec_batch].set(c)
    mod = _modulation(cvec, w_mod, b_mod).reshape(DEPTH, MOD_ROWS, N_MOD, D_MODEL)

    cache_k = cache_attn_k.reshape(dec_batch, DEPTH, past, KV_WIDTH)
    cache_v = cache_attn_v.reshape(dec_batch, DEPTH, past, KV_WIDTH)

    xp = x_prompt.reshape(batch * seq, D_MODEL)
    xs = x_sample.reshape(dec_batch * dec_seq, D_MODEL)
    new_k, new_v, new_s = [], [], []
    for layer in range(DEPTH):
        lp = {
            "norm1_w": norm1_w[layer].reshape(1, D_MODEL),
            "q_norm_w": jnp.tile(q_norm_w[layer], ATTN_HEADS).reshape(1, ATTN_WIDTH),
            "k_norm_w": jnp.tile(k_norm_w[layer], ATTN_KV_HEADS).reshape(1, KV_WIDTH),
            "norm2_w": norm2_w[layer].reshape(1, D_MODEL),
            "conv_w": conv_w[layer],
            "conv_b": conv_b[layer].reshape(1, 2 * D_FF),
        }
        final = layer == DEPTH - 1
        xp, (k_l, v_l, s_l) = _layer(xp, mod[layer, 0:1], lp, consts, layer, batch, seq, None, final)
        new_k.append(k_l.reshape(batch, seq, ATTN_KV_HEADS, HEAD_DIM))
        new_v.append(v_l.reshape(batch, seq, ATTN_KV_HEADS, HEAD_DIM))
        new_s.append(s_l.reshape(batch, 2, RET_HEADS, RET_DK, RET_DV))
        xs, _ = _layer(xs, mod[layer, 1:1 + dec_batch], lp, consts, layer, dec_batch, dec_seq,
                       (cache_k, cache_v, state_ret), final)
    return (xp.reshape(batch, seq, D_MODEL), xs.reshape(dec_batch, dec_seq, D_MODEL),
            jnp.stack(new_k, axis=1), jnp.stack(new_v, axis=1), jnp.stack(new_s, axis=1))
```

```python
import functools

import jax
import jax.numpy as jnp
import numpy as np
from jax import lax
from jax.experimental import pallas as pl
from jax.experimental.pallas import tpu as pltpu

F32 = jnp.float32
BF16 = jnp.bfloat16

D_MODEL = 1024
DEPTH = 2
GRID_W = 64
NORM_EPS = 1e-6
ATTN_HEADS = 8
ATTN_KV_HEADS = 2
HEAD_DIM = 64
ATTN_WIDTH = ATTN_HEADS * HEAD_DIM
KV_WIDTH = ATTN_KV_HEADS * HEAD_DIM
HEADS_PER_KV = ATTN_HEADS // ATTN_KV_HEADS
GROUP_WIDTH = HEADS_PER_KV * HEAD_DIM
ATTN_LOGIT_SCALE = HEAD_DIM ** -0.5 * 1.4426950408889634
ROPE_THETA = 10000.0
ROPE_AXIS_DIM = HEAD_DIM // 2
ROPE_HALF = ROPE_AXIS_DIM // 2
RET_HEADS = 4
RET_DK = 64
RET_DV = 64
RET_WIDTH = RET_HEADS * RET_DV
RET_CHUNK = 128
RET_DECAY_EXP_FWD = 5.0
RET_DECAY_EXP_BWD = 5.5
FOURIER_GROUPS = 4
FOURIER_DIM = 64
FOURIER_WIDTH = FOURIER_GROUPS * FOURIER_DIM
D_FF = 2816
N_MOD = 6

OFF_Q = 0
OFF_K = OFF_Q + ATTN_WIDTH
OFF_V = OFF_K + KV_WIDTH
OFF_RQ = OFF_V + KV_WIDTH
OFF_RK = OFF_RQ + RET_WIDTH
OFF_RV = OFF_RK + RET_WIDTH
OFF_GF = OFF_RV + RET_WIDTH
OFF_GB = OFF_GF + RET_WIDTH
OFF_FX = OFF_GB + RET_WIDTH
IN_WIDTH = OFF_FX + FOURIER_WIDTH

V7X_VMEM_BYTES = 64 * 1024 * 1024
VMEM_LIMIT = V7X_VMEM_BYTES - 12 * 1024 * 1024

MOD_ROWS = 16
MOD_COLS = 1536
INPROJ_ROWS = 1024
ATTN_Q_ROWS = 512
OUTPROJ_ROWS = 1024
FFN_ROWS = 512
F32_SUBLANES = 8


def _params(*semantics):
    return pltpu.CompilerParams(dimension_semantics=semantics, vmem_limit_bytes=VMEM_LIMIT)


def _dot(a, b):
    return jnp.dot(a, b, preferred_element_type=F32)


def _dot_nt(a, b):
    return lax.dot_general(a, b, (((1,), (1,)), ((), ())), preferred_element_type=F32)


def _sigmoid(x):
    return 1.0 / (1.0 + jnp.exp(-x))


def _silu(x):
    return x * _sigmoid(x)


def _rms_rows(x, w):
    ms = jnp.mean(x * x, axis=-1, keepdims=True)
    return x * lax.rsqrt(ms + NORM_EPS) * w


def _lane_block(shape, width):
    return lax.broadcasted_iota(jnp.int32, shape, len(shape) - 1) // width


def _group_mean_matrix(width, group):
    idx = np.arange(width) // group
    return jnp.asarray((idx[:, None] == idx[None, :]).astype(np.float32) / group, dtype=BF16)


def _rope_tables(n_tokens):
    pos = np.arange(n_tokens)
    row = (pos // GRID_W).astype(np.float64)
    col = (pos % GRID_W).astype(np.float64)
    freqs = ROPE_THETA ** (-np.arange(ROPE_HALF, dtype=np.float64) / ROPE_HALF)
    d = np.arange(HEAD_DIM)
    coord = np.where((d // ROPE_AXIS_DIM)[None, :] == 0, row[:, None], col[:, None])
    ang = coord * freqs[d % ROPE_HALF][None, :]
    sign = np.where((d & ROPE_HALF) == 0, -1.0, 1.0)[None, :]
    cos = np.tile(np.cos(ang), (1, ATTN_HEADS))
    sin = np.tile(np.sin(ang) * sign, (1, ATTN_HEADS))
    return jnp.asarray(cos, F32), jnp.asarray(sin, F32)


def _dft_tables(seq):
    n = np.arange(seq)
    ang = 2.0 * np.pi * ((n[:, None] * n[None, :]) % seq) / seq
    scale = 1.0 / np.sqrt(seq * FOURIER_DIM)
    c = np.arange(FOURIER_WIDTH)
    same = (c[:, None] // FOURIER_DIM) == (c[None, :] // FOURIER_DIM)
    angc = 2.0 * np.pi * (((c % FOURIER_DIM)[:, None] * (c % FOURIER_DIM)[None, :]) % FOURIER_DIM) / FOURIER_DIM
    as_bf16 = lambda a: jnp.asarray(a, F32).astype(BF16)
    return (as_bf16(np.cos(ang) * scale), as_bf16(-np.sin(ang) * scale),
            as_bf16(np.where(same, np.cos(angc), 0.0)), as_bf16(np.where(same, np.sin(angc), 0.0)))


def _retention_tables():
    heads = jnp.arange(RET_HEADS, dtype=F32)
    idx = jnp.arange(RET_CHUNK, dtype=F32)
    diff = idx[:, None] - idx[None, :]
    out = []
    for exp0, backward in ((RET_DECAY_EXP_FWD, False), (RET_DECAY_EXP_BWD, True)):
        lg = jnp.log1p(-jnp.exp2(-(exp0 + heads)))
        dd = -diff if backward else diff
        inner = jnp.where(dd[None] >= 0, jnp.exp(jnp.maximum(dd, 0.0)[None] * lg[:, None, None]), 0.0)
        inner = inner.transpose(1, 0, 2).reshape(RET_CHUNK, RET_HEADS * RET_CHUNK)
        q_pow = (RET_CHUNK - idx) if backward else (idx + 1.0)
        k_pow = idx if backward else (RET_CHUNK - 1.0 - idx)
        spread = lambda p: jnp.repeat(jnp.exp(p[:, None] * lg[None, :]), RET_DK, axis=1)
        q_decay, k_decay = spread(q_pow), spread(k_pow)
        chunk_decay = jnp.repeat(jnp.exp(RET_CHUNK * lg), RET_DV)[None, :]
        out.append((inner, q_decay, k_decay, chunk_decay))
    (inner_f, qd_f, kd_f, cd_f), (inner_b, qd_b, kd_b, cd_b) = out
    return (inner_f, inner_b, jnp.concatenate([qd_f, qd_b], axis=1), jnp.concatenate([kd_f, kd_b], axis=1), cd_f, cd_b)


def _mod_kernel(c_ref, w_ref, b_ref, o_ref):
    act = _silu(c_ref[...]).astype(BF16)
    o_ref[...] = _dot(act, w_ref[...].astype(BF16)) + b_ref[...]


def _modulation(cvec, w_mod, b_mod):
    n_cols = w_mod.shape[-1]
    return pl.pallas_call(
        _mod_kernel,
        out_shape=jax.ShapeDtypeStruct((DEPTH, MOD_ROWS, n_cols), F32),
        grid=(DEPTH, n_cols // MOD_COLS),
        in_specs=[
            pl.BlockSpec((MOD_ROWS, D_MODEL), lambda l, j: (0, 0)),
            pl.BlockSpec((None, D_MODEL, MOD_COLS), lambda l, j: (l, 0, j)),
            pl.BlockSpec((None, 1, MOD_COLS), lambda l, j: (l, 0, j)),
        ],
        out_specs=pl.BlockSpec((None, MOD_ROWS, MOD_COLS), lambda l, j: (l, 0, j)),
        compiler_params=_params("parallel", "parallel"),
        name="modulation",
    )(cvec, w_mod, b_mod.reshape(DEPTH, 1, n_cols))


def _swap_rotary_pairs(x):
    width = x.shape[-1]
    lane = lax.broadcasted_iota(jnp.int32, x.shape, 1)
    from_below = pltpu.roll(x, ROPE_HALF, 1)
    from_above = pltpu.roll(x, width - ROPE_HALF, 1)
    return jnp.where((lane & ROPE_HALF) != 0, from_below, from_above)


def _inproj_kernel(*refs, latent):
    x_ref, mod_ref, n1_ref, win_ref, qnw_ref, knw_ref, gm_ref = refs[:7]
    refs = refs[7:]
    if latent:
        cos_ref, sin_ref = refs[:2]
        refs = refs[2:]
        q_ref, k_ref, v_ref, rq_ref, rk_ref, rv_ref, g_ref, fx_ref = refs
    else:
        q_ref, k_ref, v_ref, rq_ref, rk_ref, rv_ref, g_ref, fx_ref, k32_ref, v32_ref = refs

    shift, scale = mod_ref[0:1, :], mod_ref[1:2, :]
    h = _rms_rows(x_ref[...], n1_ref[...]) * (1.0 + scale) + shift
    all_proj = _dot(h.astype(BF16), win_ref[...])
    proj = lambda off, width: all_proj[:, off:off + width]

    q = proj(OFF_Q, ATTN_WIDTH)
    q = q * lax.rsqrt(_dot((q * q).astype(BF16), gm_ref[...]) + NORM_EPS) * qnw_ref[...]
    k = proj(OFF_K, KV_WIDTH)
    k = k * lax.rsqrt(_dot((k * k).astype(BF16), gm_ref[0:KV_WIDTH, 0:KV_WIDTH]) + NORM_EPS) * knw_ref[...]
    v = proj(OFF_V, KV_WIDTH)
    if latent:
        cos, sin = cos_ref[...], sin_ref[...]
        q = q * cos + _swap_rotary_pairs(q) * sin
        k = k * cos[:, 0:KV_WIDTH] + _swap_rotary_pairs(k) * sin[:, 0:KV_WIDTH]
    else:
        k32_ref[...] = k
        v32_ref[...] = v
    q_ref[...] = (q * ATTN_LOGIT_SCALE).astype(BF16)
    k_ref[...] = k.astype(BF16)
    v_ref[...] = v.astype(BF16)
    rq_ref[...] = proj(OFF_RQ, RET_WIDTH).astype(BF16)
    rk_ref[...] = (proj(OFF_RK, RET_WIDTH) * RET_DK ** -0.5).astype(BF16)
    rv_ref[...] = proj(OFF_RV, RET_WIDTH).astype(BF16)
    g_ref[...] = proj(OFF_GF, 2 * RET_WIDTH)
    fx_ref[...] = proj(OFF_FX, FOURIER_WIDTH).astype(BF16)


def _in_projection(x, mod, norm1_w, w_in, q_norm_w, k_norm_w, group_mean, rope, layer, seq):
    n = x.shape[0]
    rows = INPROJ_ROWS
    latent = rope is not None
    row_blk = lambda width: pl.BlockSpec((rows, width), lambda i: (i, 0))
    whole = lambda shape: pl.BlockSpec(shape, lambda i: (0,) * len(shape))
    mod_idx = (lambda i: (i * rows // seq, 0, 0)) if latent else (lambda i: (0, 0, 0))
    in_specs = [
        row_blk(D_MODEL),
        pl.BlockSpec((None, N_MOD, D_MODEL), mod_idx),
        whole((1, D_MODEL)),
        pl.BlockSpec((None, D_MODEL, IN_WIDTH), lambda i: (layer, 0, 0)),
        whole((1, ATTN_WIDTH)),
        whole((1, KV_WIDTH)),
        whole((ATTN_WIDTH, ATTN_WIDTH)),
    ]
    args = [x, mod, norm1_w, w_in, q_norm_w, k_norm_w, group_mean]
    out_widths = [(ATTN_WIDTH, BF16), (KV_WIDTH, BF16), (KV_WIDTH, BF16), (RET_WIDTH, BF16), (RET_WIDTH, BF16),
                  (RET_WIDTH, BF16), (2 * RET_WIDTH, F32), (FOURIER_WIDTH, BF16)]
    if latent:
        pos_blk = pl.BlockSpec((rows, ATTN_WIDTH), lambda i: (i % (seq // rows), 0))
        in_specs += [pos_blk, pos_blk]
        args += list(rope)
    else:
        out_widths += [(KV_WIDTH, F32), (KV_WIDTH, F32)]
    return pl.pallas_call(
        functools.partial(_inproj_kernel, latent=latent),
        out_shape=[jax.ShapeDtypeStruct((n, w), dt) for w, dt in out_widths],
        grid=(n // rows,),
        in_specs=in_specs,
        out_specs=[row_blk(w) for w, _ in out_widths],
        compiler_params=_params("parallel"),
        name="in_projection_latent" if latent else "in_projection_context",
    )(*args)


def _spread_kv(x, group):
    lane = lax.broadcasted_iota(jnp.int32, x.shape, 1)
    other = pltpu.roll(x, HEAD_DIM, 1)
    own = (lane // HEAD_DIM) == group
    pair = jnp.where(own, x, other).astype(BF16)
    return jnp.concatenate([pair, pair], axis=1)


def _attention_kernel(*refs, past):
    if past:
        q_ref, k_ref, v_ref, ck_ref, cv_ref, o_ref, kt_ref, vt_ref = refs
    else:
        q_ref, k_ref, v_ref, o_ref, kt_ref, vt_ref = refs

    @pl.when(pl.program_id(1) == 0)
    def _():
        for g in range(ATTN_KV_HEADS):
            if past:
                kt_ref[g, 0:past, :] = _spread_kv(ck_ref[...], g)
                vt_ref[g, 0:past, :] = _spread_kv(cv_ref[...], g)
            kt_ref[g, past:, :] = _spread_kv(k_ref[...].astype(F32), g)
            vt_ref[g, past:, :] = _spread_kv(v_ref[...].astype(F32), g)

    rows = q_ref.shape[0]
    block = _lane_block((rows, GROUP_WIDTH), HEAD_DIM)
    for g in range(ATTN_KV_HEADS):
        qg = q_ref[:, g * GROUP_WIDTH:(g + 1) * GROUP_WIDTH]
        out = jnp.zeros((rows, GROUP_WIDTH), F32)
        for h in range(HEADS_PER_KV):
            qh = jnp.where(block == h, qg, jnp.zeros_like(qg))
            s = _dot_nt(qh, kt_ref[g])
            p = jnp.exp2(s - jnp.max(s, axis=-1, keepdims=True))
            denom = jnp.sum(p, axis=-1, keepdims=True)
            o = _dot(p.astype(BF16), vt_ref[g])
            out = jnp.where(block == h, o / denom, out)
        o_ref[:, g * GROUP_WIDTH:(g + 1) * GROUP_WIDTH] = out.astype(BF16)


def _attention(q, k, v, cache, layer, batch, seq):
    n = q.shape[0]
    tq = min(ATTN_Q_ROWS, seq)
    nq = seq // tq
    past = 0 if cache is None else cache[0].shape[2]
    own_kv = pl.BlockSpec((seq, KV_WIDTH), lambda b, i: (b, 0))
    in_specs = [pl.BlockSpec((tq, ATTN_WIDTH), lambda b, i: (b * nq + i, 0)), own_kv, own_kv]
    args = [q, k, v]
    if past:
        cached_kv = pl.BlockSpec((None, None, past, KV_WIDTH), lambda b, i: (b, layer, 0, 0))
        in_specs += [cached_kv, cached_kv]
        args += list(cache)
    return pl.pallas_call(
        functools.partial(_attention_kernel, past=past),
        out_shape=jax.ShapeDtypeStruct((n, ATTN_WIDTH), BF16),
        grid=(batch, nq),
        in_specs=in_specs,
        out_specs=pl.BlockSpec((tq, ATTN_WIDTH), lambda b, i: (b * nq + i, 0)),
        scratch_shapes=[pltpu.VMEM((ATTN_KV_HEADS, past + seq, GROUP_WIDTH), BF16)] * 2,
        compiler_params=_params("parallel", "arbitrary"),
        name="attention_latent" if past else "attention_context",
    )(*args)


def _stack_heads(x):
    block = _lane_block(x.shape, RET_DK)
    zero = jnp.zeros_like(x)
    return jnp.concatenate([jnp.where(block == h, x, zero) for h in range(RET_HEADS)], axis=0)


def _head_norm(o, gm):
    mu = _dot(o.astype(BF16), gm)
    d = o - mu
    var = _dot((d * d).astype(BF16), gm)
    return d * lax.rsqrt(var + NORM_EPS)


def _retention_kernel(*refs, has_state, n_chunks):
    rq_ref, rk_ref, rv_ref, g_ref, gm_ref, df_ref, db_ref, qd_ref, kd_ref, cdf_ref, cdb_ref = refs[:11]
    refs = refs[11:]
    if has_state:
        s0_ref, o_ref, both_ref, upd_ref, st_ref = refs
    else:
        o_ref, sfin_ref, both_ref, upd_ref, st_ref = refs

    width = RET_WIDTH
    diag = (lax.broadcasted_iota(jnp.int32, (width, width), 0) // RET_DK
            == lax.broadcasted_iota(jnp.int32, (width, width), 1) // RET_DV)
    chunk_rows = lambda c: pl.ds(c * RET_CHUNK, RET_CHUNK)

    for c in range(n_chunks):
        rows = chunk_rows(c)
        qc, kc, vc = rq_ref[rows, :], rk_ref[rows, :], rv_ref[rows, :]
        scores = _dot_nt(qc, _stack_heads(kc))
        v_heads = _stack_heads(vc)
        both_ref[rows, 0:width] = _dot((scores * df_ref[...]).astype(BF16), v_heads)
        both_ref[rows, width:] = _dot((scores * db_ref[...]).astype(BF16), v_heads)
        k32 = kc.astype(F32)
        k_decayed = jnp.concatenate([k32, k32], axis=1) * kd_ref[...]
        upd_ref[c] = _dot(k_decayed.T.astype(BF16), vc)

    for direction, order, cd_ref in ((0, range(n_chunks), cdf_ref), (1, reversed(range(n_chunks)), cdb_ref)):
        if has_state:
            s = s0_ref[direction].reshape(width, RET_DV)
            state = jnp.where(diag, jnp.concatenate([s] * RET_HEADS, axis=1), 0.0)
        else:
            state = jnp.zeros((width, width), F32)
        for c in order:
            st_ref[direction, c] = state.astype(BF16)
            update = upd_ref[c, direction * width:(direction + 1) * width, :]
            state = cd_ref[...] * state + jnp.where(diag, update, 0.0)
        if not has_state:
            folded = state[:, 0:width // 2] + state[:, width // 2:]
            sfin_ref[direction] = folded[:, 0:RET_DV] + folded[:, RET_DV:]

    for c in range(n_chunks):
        rows = chunk_rows(c)
        q32 = rq_ref[rows, :].astype(F32)
        q_decayed = (jnp.concatenate([q32, q32], axis=1) * qd_ref[...]).astype(BF16)
        both_ref[rows, 0:width] += _dot(q_decayed[:, 0:width], st_ref[0, c])
        both_ref[rows, width:] += _dot(q_decayed[:, width:], st_ref[1, c])

    gated = _silu(g_ref[...]) * _head_norm(both_ref[...], gm_ref[...])
    o_ref[...] = (gated[:, 0:width] + gated[:, width:]).astype(BF16)


def _retention(rq, rk, rv, gates, group_mean, tables, state, layer, batch, seq):
    n = rq.shape[0]
    has_state = state is not None
    n_chunks = seq // RET_CHUNK
    seq_blk = lambda width: pl.BlockSpec((seq, width), lambda b: (b, 0))
    whole = lambda a: pl.BlockSpec(a.shape, lambda b: (0,) * a.ndim)
    in_specs = [seq_blk(RET_WIDTH)] * 3 + [seq_blk(2 * RET_WIDTH), whole(group_mean)] + [whole(t) for t in tables]
    args = [rq, rk, rv, gates, group_mean] + list(tables)
    out_shape = [jax.ShapeDtypeStruct((n, RET_WIDTH), BF16)]
    out_specs = [seq_blk(RET_WIDTH)]
    if has_state:
        in_specs.append(pl.BlockSpec((None, None, 2, RET_HEADS, RET_DK, RET_DV), lambda b: (b, layer, 0, 0, 0, 0)))
        args.append(state)
    else:
        out_shape.append(jax.ShapeDtypeStruct((batch, 2, RET_HEADS * RET_DK, RET_DV), F32))
        out_specs.append(pl.BlockSpec((None, 2, RET_HEADS * RET_DK, RET_DV), lambda b: (b, 0, 0, 0)))
    return pl.pallas_call(
        functools.partial(_retention_kernel, has_state=has_state, n_chunks=n_chunks),
        out_shape=out_shape,
        grid=(batch,),
        in_specs=in_specs,
        out_specs=out_specs,
        scratch_shapes=[pltpu.VMEM((seq, 2 * RET_WIDTH), F32),
                        pltpu.VMEM((n_chunks, 2 * RET_WIDTH, RET_WIDTH), F32),
                        pltpu.VMEM((2, n_chunks, RET_WIDTH, RET_WIDTH), BF16)],
        compiler_params=_params("parallel"),
        name="retention_latent" if has_state else "retention_context",
    )(*args)


def _fourier_kernel(f_ref, cos_ref, nsin_ref, ccos_ref, csin_ref, o_ref):
    f = f_ref[...]
    along_c = _dot(f, ccos_ref[...]).astype(BF16)
    along_s = _dot(f, csin_ref[...]).astype(BF16)
    o_ref[...] = (_dot(cos_ref[...], along_c) + _dot(nsin_ref[...], along_s)).astype(BF16)


def _fourier(fx, tables, batch, seq):
    n = fx.shape[0]
    whole = lambda a: pl.BlockSpec(a.shape, lambda b: (0,) * a.ndim)
    seq_blk = pl.BlockSpec((seq, FOURIER_WIDTH), lambda b: (b, 0))
    return pl.pallas_call(
        _fourier_kernel,
        out_shape=jax.ShapeDtypeStruct((n, FOURIER_WIDTH), BF16),
        grid=(batch,),
        in_specs=[seq_blk] + [whole(t) for t in tables],
        out_specs=seq_blk,
        compiler_params=_params("parallel"),
        name="fourier_seq%d" % seq,
    )(fx, *tables)


def _shift_rows(u, seq):
    rows = u.shape[0]
    pos = lax.broadcasted_iota(jnp.int32, u.shape, 0) % seq
    prev = jnp.where(pos == 0, 0.0, pltpu.roll(u, 1, 0))
    nxt = jnp.where(pos == seq - 1, 0.0, pltpu.roll(u, rows - 1, 0))
    return prev, nxt


def _outproj_kernel(x_ref, attn_ref, ret_ref, four_ref, mod_ref, wout_ref, o_ref):
    mixed = (_dot(attn_ref[...], wout_ref[0:ATTN_WIDTH, :])
             + _dot(ret_ref[...], wout_ref[ATTN_WIDTH:ATTN_WIDTH + RET_WIDTH, :])
             + _dot(four_ref[...], wout_ref[ATTN_WIDTH + RET_WIDTH:, :]))
    o_ref[...] = x_ref[...] + mod_ref[2:3, :] * mixed


def _out_projection(x, attn, ret, four, mod, w_out, layer, seq, per_seq_mod):
    n = x.shape[0]
    rows = OUTPROJ_ROWS
    row_blk = lambda width: pl.BlockSpec((rows, width), lambda i: (i, 0))
    mod_idx = (lambda i: (i * rows // seq, 0, 0)) if per_seq_mod else (lambda i: (0, 0, 0))
    return pl.pallas_call(
        _outproj_kernel,
        out_shape=jax.ShapeDtypeStruct((n, D_MODEL), F32),
        grid=(n // rows,),
        in_specs=[row_blk(D_MODEL), row_blk(ATTN_WIDTH), row_blk(RET_WIDTH), row_blk(FOURIER_WIDTH),
                  pl.BlockSpec((None, N_MOD, D_MODEL), mod_idx),
                  pl.BlockSpec((None, D_MODEL, D_MODEL), lambda i: (layer, 0, 0))],
        out_specs=row_blk(D_MODEL),
        compiler_params=_params("parallel"),
        name="out_projection_seq%d" % seq,
    )(x, attn, ret, four, mod, w_out)


def _ffn_kernel(*refs, seq, halo, final):
    if halo:
        x_ref, top_ref, bot_ref, mod_ref, n2_ref, wup_ref, cw_ref, cb_ref, wd_ref, fin_ref, o_ref = refs
    else:
        x_ref, mod_ref, n2_ref, wup_ref, cw_ref, cb_ref, wd_ref, fin_ref, o_ref = refs
    rows = x_ref.shape[0]
    x1 = x_ref[...]
    modulate = lambda x: _rms_rows(x, n2_ref[...]) * (1.0 + mod_ref[4:5, :]) + mod_ref[3:4, :]
    if halo:
        pad = top_ref.shape[0]
        tiles_per_seq = seq // rows
        place = pl.program_id(0) % tiles_per_seq
        h2 = modulate(jnp.concatenate([top_ref[...], x1, bot_ref[...]], axis=0))
        row = lax.broadcasted_iota(jnp.int32, h2.shape, 0)
        first_kept = jnp.where(place == 0, pad, 0)
        end_kept = jnp.where(place == tiles_per_seq - 1, pad + rows, pad + rows + pad)
        h2 = jnp.where((row >= first_kept) & (row < end_kept), h2, 0.0)
        u = _dot(h2.astype(BF16), wup_ref[...])
        prev, nxt = pltpu.roll(u, 1, 0), pltpu.roll(u, u.shape[0] - 1, 0)
    else:
        u = _dot(modulate(x1).astype(BF16), wup_ref[...])
        prev, nxt = _shift_rows(u, seq)
    u = prev * cw_ref[0:1, :] + u * cw_ref[1:2, :] + nxt * cw_ref[2:3, :] + cb_ref[...]
    if halo:
        u = u[pad:pad + rows, :]
    act = _silu(u[:, 0:D_FF]) * u[:, D_FF:]
    y = x1 + mod_ref[5:6, :] * _dot(act.astype(BF16), wd_ref[...])
    o_ref[...] = _rms_rows(y, fin_ref[...]) if final else y


def _ffn(x1, mod, norm2_w, w_up, conv_w, conv_b, w_down, final_w, layer, seq, per_seq_mod, final):
    n = x1.shape[0]
    rows, pad = FFN_ROWS, F32_SUBLANES
    halo = seq > rows
    assert (seq % rows == 0) if halo else (rows % seq == 0)
    row_blk = pl.BlockSpec((rows, D_MODEL), lambda i: (i, 0))
    whole = lambda shape: pl.BlockSpec(shape, lambda i: (0,) * len(shape))
    resident = lambda r, c: pl.BlockSpec((None, r, c), lambda i: (layer, 0, 0), pipeline_mode=pl.Buffered(1))
    mod_idx = (lambda i: (i * rows // seq, 0, 0)) if per_seq_mod else (lambda i: (0, 0, 0))
    in_specs, args = [row_blk], [x1]
    if halo:
        pads_per_tile = rows // pad
        last = n // pad - 1
        in_specs += [pl.BlockSpec((pad, D_MODEL), lambda i: (jnp.maximum(i * pads_per_tile - 1, 0), 0)),
                     pl.BlockSpec((pad, D_MODEL), lambda i: (jnp.minimum((i + 1) * pads_per_tile, last), 0))]
        args += [x1, x1]
    in_specs += [pl.BlockSpec((None, N_MOD, D_MODEL), mod_idx), whole((1, D_MODEL)),
                 resident(D_MODEL, 2 * D_FF), whole((3, 2 * D_FF)), whole((1, 2 * D_FF)),
                 resident(D_FF, D_MODEL), whole((1, D_MODEL))]
    args += [mod, norm2_w, w_up, conv_w, conv_b, w_down, final_w]
    return pl.pallas_call(
        functools.partial(_ffn_kernel, seq=seq, halo=halo, final=final),
        out_shape=jax.ShapeDtypeStruct((n, D_MODEL), F32),
        grid=(n // rows,),
        in_specs=in_specs,
        out_specs=row_blk,
        compiler_params=_params("parallel"),
        name="ffn_seq%d" % seq,
    )(*args)


def _layer(x, mod, lp, consts, layer, batch, seq, ctx, final):
    latent = ctx is not None
    outs = _in_projection(x, mod, lp["norm1_w"], consts["w_in"], lp["q_norm_w"], lp["k_norm_w"], consts["gm_attn"],
                          consts["rope"] if latent else None, layer, seq)
    q, k, v, rq, rk, rv, gates, fx = outs[:8]
    attn = _attention(q, k, v, ctx[:2] if latent else None, layer, batch, seq)
    ret_out = _retention(rq, rk, rv, gates, consts["gm_ret"], consts["ret_tables"], ctx[2] if latent else None,
                         layer, batch, seq)
    four = _fourier(fx, consts["dft"][seq], batch, seq)
    x = _out_projection(x, attn, ret_out[0], four, mod, consts["w_out"], layer, seq, latent)
    x = _ffn(x, mod, lp["norm2_w"], consts["w_up"], lp["conv_w"], lp["conv_b"], consts["w_down"], consts["final_w"],
             layer, seq, latent, final)
    if latent:
        return x, None
    return x, (outs[8], outs[9], ret_out[1])


def kernel(x_prompt, x_sample, c, cache_attn_k, cache_attn_v, state_ret, c_ctx, w_mod, b_mod, norm1_w, w_in,
           q_norm_w, k_norm_w, w_out, norm2_w, w_up, conv_w, conv_b, w_down, final_norm_w):
    batch, seq, d = x_prompt.shape
    dec_batch, dec_seq, _ = x_sample.shape
    past = cache_attn_k.shape[2]
    assert d == D_MODEL and w_in.shape == (DEPTH, D_MODEL, IN_WIDTH) and w_up.shape == (DEPTH, D_MODEL, 2 * D_FF)
    assert (batch * seq) % OUTPROJ_ROWS == 0 and OUTPROJ_ROWS % seq == 0 and dec_seq % OUTPROJ_ROWS == 0
    assert (batch * seq) % FFN_ROWS == 0 and dec_seq % FFN_ROWS == 0
    assert seq % min(ATTN_Q_ROWS, seq) == 0 and dec_seq % min(ATTN_Q_ROWS, dec_seq) == 0
    assert INPROJ_ROWS % seq == 0 and dec_seq % INPROJ_ROWS == 0 and seq % RET_CHUNK == 0 and dec_seq % RET_CHUNK == 0
    assert dec_batch + 1 <= MOD_ROWS and dec_seq % GRID_W == 0

    consts = {
        "gm_attn": _group_mean_matrix(ATTN_WIDTH, HEAD_DIM),
        "gm_ret": _group_mean_matrix(2 * RET_WIDTH, RET_DV),
        "rope": _rope_tables(dec_seq),
        "ret_tables": _retention_tables(),
        "dft": {s: _dft_tables(s) for s in {seq, dec_seq}},
        "final_w": final_norm_w.reshape(1, D_MODEL),
        "w_in": w_in.astype(BF16),
        "w_out": w_out.astype(BF16),
        "w_up": w_up.astype(BF16),
        "w_down": w_down.astype(BF16),
    }

    cvec = jnp.zeros((MOD_ROWS, D_MODEL), F32).at[0].set(c_ctx).at[1:1 + dec_batch].set(c)
    mod = _modulation(cvec, w_mod, b_mod).reshape(DEPTH, MOD_ROWS, N_MOD, D_MODEL)

    cache_k = cache_attn_k.reshape(dec_batch, DEPTH, past, KV_WIDTH)
    cache_v = cache_attn_v.reshape(dec_batch, DEPTH, past, KV_WIDTH)

    xp = x_prompt.reshape(batch * seq, D_MODEL)
    xs = x_sample.reshape(dec_batch * dec_seq, D_MODEL)
    new_k, new_v, new_s = [], [], []
    for layer in range(DEPTH):
        lp = {
            "norm1_w": norm1_w[layer].reshape(1, D_MODEL),
            "q_norm_w": jnp.tile(q_norm_w[layer], ATTN_HEADS).reshape(1, ATTN_WIDTH),
            "k_norm_w": jnp.tile(k_norm_w[layer], ATTN_KV_HEADS).reshape(1, KV_WIDTH),
            "norm2_w": norm2_w[layer].reshape(1, D_MODEL),
            "conv_w": conv_w[layer],
            "conv_b": conv_b[layer].reshape(1, 2 * D_FF),
        }
        final = layer == DEPTH - 1
        xp, (k_l, v_l, s_l) = _layer(xp, mod[layer, 0:1], lp, consts, layer, batch, seq, None, final)
        new_k.append(k_l.reshape(batch, seq, ATTN_KV_HEADS, HEAD_DIM))
        new_v.append(v_l.reshape(batch, seq, ATTN_KV_HEADS, HEAD_DIM))
        new_s.append(s_l.reshape(batch, 2, RET_HEADS, RET_DK, RET_DV))
        xs, _ = _layer(xs, mod[layer, 1:1 + dec_batch], lp, consts, layer, dec_batch, dec_seq,
                       (cache_k, cache_v, state_ret), final)
    return (xp.reshape(batch, seq, D_MODEL), xs.reshape(dec_batch, dec_seq, D_MODEL),
            jnp.stack(new_k, axis=1), jnp.stack(new_v, axis=1), jnp.stack(new_s, axis=1))
```

```python
import functools

import jax
import jax.numpy as jnp
import numpy as np
from jax import lax
from jax.experimental import pallas as pl
from jax.experimental.pallas import tpu as pltpu

F32 = jnp.float32
BF16 = jnp.bfloat16

D_MODEL = 1024
DEPTH = 2
GRID_W = 64
NORM_EPS = 1e-6
ATTN_HEADS = 8
ATTN_KV_HEADS = 2
HEAD_DIM = 64
ATTN_WIDTH = ATTN_HEADS * HEAD_DIM
KV_WIDTH = ATTN_KV_HEADS * HEAD_DIM
HEADS_PER_KV = ATTN_HEADS // ATTN_KV_HEADS
GROUP_WIDTH = HEADS_PER_KV * HEAD_DIM
ATTN_LOGIT_SCALE = HEAD_DIM ** -0.5 * 1.4426950408889634
ROPE_THETA = 10000.0
ROPE_AXIS_DIM = HEAD_DIM // 2
ROPE_HALF = ROPE_AXIS_DIM // 2
RET_HEADS = 4
RET_DK = 64
RET_DV = 64
RET_WIDTH = RET_HEADS * RET_DV
RET_CHUNK = 128
RET_DECAY_EXP_FWD = 5.0
RET_DECAY_EXP_BWD = 5.5
FOURIER_GROUPS = 4
FOURIER_DIM = 64
FOURIER_WIDTH = FOURIER_GROUPS * FOURIER_DIM
D_FF = 2816
N_MOD = 6

OFF_Q = 0
OFF_K = OFF_Q + ATTN_WIDTH
OFF_V = OFF_K + KV_WIDTH
OFF_RQ = OFF_V + KV_WIDTH
OFF_RK = OFF_RQ + RET_WIDTH
OFF_RV = OFF_RK + RET_WIDTH
OFF_GF = OFF_RV + RET_WIDTH
OFF_GB = OFF_GF + RET_WIDTH
OFF_FX = OFF_GB + RET_WIDTH
IN_WIDTH = OFF_FX + FOURIER_WIDTH
MIX_OFF_ATTN = 0
MIX_OFF_RET = MIX_OFF_ATTN + ATTN_WIDTH
MIX_OFF_FOURIER = MIX_OFF_RET + RET_WIDTH
MIX_WIDTH = MIX_OFF_FOURIER + FOURIER_WIDTH

V7X_VMEM_BYTES = 64 * 1024 * 1024
VMEM_LIMIT = V7X_VMEM_BYTES - 12 * 1024 * 1024

MOD_ROWS = 16
MOD_COLS = 1536
INPROJ_ROWS = 1024
ATTN_Q_ROWS = 512
FFN_ROWS = 512
F32_SUBLANES = 8
BF16_SUBLANES = 16


def _params(*semantics):
    return pltpu.CompilerParams(dimension_semantics=semantics, vmem_limit_bytes=VMEM_LIMIT)


def _dot(a, b):
    return jnp.dot(a, b, preferred_element_type=F32)


def _dot_nt(a, b):
    return lax.dot_general(a, b, (((1,), (1,)), ((), ())), preferred_element_type=F32)


def _sigmoid(x):
    return 1.0 / (1.0 + jnp.exp(-x))


def _silu(x):
    return x * _sigmoid(x)


def _rms_rows(x, w):
    ms = jnp.mean(x * x, axis=-1, keepdims=True)
    return x * lax.rsqrt(ms + NORM_EPS) * w


def _lane_block(shape, width):
    return lax.broadcasted_iota(jnp.int32, shape, len(shape) - 1) // width


def _group_mean_matrix(width, group):
    idx = np.arange(width) // group
    return jnp.asarray((idx[:, None] == idx[None, :]).astype(np.float32) / group, dtype=BF16)


def _rope_tables(n_tokens):
    pos = np.arange(n_tokens)
    row = (pos // GRID_W).astype(np.float64)
    col = (pos % GRID_W).astype(np.float64)
    freqs = ROPE_THETA ** (-np.arange(ROPE_HALF, dtype=np.float64) / ROPE_HALF)
    d = np.arange(HEAD_DIM)
    coord = np.where((d // ROPE_AXIS_DIM)[None, :] == 0, row[:, None], col[:, None])
    ang = coord * freqs[d % ROPE_HALF][None, :]
    sign = np.where((d & ROPE_HALF) == 0, -1.0, 1.0)[None, :]
    cos = np.tile(np.cos(ang), (1, ATTN_HEADS))
    sin = np.tile(np.sin(ang) * sign, (1, ATTN_HEADS))
    return jnp.asarray(cos, F32), jnp.asarray(sin, F32)


def _dft_tables(seq):
    n = np.arange(seq)
    ang = 2.0 * np.pi * ((n[:, None] * n[None, :]) % seq) / seq
    scale = 1.0 / np.sqrt(seq * FOURIER_DIM)
    c = np.arange(FOURIER_WIDTH)
    same = (c[:, None] // FOURIER_DIM) == (c[None, :] // FOURIER_DIM)
    angc = 2.0 * np.pi * (((c % FOURIER_DIM)[:, None] * (c % FOURIER_DIM)[None, :]) % FOURIER_DIM) / FOURIER_DIM
    as_bf16 = lambda a: jnp.asarray(a, F32).astype(BF16)
    return (as_bf16(np.cos(ang) * scale), as_bf16(-np.sin(ang) * scale),
            as_bf16(np.where(same, np.cos(angc), 0.0)), as_bf16(np.where(same, np.sin(angc), 0.0)))


def _retention_tables():
    heads = jnp.arange(RET_HEADS, dtype=F32)
    idx = jnp.arange(RET_CHUNK, dtype=F32)
    diff = idx[:, None] - idx[None, :]
    out = []
    for exp0, backward in ((RET_DECAY_EXP_FWD, False), (RET_DECAY_EXP_BWD, True)):
        lg = jnp.log1p(-jnp.exp2(-(exp0 + heads)))
        dd = -diff if backward else diff
        inner = jnp.where(dd[None] >= 0, jnp.exp(jnp.maximum(dd, 0.0)[None] * lg[:, None, None]), 0.0)
        inner = inner.transpose(1, 0, 2).reshape(RET_CHUNK, RET_HEADS * RET_CHUNK)
        q_pow = (RET_CHUNK - idx) if backward else (idx + 1.0)
        k_pow = idx if backward else (RET_CHUNK - 1.0 - idx)
        spread = lambda p: jnp.repeat(jnp.exp(p[:, None] * lg[None, :]), RET_DK, axis=1)
        q_decay, k_decay = spread(q_pow), spread(k_pow)
        chunk_decay = jnp.repeat(jnp.exp(RET_CHUNK * lg), RET_DV)[None, :]
        out.append((inner, q_decay, k_decay, chunk_decay))
    (inner_f, qd_f, kd_f, cd_f), (inner_b, qd_b, kd_b, cd_b) = out
    return (inner_f, inner_b, jnp.concatenate([qd_f, qd_b], axis=1), jnp.concatenate([kd_f, kd_b], axis=1), cd_f, cd_b)


def _mod_kernel(c_ref, w_ref, b_ref, o_ref):
    act = _silu(c_ref[...]).astype(BF16)
    o_ref[...] = _dot(act, w_ref[...].astype(BF16)) + b_ref[...]


def _modulation(cvec, w_mod, b_mod):
    n_cols = w_mod.shape[-1]
    return pl.pallas_call(
        _mod_kernel,
        out_shape=jax.ShapeDtypeStruct((DEPTH, MOD_ROWS, n_cols), F32),
        grid=(DEPTH, n_cols // MOD_COLS),
        in_specs=[
            pl.BlockSpec((MOD_ROWS, D_MODEL), lambda l, j: (0, 0)),
            pl.BlockSpec((None, D_MODEL, MOD_COLS), lambda l, j: (l, 0, j)),
            pl.BlockSpec((None, 1, MOD_COLS), lambda l, j: (l, 0, j)),
        ],
        out_specs=pl.BlockSpec((None, MOD_ROWS, MOD_COLS), lambda l, j: (l, 0, j)),
        compiler_params=_params("parallel", "parallel"),
        name="modulation",
    )(cvec, w_mod, b_mod.reshape(DEPTH, 1, n_cols))


def _swap_rotary_pairs(x):
    width = x.shape[-1]
    lane = lax.broadcasted_iota(jnp.int32, x.shape, 1)
    from_below = pltpu.roll(x, ROPE_HALF, 1)
    from_above = pltpu.roll(x, width - ROPE_HALF, 1)
    return jnp.where((lane & ROPE_HALF) != 0, from_below, from_above)


def _inproj_kernel(*refs, latent):
    x_ref, mod_ref, n1_ref, win_ref, qnw_ref, knw_ref, gm_ref = refs[:7]
    refs = refs[7:]
    if latent:
        cos_ref, sin_ref = refs[:2]
        refs = refs[2:]
        q_ref, k_ref, v_ref, rq_ref, rk_ref, rv_ref, g_ref, fx_ref = refs
    else:
        q_ref, k_ref, v_ref, rq_ref, rk_ref, rv_ref, g_ref, fx_ref, k32_ref, v32_ref = refs

    shift, scale = mod_ref[0:1, :], mod_ref[1:2, :]
    h = _rms_rows(x_ref[...], n1_ref[...]) * (1.0 + scale) + shift
    all_proj = _dot(h.astype(BF16), win_ref[...])
    proj = lambda off, width: all_proj[:, off:off + width]

    q = proj(OFF_Q, ATTN_WIDTH)
    q = q * lax.rsqrt(_dot((q * q).astype(BF16), gm_ref[...]) + NORM_EPS) * qnw_ref[...]
    k = proj(OFF_K, KV_WIDTH)
    k = k * lax.rsqrt(_dot((k * k).astype(BF16), gm_ref[0:KV_WIDTH, 0:KV_WIDTH]) + NORM_EPS) * knw_ref[...]
    v = proj(OFF_V, KV_WIDTH)
    if latent:
        cos, sin = cos_ref[...], sin_ref[...]
        q = q * cos + _swap_rotary_pairs(q) * sin
        k = k * cos[:, 0:KV_WIDTH] + _swap_rotary_pairs(k) * sin[:, 0:KV_WIDTH]
    else:
        k32_ref[...] = k
        v32_ref[...] = v
    q_ref[...] = (q * ATTN_LOGIT_SCALE).astype(BF16)
    k_ref[...] = k.astype(BF16)
    v_ref[...] = v.astype(BF16)
    rq_ref[...] = proj(OFF_RQ, RET_WIDTH).astype(BF16)
    rk_ref[...] = (proj(OFF_RK, RET_WIDTH) * RET_DK ** -0.5).astype(BF16)
    rv_ref[...] = proj(OFF_RV, RET_WIDTH).astype(BF16)
    g_ref[...] = proj(OFF_GF, 2 * RET_WIDTH)
    fx_ref[...] = proj(OFF_FX, FOURIER_WIDTH).astype(BF16)


def _in_projection(x, mod, norm1_w, w_in, q_norm_w, k_norm_w, group_mean, rope, layer, seq):
    n = x.shape[0]
    rows = INPROJ_ROWS
    latent = rope is not None
    row_blk = lambda width: pl.BlockSpec((rows, width), lambda i: (i, 0))
    whole = lambda shape: pl.BlockSpec(shape, lambda i: (0,) * len(shape))
    mod_idx = (lambda i: (i * rows // seq, 0, 0)) if latent else (lambda i: (0, 0, 0))
    in_specs = [
        row_blk(D_MODEL),
        pl.BlockSpec((None, N_MOD, D_MODEL), mod_idx),
        whole((1, D_MODEL)),
        pl.BlockSpec((None, D_MODEL, IN_WIDTH), lambda i: (layer, 0, 0)),
        whole((1, ATTN_WIDTH)),
        whole((1, KV_WIDTH)),
        whole((ATTN_WIDTH, ATTN_WIDTH)),
    ]
    args = [x, mod, norm1_w, w_in, q_norm_w, k_norm_w, group_mean]
    out_widths = [(ATTN_WIDTH, BF16), (KV_WIDTH, BF16), (KV_WIDTH, BF16), (RET_WIDTH, BF16), (RET_WIDTH, BF16),
                  (RET_WIDTH, BF16), (2 * RET_WIDTH, F32), (FOURIER_WIDTH, BF16)]
    if latent:
        pos_blk = pl.BlockSpec((rows, ATTN_WIDTH), lambda i: (i % (seq // rows), 0))
        in_specs += [pos_blk, pos_blk]
        args += list(rope)
    else:
        out_widths += [(KV_WIDTH, F32), (KV_WIDTH, F32)]
    return pl.pallas_call(
        functools.partial(_inproj_kernel, latent=latent),
        out_shape=[jax.ShapeDtypeStruct((n, w), dt) for w, dt in out_widths],
        grid=(n // rows,),
        in_specs=in_specs,
        out_specs=[row_blk(w) for w, _ in out_widths],
        compiler_params=_params("parallel"),
        name="in_projection_latent" if latent else "in_projection_context",
    )(*args)


def _spread_kv(x, group):
    lane = lax.broadcasted_iota(jnp.int32, x.shape, 1)
    other = pltpu.roll(x, HEAD_DIM, 1)
    own = (lane // HEAD_DIM) == group
    pair = jnp.where(own, x, other).astype(BF16)
    return jnp.concatenate([pair, pair], axis=1)


def _attention_kernel(*refs, past):
    if past:
        q_ref, k_ref, v_ref, ck_ref, cv_ref, o_ref, kt_ref, vt_ref = refs
    else:
        q_ref, k_ref, v_ref, o_ref, kt_ref, vt_ref = refs

    @pl.when(pl.program_id(1) == 0)
    def _():
        for g in range(ATTN_KV_HEADS):
            if past:
                kt_ref[g, 0:past, :] = _spread_kv(ck_ref[...], g)
                vt_ref[g, 0:past, :] = _spread_kv(cv_ref[...], g)
            kt_ref[g, past:, :] = _spread_kv(k_ref[...].astype(F32), g)
            vt_ref[g, past:, :] = _spread_kv(v_ref[...].astype(F32), g)

    rows = q_ref.shape[0]
    block = _lane_block((rows, GROUP_WIDTH), HEAD_DIM)
    for g in range(ATTN_KV_HEADS):
        qg = q_ref[:, g * GROUP_WIDTH:(g + 1) * GROUP_WIDTH]
        out = jnp.zeros((rows, GROUP_WIDTH), F32)
        for h in range(HEADS_PER_KV):
            qh = jnp.where(block == h, qg, jnp.zeros_like(qg))
            s = _dot_nt(qh, kt_ref[g])
            p = jnp.exp2(s - jnp.max(s, axis=-1, keepdims=True))
            denom = jnp.sum(p, axis=-1, keepdims=True)
            o = _dot(p.astype(BF16), vt_ref[g])
            out = jnp.where(block == h, o / denom, out)
        o_ref[:, g * GROUP_WIDTH:(g + 1) * GROUP_WIDTH] = out.astype(BF16)


def _attention(q, k, v, cache, layer, batch, seq):
    n = q.shape[0]
    tq = min(ATTN_Q_ROWS, seq)
    nq = seq // tq
    past = 0 if cache is None else cache[0].shape[2]
    own_kv = pl.BlockSpec((seq, KV_WIDTH), lambda b, i: (b, 0))
    in_specs = [pl.BlockSpec((tq, ATTN_WIDTH), lambda b, i: (b * nq + i, 0)), own_kv, own_kv]
    args = [q, k, v]
    if past:
        cached_kv = pl.BlockSpec((None, None, past, KV_WIDTH), lambda b, i: (b, layer, 0, 0))
        in_specs += [cached_kv, cached_kv]
        args += list(cache)
    return pl.pallas_call(
        functools.partial(_attention_kernel, past=past),
        out_shape=jax.ShapeDtypeStruct((n, MIX_WIDTH), BF16),
        grid=(batch, nq),
        in_specs=in_specs,
        out_specs=pl.BlockSpec((tq, ATTN_WIDTH), lambda b, i: (b * nq + i, 0)),
        scratch_shapes=[pltpu.VMEM((ATTN_KV_HEADS, past + seq, GROUP_WIDTH), BF16)] * 2,
        compiler_params=_params("parallel", "arbitrary"),
        name="attention_latent" if past else "attention_context",
    )(*args)


def _stack_heads(x):
    block = _lane_block(x.shape, RET_DK)
    zero = jnp.zeros_like(x)
    return jnp.concatenate([jnp.where(block == h, x, zero) for h in range(RET_HEADS)], axis=0)


def _head_norm(o, gm):
    mu = _dot(o.astype(BF16), gm)
    d = o - mu
    var = _dot((d * d).astype(BF16), gm)
    return d * lax.rsqrt(var + NORM_EPS)


def _retention_kernel(*refs, has_state, n_chunks):
    rq_ref, rk_ref, rv_ref, g_ref, gm_ref, df_ref, db_ref, qd_ref, kd_ref, cdf_ref, cdb_ref = refs[:11]
    refs = refs[12:]
    if has_state:
        s0_ref, o_ref, both_ref, upd_ref, st_ref = refs
    else:
        o_ref, sfin_ref, both_ref, upd_ref, st_ref = refs

    width = RET_WIDTH
    diag = (lax.broadcasted_iota(jnp.int32, (width, width), 0) // RET_DK
            == lax.broadcasted_iota(jnp.int32, (width, width), 1) // RET_DV)
    chunk_rows = lambda c: pl.ds(c * RET_CHUNK, RET_CHUNK)

    for c in range(n_chunks):
        rows = chunk_rows(c)
        qc, kc, vc = rq_ref[rows, :], rk_ref[rows, :], rv_ref[rows, :]
        scores = _dot_nt(qc, _stack_heads(kc))
        v_heads = _stack_heads(vc)
        both_ref[rows, 0:width] = _dot((scores * df_ref[...]).astype(BF16), v_heads)
        both_ref[rows, width:] = _dot((scores * db_ref[...]).astype(BF16), v_heads)
        k32 = kc.astype(F32)
        k_decayed = jnp.concatenate([k32, k32], axis=1) * kd_ref[...]
        upd_ref[c] = _dot(k_decayed.T.astype(BF16), vc)

    for direction, order, cd_ref in ((0, range(n_chunks), cdf_ref), (1, reversed(range(n_chunks)), cdb_ref)):
        if has_state:
            s = s0_ref[direction].reshape(width, RET_DV)
            state = jnp.where(diag, jnp.concatenate([s] * RET_HEADS, axis=1), 0.0)
        else:
            state = jnp.zeros((width, width), F32)
        for c in order:
            st_ref[direction, c] = state.astype(BF16)
            update = upd_ref[c, direction * width:(direction + 1) * width, :]
            state = cd_ref[...] * state + jnp.where(diag, update, 0.0)
        if not has_state:
            folded = state[:, 0:width // 2] + state[:, width // 2:]
            sfin_ref[direction] = folded[:, 0:RET_DV] + folded[:, RET_DV:]

    for c in range(n_chunks):
        rows = chunk_rows(c)
        q32 = rq_ref[rows, :].astype(F32)
        q_decayed = (jnp.concatenate([q32, q32], axis=1) * qd_ref[...]).astype(BF16)
        both_ref[rows, 0:width] += _dot(q_decayed[:, 0:width], st_ref[0, c])
        both_ref[rows, width:] += _dot(q_decayed[:, width:], st_ref[1, c])

    gated = _silu(g_ref[...]) * _head_norm(both_ref[...], gm_ref[...])
    o_ref[...] = (gated[:, 0:width] + gated[:, width:]).astype(BF16)


def _retention(rq, rk, rv, gates, group_mean, tables, state, mix, layer, batch, seq):
    has_state = state is not None
    n_chunks = seq // RET_CHUNK
    seq_blk = lambda width: pl.BlockSpec((seq, width), lambda b: (b, 0))
    whole = lambda a: pl.BlockSpec(a.shape, lambda b: (0,) * a.ndim)
    in_specs = ([seq_blk(RET_WIDTH)] * 3 + [seq_blk(2 * RET_WIDTH), whole(group_mean)] + [whole(t) for t in tables]
                + [pl.BlockSpec(memory_space=pl.ANY)])
    args = [rq, rk, rv, gates, group_mean] + list(tables) + [mix]
    mix_arg = len(args) - 1
    out_shape = [jax.ShapeDtypeStruct(mix.shape, mix.dtype)]
    out_specs = [pl.BlockSpec((seq, RET_WIDTH), lambda b: (b, MIX_OFF_RET // RET_WIDTH))]
    if has_state:
        in_specs.append(pl.BlockSpec((None, None, 2, RET_HEADS, RET_DK, RET_DV), lambda b: (b, layer, 0, 0, 0, 0)))
        args.append(state)
    else:
        out_shape.append(jax.ShapeDtypeStruct((batch, 2, RET_HEADS * RET_DK, RET_DV), F32))
        out_specs.append(pl.BlockSpec((None, 2, RET_HEADS * RET_DK, RET_DV), lambda b: (b, 0, 0, 0)))
    return pl.pallas_call(
        functools.partial(_retention_kernel, has_state=has_state, n_chunks=n_chunks),
        out_shape=out_shape,
        grid=(batch,),
        in_specs=in_specs,
        out_specs=out_specs,
        input_output_aliases={mix_arg: 0},
        scratch_shapes=[pltpu.VMEM((seq, 2 * RET_WIDTH), F32),
                        pltpu.VMEM((n_chunks, 2 * RET_WIDTH, RET_WIDTH), F32),
                        pltpu.VMEM((2, n_chunks, RET_WIDTH, RET_WIDTH), BF16)],
        compiler_params=_params("parallel"),
        name="retention_latent" if has_state else "retention_context",
    )(*args)


def _fourier_kernel(f_ref, cos_ref, nsin_ref, ccos_ref, csin_ref, mix_in_ref, o_ref):
    del mix_in_ref
    f = f_ref[...]
    along_c = _dot(f, ccos_ref[...]).astype(BF16)
    along_s = _dot(f, csin_ref[...]).astype(BF16)
    o_ref[...] = (_dot(cos_ref[...], along_c) + _dot(nsin_ref[...], along_s)).astype(BF16)


def _fourier(fx, tables, mix, batch, seq):
    whole = lambda a: pl.BlockSpec(a.shape, lambda b: (0,) * a.ndim)
    return pl.pallas_call(
        _fourier_kernel,
        out_shape=jax.ShapeDtypeStruct(mix.shape, mix.dtype),
        grid=(batch,),
        in_specs=[pl.BlockSpec((seq, FOURIER_WIDTH), lambda b: (b, 0))] + [whole(t) for t in tables]
        + [pl.BlockSpec(memory_space=pl.ANY)],
        out_specs=pl.BlockSpec((seq, FOURIER_WIDTH), lambda b: (b, MIX_OFF_FOURIER // FOURIER_WIDTH)),
        input_output_aliases={1 + len(tables): 0},
        compiler_params=_params("parallel"),
        name="fourier_seq%d" % seq,
    )(fx, *tables, mix)


def _shift_rows(u, seq):
    rows = u.shape[0]
    pos = lax.broadcasted_iota(jnp.int32, u.shape, 0) % seq
    prev = jnp.where(pos == 0, 0.0, pltpu.roll(u, 1, 0))
    nxt = jnp.where(pos == seq - 1, 0.0, pltpu.roll(u, rows - 1, 0))
    return prev, nxt


def _mix_ffn_kernel(*refs, seq, halo, final):
    if halo:
        (x_ref, x_top_ref, x_bot_ref, mix_ref, mix_top_ref, mix_bot_ref,
         mod_ref, wout_ref, n2_ref, wup_ref, cw_ref, cb_ref, wd_ref, fin_ref, o_ref) = refs
    else:
        x_ref, mix_ref, mod_ref, wout_ref, n2_ref, wup_ref, cw_ref, cb_ref, wd_ref, fin_ref, o_ref = refs
    rows = x_ref.shape[0]
    modulate = lambda x: _rms_rows(x, n2_ref[...]) * (1.0 + mod_ref[4:5, :]) + mod_ref[3:4, :]
    if halo:
        pad = x_top_ref.shape[0]
        skip = mix_top_ref.shape[0] - pad
        tiles_per_seq = seq // rows
        place = pl.program_id(0) % tiles_per_seq
        mixed = _dot(jnp.concatenate([mix_top_ref[...], mix_ref[...], mix_bot_ref[...]], axis=0), wout_ref[...])
        x_ext = jnp.concatenate([x_top_ref[...], x_ref[...], x_bot_ref[...]], axis=0)
        x1_ext = x_ext + mod_ref[2:3, :] * mixed[skip:skip + pad + rows + pad, :]
        x1 = x1_ext[pad:pad + rows, :]
        h2 = modulate(x1_ext)
        row = lax.broadcasted_iota(jnp.int32, h2.shape, 0)
        first_kept = jnp.where(place == 0, pad, 0)
        end_kept = jnp.where(place == tiles_per_seq - 1, pad + rows, pad + rows + pad)
        h2 = jnp.where((row >= first_kept) & (row < end_kept), h2, 0.0)
        u = _dot(h2.astype(BF16), wup_ref[...])
        prev, nxt = pltpu.roll(u, 1, 0), pltpu.roll(u, u.shape[0] - 1, 0)
    else:
        x1 = x_ref[...] + mod_ref[2:3, :] * _dot(mix_ref[...], wout_ref[...])
        u = _dot(modulate(x1).astype(BF16), wup_ref[...])
        prev, nxt = _shift_rows(u, seq)
    u = prev * cw_ref[0:1, :] + u * cw_ref[1:2, :] + nxt * cw_ref[2:3, :] + cb_ref[...]
    if halo:
        u = u[pad:pad + rows, :]
    act = _silu(u[:, 0:D_FF]) * u[:, D_FF:]
    y = x1 + mod_ref[5:6, :] * _dot(act.astype(BF16), wd_ref[...])
    o_ref[...] = _rms_rows(y, fin_ref[...]) if final else y


def _mix_ffn(x, mix, mod, w_out, norm2_w, w_up, conv_w, conv_b, w_down, final_w, layer, seq, per_seq_mod, final):
    n = x.shape[0]
    rows = FFN_ROWS
    halo = seq > rows
    assert (seq % rows == 0) if halo else (rows % seq == 0)
    row_blk = lambda width: pl.BlockSpec((rows, width), lambda i: (i, 0))
    whole = lambda shape: pl.BlockSpec(shape, lambda i: (0,) * len(shape))
    resident = lambda r, c: pl.BlockSpec((None, r, c), lambda i: (layer, 0, 0), pipeline_mode=pl.Buffered(1))
    mod_idx = (lambda i: (i * rows // seq, 0, 0)) if per_seq_mod else (lambda i: (0, 0, 0))

    def with_halo(a, pad):
        width = a.shape[1]
        per_tile, last = rows // pad, n // pad - 1
        return ([row_blk(width),
                 pl.BlockSpec((pad, width), lambda i: (jnp.maximum(i * per_tile - 1, 0), 0)),
                 pl.BlockSpec((pad, width), lambda i: (jnp.minimum((i + 1) * per_tile, last), 0))], [a, a, a])

    if halo:
        x_specs, x_args = with_halo(x, F32_SUBLANES)
        mix_specs, mix_args = with_halo(mix, BF16_SUBLANES)
    else:
        x_specs, x_args, mix_specs, mix_args = [row_blk(D_MODEL)], [x], [row_blk(MIX_WIDTH)], [mix]
    in_specs = x_specs + mix_specs + [
        pl.BlockSpec((None, N_MOD, D_MODEL), mod_idx), resident(MIX_WIDTH, D_MODEL), whole((1, D_MODEL)),
        resident(D_MODEL, 2 * D_FF), whole((3, 2 * D_FF)), whole((1, 2 * D_FF)),
        resident(D_FF, D_MODEL), whole((1, D_MODEL))]
    args = x_args + mix_args + [mod, w_out, norm2_w, w_up, conv_w, conv_b, w_down, final_w]
    return pl.pallas_call(
        functools.partial(_mix_ffn_kernel, seq=seq, halo=halo, final=final),
        out_shape=jax.ShapeDtypeStruct((n, D_MODEL), F32),
        grid=(n // rows,),
        in_specs=in_specs,
        out_specs=row_blk(D_MODEL),
        compiler_params=_params("parallel"),
        name="mix_ffn_seq%d" % seq,
    )(*args)


def _layer(x, mod, lp, consts, layer, batch, seq, ctx, final):
    latent = ctx is not None
    outs = _in_projection(x, mod, lp["norm1_w"], consts["w_in"], lp["q_norm_w"], lp["k_norm_w"], consts["gm_attn"],
                          consts["rope"] if latent else None, layer, seq)
    q, k, v, rq, rk, rv, gates, fx = outs[:8]
    mix = _attention(q, k, v, ctx[:2] if latent else None, layer, batch, seq)
    ret_out = _retention(rq, rk, rv, gates, consts["gm_ret"], consts["ret_tables"], ctx[2] if latent else None,
                         mix, layer, batch, seq)
    mix = _fourier(fx, consts["dft"][seq], ret_out[0], batch, seq)
    x = _mix_ffn(x, mix, mod, consts["w_out"], lp["norm2_w"], consts["w_up"], lp["conv_w"], lp["conv_b"],
                 consts["w_down"], consts["final_w"], layer, seq, latent, final)
    if latent:
        return x, None
    return x, (outs[8], outs[9], ret_out[1])


def kernel(x_prompt, x_sample, c, cache_attn_k, cache_attn_v, state_ret, c_ctx, w_mod, b_mod, norm1_w, w_in,
           q_norm_w, k_norm_w, w_out, norm2_w, w_up, conv_w, conv_b, w_down, final_norm_w):
    batch, seq, d = x_prompt.shape
    dec_batch, dec_seq, _ = x_sample.shape
    past = cache_attn_k.shape[2]
    assert d == D_MODEL and w_in.shape == (DEPTH, D_MODEL, IN_WIDTH) and w_up.shape == (DEPTH, D_MODEL, 2 * D_FF)
    assert (batch * seq) % FFN_ROWS == 0 and dec_seq % FFN_ROWS == 0
    assert seq % min(ATTN_Q_ROWS, seq) == 0 and dec_seq % min(ATTN_Q_ROWS, dec_seq) == 0
    assert INPROJ_ROWS % seq == 0 and dec_seq % INPROJ_ROWS == 0 and seq % RET_CHUNK == 0 and dec_seq % RET_CHUNK == 0
    assert dec_batch + 1 <= MOD_ROWS and dec_seq % GRID_W == 0

    consts = {
        "gm_attn": _group_mean_matrix(ATTN_WIDTH, HEAD_DIM),
        "gm_ret": _group_mean_matrix(2 * RET_WIDTH, RET_DV),
        "rope": _rope_tables(dec_seq),
        "ret_tables": _retention_tables(),
        "dft": {s: _dft_tables(s) for s in {seq, dec_seq}},
        "final_w": final_norm_w.reshape(1, D_MODEL),
        "w_in": w_in.astype(BF16),
        "w_out": w_out.astype(BF16),
        "w_up": w_up.astype(BF16),
        "w_down": w_down.astype(BF16),
    }

    cvec = jnp.zeros((MOD_ROWS, D_MODEL), F32).at[0].set(c_ctx).at[1:1 + dec_batch].set(c)
    mod = _modulation(cvec, w_mod, b_mod).reshape(DEPTH, MOD_ROWS, N_MOD, D_MODEL)

    cache_k = cache_attn_k.reshape(dec_batch, DEPTH, past, KV_WIDTH)
    cache_v = cache_attn_v.reshape(dec_batch, DEPTH, past, KV_WIDTH)

    xp = x_prompt.reshape(batch * seq, D_MODEL)
    xs = x_sample.reshape(dec_batch * dec_seq, D_MODEL)
    new_k, new_v, new_s = [], [], []
    for layer in range(DEPTH):
        lp = {
            "norm1_w": norm1_w[layer].reshape(1, D_MODEL),
            "q_norm_w": jnp.tile(q_norm_w[layer], ATTN_HEADS).reshape(1, ATTN_WIDTH),
            "k_norm_w": jnp.tile(k_norm_w[layer], ATTN_KV_HEADS).reshape(1, KV_WIDTH),
            "norm2_w": norm2_w[layer].reshape(1, D_MODEL),
            "conv_w": conv_w[layer],
            "conv_b": conv_b[layer].reshape(1, 2 * D_FF),
        }
        final = layer == DEPTH - 1
        xp, (k_l, v_l, s_l) = _layer(xp, mod[layer, 0:1], lp, consts, layer, batch, seq, None, final)
        new_k.append(k_l.reshape(batch, seq, ATTN_KV_HEADS, HEAD_DIM))
        new_v.append(v_l.reshape(batch, seq, ATTN_KV_HEADS, HEAD_DIM))
        new_s.append(s_l.reshape(batch, 2, RET_HEADS, RET_DK, RET_DV))
        xs, _ = _layer(xs, mod[layer, 1:1 + dec_batch], lp, consts, layer, dec_batch, dec_seq,
                       (cache_k, cache_v, state_ret), final)
    return (xp.reshape(batch, seq, D_MODEL), xs.reshape(dec_batch, dec_seq, D_MODEL),
            jnp.stack(new_k, axis=1), jnp.stack(new_v, axis=1), jnp.stack(new_s, axis=1))
```

```python
import functools

import jax
import jax.numpy as jnp
import numpy as np
from jax import lax
from jax.experimental import pallas as pl
from jax.experimental.pallas import tpu as pltpu

F32 = jnp.float32
BF16 = jnp.bfloat16

D_MODEL = 1024
DEPTH = 2
GRID_W = 64
NORM_EPS = 1e-6
ATTN_HEADS = 8
ATTN_KV_HEADS = 2
HEAD_DIM = 64
ATTN_WIDTH = ATTN_HEADS * HEAD_DIM
KV_WIDTH = ATTN_KV_HEADS * HEAD_DIM
HEADS_PER_KV = ATTN_HEADS // ATTN_KV_HEADS
GROUP_WIDTH = HEADS_PER_KV * HEAD_DIM
ATTN_LOGIT_SCALE = HEAD_DIM ** -0.5 * 1.4426950408889634
ROPE_THETA = 10000.0
ROPE_AXIS_DIM = HEAD_DIM // 2
ROPE_HALF = ROPE_AXIS_DIM // 2
RET_HEADS = 4
RET_DK = 64
RET_DV = 64
RET_WIDTH = RET_HEADS * RET_DV
RET_CHUNK = 128
RET_DECAY_EXP_FWD = 5.0
RET_DECAY_EXP_BWD = 5.5
FOURIER_GROUPS = 4
FOURIER_DIM = 64
FOURIER_WIDTH = FOURIER_GROUPS * FOURIER_DIM
D_FF = 2816
N_MOD = 6

OFF_Q = 0
OFF_K = OFF_Q + ATTN_WIDTH
OFF_V = OFF_K + KV_WIDTH
OFF_RQ = OFF_V + KV_WIDTH
OFF_RK = OFF_RQ + RET_WIDTH
OFF_RV = OFF_RK + RET_WIDTH
OFF_GF = OFF_RV + RET_WIDTH
OFF_GB = OFF_GF + RET_WIDTH
OFF_FX = OFF_GB + RET_WIDTH
IN_WIDTH = OFF_FX + FOURIER_WIDTH
MIX_OFF_ATTN = 0
MIX_OFF_RET = MIX_OFF_ATTN + ATTN_WIDTH
MIX_OFF_FOURIER = MIX_OFF_RET + RET_WIDTH
MIX_WIDTH = MIX_OFF_FOURIER + FOURIER_WIDTH

V7X_VMEM_BYTES = 64 * 1024 * 1024
VMEM_LIMIT = V7X_VMEM_BYTES - 12 * 1024 * 1024

MOD_ROWS = 16
MOD_COLS = 1536
INPROJ_ROWS = 1024
ATTN_Q_ROWS = 512
FFN_ROWS = 512
F32_SUBLANES = 8
BF16_SUBLANES = 16


def _params(*semantics):
    return pltpu.CompilerParams(dimension_semantics=semantics, vmem_limit_bytes=VMEM_LIMIT)


def _dot(a, b):
    return jnp.dot(a, b, preferred_element_type=F32)


def _dot_nt(a, b):
    return lax.dot_general(a, b, (((1,), (1,)), ((), ())), preferred_element_type=F32)


def _sigmoid(x):
    return 1.0 / (1.0 + jnp.exp(-x))


def _silu(x):
    return x * _sigmoid(x)


def _rms_rows(x, w):
    ms = jnp.mean(x * x, axis=-1, keepdims=True)
    return x * lax.rsqrt(ms + NORM_EPS) * w


def _lane_block(shape, width):
    return lax.broadcasted_iota(jnp.int32, shape, len(shape) - 1) // width


def _group_mean_matrix(width, group):
    idx = np.arange(width) // group
    return jnp.asarray((idx[:, None] == idx[None, :]).astype(np.float32) / group, dtype=BF16)


def _rope_tables(n_tokens):
    pos = np.arange(n_tokens)
    row = (pos // GRID_W).astype(np.float64)
    col = (pos % GRID_W).astype(np.float64)
    freqs = ROPE_THETA ** (-np.arange(ROPE_HALF, dtype=np.float64) / ROPE_HALF)
    d = np.arange(HEAD_DIM)
    coord = np.where((d // ROPE_AXIS_DIM)[None, :] == 0, row[:, None], col[:, None])
    ang = coord * freqs[d % ROPE_HALF][None, :]
    sign = np.where((d & ROPE_HALF) == 0, -1.0, 1.0)[None, :]
    cos = np.tile(np.cos(ang), (1, ATTN_HEADS))
    sin = np.tile(np.sin(ang) * sign, (1, ATTN_HEADS))
    return jnp.asarray(cos, F32), jnp.asarray(sin, F32)


def _dft_tables(seq, rows):
    n = np.arange(rows)
    same_seq = (n[:, None] // seq) == (n[None, :] // seq)
    ang = 2.0 * np.pi * (((n % seq)[:, None] * (n % seq)[None, :]) % seq) / seq
    scale = 1.0 / np.sqrt(seq * FOURIER_DIM)
    position = np.concatenate([np.where(same_seq, np.cos(ang), 0.0), np.where(same_seq, -np.sin(ang), 0.0)], axis=1)
    c = np.arange(FOURIER_WIDTH)
    same = (c[:, None] // FOURIER_DIM) == (c[None, :] // FOURIER_DIM)
    angc = 2.0 * np.pi * (((c % FOURIER_DIM)[:, None] * (c % FOURIER_DIM)[None, :]) % FOURIER_DIM) / FOURIER_DIM
    channel = np.concatenate([np.where(same, np.cos(angc), 0.0), np.where(same, np.sin(angc), 0.0)], axis=1)
    as_bf16 = lambda a: jnp.asarray(a, F32).astype(BF16)
    return as_bf16(channel), as_bf16(position * scale)


def _retention_tables():
    heads = jnp.arange(RET_HEADS, dtype=F32)
    idx = jnp.arange(RET_CHUNK, dtype=F32)
    diff = idx[:, None] - idx[None, :]
    out = []
    for exp0, backward in ((RET_DECAY_EXP_FWD, False), (RET_DECAY_EXP_BWD, True)):
        lg = jnp.log1p(-jnp.exp2(-(exp0 + heads)))
        dd = -diff if backward else diff
        inner = jnp.where(dd[None] >= 0, jnp.exp(jnp.maximum(dd, 0.0)[None] * lg[:, None, None]), 0.0)
        inner = inner.transpose(1, 0, 2).reshape(RET_CHUNK, RET_HEADS * RET_CHUNK)
        q_pow = (RET_CHUNK - idx) if backward else (idx + 1.0)
        k_pow = idx if backward else (RET_CHUNK - 1.0 - idx)
        spread = lambda p: jnp.repeat(jnp.exp(p[:, None] * lg[None, :]), RET_DK, axis=1)
        q_decay, k_decay = spread(q_pow), spread(k_pow)
        chunk_decay = jnp.repeat(jnp.exp(RET_CHUNK * lg), RET_DV)[None, :]
        out.append((inner, q_decay, k_decay, chunk_decay))
    (inner_f, qd_f, kd_f, cd_f), (inner_b, qd_b, kd_b, cd_b) = out
    return (inner_f, inner_b, jnp.concatenate([qd_f, qd_b], axis=1), jnp.concatenate([kd_f, kd_b], axis=1), cd_f, cd_b)


def _mod_kernel(c_ref, w_ref, b_ref, o_ref):
    act = _silu(c_ref[...]).astype(BF16)
    o_ref[...] = _dot(act, w_ref[...].astype(BF16)) + b_ref[...]


def _modulation(cvec, w_mod, b_mod):
    n_cols = w_mod.shape[-1]
    return pl.pallas_call(
        _mod_kernel,
        out_shape=jax.ShapeDtypeStruct((DEPTH, MOD_ROWS, n_cols), F32),
        grid=(DEPTH, n_cols // MOD_COLS),
        in_specs=[
            pl.BlockSpec((MOD_ROWS, D_MODEL), lambda l, j: (0, 0)),
            pl.BlockSpec((None, D_MODEL, MOD_COLS), lambda l, j: (l, 0, j)),
            pl.BlockSpec((None, 1, MOD_COLS), lambda l, j: (l, 0, j)),
        ],
        out_specs=pl.BlockSpec((None, MOD_ROWS, MOD_COLS), lambda l, j: (l, 0, j)),
        compiler_params=_params("parallel", "parallel"),
        name="modulation",
    )(cvec, w_mod, b_mod.reshape(DEPTH, 1, n_cols))


def _swap_rotary_pairs(x):
    width = x.shape[-1]
    lane = lax.broadcasted_iota(jnp.int32, x.shape, 1)
    from_below = pltpu.roll(x, ROPE_HALF, 1)
    from_above = pltpu.roll(x, width - ROPE_HALF, 1)
    return jnp.where((lane & ROPE_HALF) != 0, from_below, from_above)


def _inproj_kernel(*refs, latent):
    x_ref, mod_ref, n1_ref, win_ref, qnw_ref, knw_ref, gm_ref, dft_c_ref, dft_p_ref = refs[:9]
    refs = refs[9:]
    if latent:
        cos_ref, sin_ref = refs[:2]
        refs = refs[2:]
        q_ref, k_ref, v_ref, rq_ref, rk_ref, rv_ref, g_ref, four_ref = refs
    else:
        q_ref, k_ref, v_ref, rq_ref, rk_ref, rv_ref, g_ref, four_ref, k32_ref, v32_ref = refs

    shift, scale = mod_ref[0:1, :], mod_ref[1:2, :]
    h = _rms_rows(x_ref[...], n1_ref[...]) * (1.0 + scale) + shift
    hb = h.astype(BF16)
    along = _dot(_dot(hb, win_ref[:, OFF_FX:]).astype(BF16), dft_c_ref[...])
    stacked = jnp.concatenate([along[:, 0:FOURIER_WIDTH], along[:, FOURIER_WIDTH:]], axis=0).astype(BF16)
    four_ref[...] = _dot(dft_p_ref[...], stacked).astype(BF16)

    all_proj = _dot(hb, win_ref[:, 0:OFF_FX])
    proj = lambda off, width: all_proj[:, off:off + width]

    q = proj(OFF_Q, ATTN_WIDTH)
    q = q * lax.rsqrt(_dot((q * q).astype(BF16), gm_ref[...]) + NORM_EPS) * qnw_ref[...]
    k = proj(OFF_K, KV_WIDTH)
    k = k * lax.rsqrt(_dot((k * k).astype(BF16), gm_ref[0:KV_WIDTH, 0:KV_WIDTH]) + NORM_EPS) * knw_ref[...]
    v = proj(OFF_V, KV_WIDTH)
    if latent:
        cos, sin = cos_ref[...], sin_ref[...]
        q = q * cos + _swap_rotary_pairs(q) * sin
        k = k * cos[:, 0:KV_WIDTH] + _swap_rotary_pairs(k) * sin[:, 0:KV_WIDTH]
    else:
        k32_ref[...] = k
        v32_ref[...] = v
    q_ref[...] = (q * ATTN_LOGIT_SCALE).astype(BF16)
    k_ref[...] = k.astype(BF16)
    v_ref[...] = v.astype(BF16)
    rq_ref[...] = proj(OFF_RQ, RET_WIDTH).astype(BF16)
    rk_ref[...] = (proj(OFF_RK, RET_WIDTH) * RET_DK ** -0.5).astype(BF16)
    rv_ref[...] = proj(OFF_RV, RET_WIDTH).astype(BF16)
    g_ref[...] = proj(OFF_GF, 2 * RET_WIDTH)


def _in_projection(x, mod, norm1_w, w_in, q_norm_w, k_norm_w, group_mean, dft, rope, layer, seq):
    n = x.shape[0]
    rows = INPROJ_ROWS
    latent = rope is not None
    row_blk = lambda width, col=0: pl.BlockSpec((rows, width), lambda i: (i, col))
    whole = lambda shape: pl.BlockSpec(shape, lambda i: (0,) * len(shape))
    once = lambda shape: pl.BlockSpec(shape, lambda i: (0,) * len(shape), pipeline_mode=pl.Buffered(1))
    mod_idx = (lambda i: (i * rows // seq, 0, 0)) if latent else (lambda i: (0, 0, 0))
    in_specs = [
        row_blk(D_MODEL),
        pl.BlockSpec((None, N_MOD, D_MODEL), mod_idx),
        whole((1, D_MODEL)),
        pl.BlockSpec((None, D_MODEL, IN_WIDTH), lambda i: (layer, 0, 0), pipeline_mode=pl.Buffered(1)),
        whole((1, ATTN_WIDTH)),
        whole((1, KV_WIDTH)),
        whole((ATTN_WIDTH, ATTN_WIDTH)),
        once(dft[0].shape),
        once(dft[1].shape),
    ]
    args = [x, mod, norm1_w, w_in, q_norm_w, k_norm_w, group_mean, dft[0], dft[1]]
    outs = [(ATTN_WIDTH, BF16), (KV_WIDTH, BF16), (KV_WIDTH, BF16), (RET_WIDTH, BF16), (RET_WIDTH, BF16),
            (RET_WIDTH, BF16), (2 * RET_WIDTH, F32)]
    out_shape = [jax.ShapeDtypeStruct((n, w), dt) for w, dt in outs] + [jax.ShapeDtypeStruct((n, MIX_WIDTH), BF16)]
    out_specs = [row_blk(w) for w, _ in outs] + [row_blk(FOURIER_WIDTH, MIX_OFF_FOURIER // FOURIER_WIDTH)]
    if latent:
        pos_blk = pl.BlockSpec((rows, ATTN_WIDTH), lambda i: (i % (seq // rows), 0))
        in_specs += [pos_blk, pos_blk]
        args += list(rope)
    else:
        out_shape += [jax.ShapeDtypeStruct((n, KV_WIDTH), F32)] * 2
        out_specs += [row_blk(KV_WIDTH)] * 2
    return pl.pallas_call(
        functools.partial(_inproj_kernel, latent=latent),
        out_shape=out_shape,
        grid=(n // rows,),
        in_specs=in_specs,
        out_specs=out_specs,
        compiler_params=_params("parallel"),
        name="in_projection_latent" if latent else "in_projection_context",
    )(*args)


def _spread_kv(x, group):
    lane = lax.broadcasted_iota(jnp.int32, x.shape, 1)
    other = pltpu.roll(x, HEAD_DIM, 1)
    own = (lane // HEAD_DIM) == group
    pair = jnp.where(own, x, other).astype(BF16)
    return jnp.concatenate([pair, pair], axis=1)


def _attention_kernel(*refs, past):
    if past:
        q_ref, k_ref, v_ref, ck_ref, cv_ref, _, o_ref, kt_ref, vt_ref = refs
    else:
        q_ref, k_ref, v_ref, _, o_ref, kt_ref, vt_ref = refs

    @pl.when(pl.program_id(1) == 0)
    def _():
        for g in range(ATTN_KV_HEADS):
            if past:
                kt_ref[g, 0:past, :] = _spread_kv(ck_ref[...], g)
                vt_ref[g, 0:past, :] = _spread_kv(cv_ref[...], g)
            kt_ref[g, past:, :] = _spread_kv(k_ref[...].astype(F32), g)
            vt_ref[g, past:, :] = _spread_kv(v_ref[...].astype(F32), g)

    rows = q_ref.shape[0]
    block = _lane_block((rows, GROUP_WIDTH), HEAD_DIM)
    for g in range(ATTN_KV_HEADS):
        qg = q_ref[:, g * GROUP_WIDTH:(g + 1) * GROUP_WIDTH]
        out = jnp.zeros((rows, GROUP_WIDTH), F32)
        for h in range(HEADS_PER_KV):
            qh = jnp.where(block == h, qg, jnp.zeros_like(qg))
            s = _dot_nt(qh, kt_ref[g])
            p = jnp.exp2(s - jnp.max(s, axis=-1, keepdims=True))
            denom = jnp.sum(p, axis=-1, keepdims=True)
            o = _dot(p.astype(BF16), vt_ref[g])
            out = jnp.where(block == h, o / denom, out)
        o_ref[:, g * GROUP_WIDTH:(g + 1) * GROUP_WIDTH] = out.astype(BF16)


def _attention(q, k, v, cache, mix, layer, batch, seq):
    tq = min(ATTN_Q_ROWS, seq)
    nq = seq // tq
    past = 0 if cache is None else cache[0].shape[2]
    own_kv = pl.BlockSpec((seq, KV_WIDTH), lambda b, i: (b, 0))
    in_specs = [pl.BlockSpec((tq, ATTN_WIDTH), lambda b, i: (b * nq + i, 0)), own_kv, own_kv]
    args = [q, k, v]
    if past:
        cached_kv = pl.BlockSpec((None, None, past, KV_WIDTH), lambda b, i: (b, layer, 0, 0))
        in_specs += [cached_kv, cached_kv]
        args += list(cache)
    in_specs.append(pl.BlockSpec(memory_space=pl.ANY))
    args.append(mix)
    return pl.pallas_call(
        functools.partial(_attention_kernel, past=past),
        out_shape=jax.ShapeDtypeStruct(mix.shape, mix.dtype),
        grid=(batch, nq),
        in_specs=in_specs,
        out_specs=pl.BlockSpec((tq, ATTN_WIDTH), lambda b, i: (b * nq + i, MIX_OFF_ATTN // ATTN_WIDTH)),
        input_output_aliases={len(args) - 1: 0},
        scratch_shapes=[pltpu.VMEM((ATTN_KV_HEADS, past + seq, GROUP_WIDTH), BF16)] * 2,
        compiler_params=_params("parallel", "arbitrary"),
        name="attention_latent" if past else "attention_context",
    )(*args)


def _stack_heads(x):
    block = _lane_block(x.shape, RET_DK)
    zero = jnp.zeros_like(x)
    return jnp.concatenate([jnp.where(block == h, x, zero) for h in range(RET_HEADS)], axis=0)


def _head_norm(o, gm):
    mu = _dot(o.astype(BF16), gm)
    d = o - mu
    var = _dot((d * d).astype(BF16), gm)
    return d * lax.rsqrt(var + NORM_EPS)


def _retention_kernel(*refs, has_state, n_chunks):
    rq_ref, rk_ref, rv_ref, g_ref, gm_ref, df_ref, db_ref, qd_ref, kd_ref, cdf_ref, cdb_ref = refs[:11]
    refs = refs[12:]
    if has_state:
        s0_ref, o_ref, both_ref, upd_ref, st_ref = refs
    else:
        o_ref, sfin_ref, both_ref, upd_ref, st_ref = refs

    width = RET_WIDTH
    diag = (lax.broadcasted_iota(jnp.int32, (width, width), 0) // RET_DK
            == lax.broadcasted_iota(jnp.int32, (width, width), 1) // RET_DV)
    chunk_rows = lambda c: pl.ds(c * RET_CHUNK, RET_CHUNK)

    for c in range(n_chunks):
        rows = chunk_rows(c)
        qc, kc, vc = rq_ref[rows, :], rk_ref[rows, :], rv_ref[rows, :]
        scores = _dot_nt(qc, _stack_heads(kc))
        v_heads = _stack_heads(vc)
        both_ref[rows, 0:width] = _dot((scores * df_ref[...]).astype(BF16), v_heads)
        both_ref[rows, width:] = _dot((scores * db_ref[...]).astype(BF16), v_heads)
        k32 = kc.astype(F32)
        k_decayed = jnp.concatenate([k32, k32], axis=1) * kd_ref[...]
        upd_ref[c] = _dot(k_decayed.T.astype(BF16), vc)

    for direction, order, cd_ref in ((0, range(n_chunks), cdf_ref), (1, reversed(range(n_chunks)), cdb_ref)):
        if has_state:
            s = s0_ref[direction].reshape(width, RET_DV)
            state = jnp.where(diag, jnp.concatenate([s] * RET_HEADS, axis=1), 0.0)
        else:
            state = jnp.zeros((width, width), F32)
        for c in order:
            st_ref[direction, c] = state.astype(BF16)
            update = upd_ref[c, direction * width:(direction + 1) * width, :]
            state = cd_ref[...] * state + jnp.where(diag, update, 0.0)
        if not has_state:
            folded = state[:, 0:width // 2] + state[:, width // 2:]
            sfin_ref[direction] = folded[:, 0:RET_DV] + folded[:, RET_DV:]

    for c in range(n_chunks):
        rows = chunk_rows(c)
        q32 = rq_ref[rows, :].astype(F32)
        q_decayed = (jnp.concatenate([q32, q32], axis=1) * qd_ref[...]).astype(BF16)
        both_ref[rows, 0:width] += _dot(q_decayed[:, 0:width], st_ref[0, c])
        both_ref[rows, width:] += _dot(q_decayed[:, width:], st_ref[1, c])

    gated = _silu(g_ref[...]) * _head_norm(both_ref[...], gm_ref[...])
    o_ref[...] = (gated[:, 0:width] + gated[:, width:]).astype(BF16)


def _retention(rq, rk, rv, gates, group_mean, tables, state, mix, layer, batch, seq):
    has_state = state is not None
    n_chunks = seq // RET_CHUNK
    seq_blk = lambda width: pl.BlockSpec((seq, width), lambda b: (b, 0))
    whole = lambda a: pl.BlockSpec(a.shape, lambda b: (0,) * a.ndim)
    in_specs = ([seq_blk(RET_WIDTH)] * 3 + [seq_blk(2 * RET_WIDTH), whole(group_mean)] + [whole(t) for t in tables]
                + [pl.BlockSpec(memory_space=pl.ANY)])
    args = [rq, rk, rv, gates, group_mean] + list(tables) + [mix]
    mix_arg = len(args) - 1
    out_shape = [jax.ShapeDtypeStruct(mix.shape, mix.dtype)]
    out_specs = [pl.BlockSpec((seq, RET_WIDTH), lambda b: (b, MIX_OFF_RET // RET_WIDTH))]
    if has_state:
        in_specs.append(pl.BlockSpec((None, None, 2, RET_HEADS, RET_DK, RET_DV), lambda b: (b, layer, 0, 0, 0, 0)))
        args.append(state)
    else:
        out_shape.append(jax.ShapeDtypeStruct((batch, 2, RET_HEADS * RET_DK, RET_DV), F32))
        out_specs.append(pl.BlockSpec((None, 2, RET_HEADS * RET_DK, RET_DV), lambda b: (b, 0, 0, 0)))
    return pl.pallas_call(
        functools.partial(_retention_kernel, has_state=has_state, n_chunks=n_chunks),
        out_shape=out_shape,
        grid=(batch,),
        in_specs=in_specs,
        out_specs=out_specs,
        input_output_aliases={mix_arg: 0},
        scratch_shapes=[pltpu.VMEM((seq, 2 * RET_WIDTH), F32),
                        pltpu.VMEM((n_chunks, 2 * RET_WIDTH, RET_WIDTH), F32),
                        pltpu.VMEM((2, n_chunks, RET_WIDTH, RET_WIDTH), BF16)],
        compiler_params=_params("parallel"),
        name="retention_latent" if has_state else "retention_context",
    )(*args)


def _shift_rows(u, seq):
    rows = u.shape[0]
    pos = lax.broadcasted_iota(jnp.int32, u.shape, 0) % seq
    prev = jnp.where(pos == 0, 0.0, pltpu.roll(u, 1, 0))
    nxt = jnp.where(pos == seq - 1, 0.0, pltpu.roll(u, rows - 1, 0))
    return prev, nxt


def _mix_ffn_kernel(*refs, seq, halo, final):
    if halo:
        (x_ref, x_top_ref, x_bot_ref, mix_ref, mix_top_ref, mix_bot_ref,
         mod_ref, wout_ref, n2_ref, wup_ref, cw_ref, cb_ref, wd_ref, fin_ref, o_ref) = refs
    else:
        x_ref, mix_ref, mod_ref, wout_ref, n2_ref, wup_ref, cw_ref, cb_ref, wd_ref, fin_ref, o_ref = refs
    rows = x_ref.shape[0]
    modulate = lambda x: _rms_rows(x, n2_ref[...]) * (1.0 + mod_ref[4:5, :]) + mod_ref[3:4, :]
    if halo:
        pad = x_top_ref.shape[0]
        skip = mix_top_ref.shape[0] - pad
        tiles_per_seq = seq // rows
        place = pl.program_id(0) % tiles_per_seq
        mixed = _dot(jnp.concatenate([mix_top_ref[...], mix_ref[...], mix_bot_ref[...]], axis=0), wout_ref[...])
        x_ext = jnp.concatenate([x_top_ref[...], x_ref[...], x_bot_ref[...]], axis=0)
        x1_ext = x_ext + mod_ref[2:3, :] * mixed[skip:skip + pad + rows + pad, :]
        x1 = x1_ext[pad:pad + rows, :]
        h2 = modulate(x1_ext)
        row = lax.broadcasted_iota(jnp.int32, h2.shape, 0)
        first_kept = jnp.where(place == 0, pad, 0)
        end_kept = jnp.where(place == tiles_per_seq - 1, pad + rows, pad + rows + pad)
        h2 = jnp.where((row >= first_kept) & (row < end_kept), h2, 0.0)
        u = _dot(h2.astype(BF16), wup_ref[...])
        prev, nxt = pltpu.roll(u, 1, 0), pltpu.roll(u, u.shape[0] - 1, 0)
    else:
        x1 = x_ref[...] + mod_ref[2:3, :] * _dot(mix_ref[...], wout_ref[...])
        u = _dot(modulate(x1).astype(BF16), wup_ref[...])
        prev, nxt = _shift_rows(u, seq)
    u = prev * cw_ref[0:1, :] + u * cw_ref[1:2, :] + nxt * cw_ref[2:3, :] + cb_ref[...]
    if halo:
        u = u[pad:pad + rows, :]
    act = _silu(u[:, 0:D_FF]) * u[:, D_FF:]
    y = x1 + mod_ref[5:6, :] * _dot(act.astype(BF16), wd_ref[...])
    o_ref[...] = _rms_rows(y, fin_ref[...]) if final else y


def _mix_ffn(x, mix, mod, w_out, norm2_w, w_up, conv_w, conv_b, w_down, final_w, layer, seq, per_seq_mod, final):
    n = x.shape[0]
    rows = FFN_ROWS
    halo = seq > rows
    assert (seq % rows == 0) if halo else (rows % seq == 0)
    row_blk = lambda width: pl.BlockSpec((rows, width), lambda i: (i, 0))
    whole = lambda shape: pl.BlockSpec(shape, lambda i: (0,) * len(shape))
    resident = lambda r, c: pl.BlockSpec((None, r, c), lambda i: (layer, 0, 0), pipeline_mode=pl.Buffered(1))
    mod_idx = (lambda i: (i * rows // seq, 0, 0)) if per_seq_mod else (lambda i: (0, 0, 0))

    def with_halo(a, pad):
        width = a.shape[1]
        per_tile, last = rows // pad, n // pad - 1
        return ([row_blk(width),
                 pl.BlockSpec((pad, width), lambda i: (jnp.maximum(i * per_tile - 1, 0), 0)),
                 pl.BlockSpec((pad, width), lambda i: (jnp.minimum((i + 1) * per_tile, last), 0))], [a, a, a])

    if halo:
        x_specs, x_args = with_halo(x, F32_SUBLANES)
        mix_specs, mix_args = with_halo(mix, BF16_SUBLANES)
    else:
        x_specs, x_args, mix_specs, mix_args = [row_blk(D_MODEL)], [x], [row_blk(MIX_WIDTH)], [mix]
    in_specs = x_specs + mix_specs + [
        pl.BlockSpec((None, N_MOD, D_MODEL), mod_idx), resident(MIX_WIDTH, D_MODEL), whole((1, D_MODEL)),
        resident(D_MODEL, 2 * D_FF), whole((3, 2 * D_FF)), whole((1, 2 * D_FF)),
        resident(D_FF, D_MODEL), whole((1, D_MODEL))]
    args = x_args + mix_args + [mod, w_out, norm2_w, w_up, conv_w, conv_b, w_down, final_w]
    return pl.pallas_call(
        functools.partial(_mix_ffn_kernel, seq=seq, halo=halo, final=final),
        out_shape=jax.ShapeDtypeStruct((n, D_MODEL), F32),
        grid=(n // rows,),
        in_specs=in_specs,
        out_specs=row_blk(D_MODEL),
        compiler_params=_params("parallel"),
        name="mix_ffn_seq%d" % seq,
    )(*args)


def _layer(x, mod, lp, consts, layer, batch, seq, ctx, final):
    latent = ctx is not None
    outs = _in_projection(x, mod, lp["norm1_w"], consts["w_in"], lp["q_norm_w"], lp["k_norm_w"], consts["gm_attn"],
                          consts["dft"][seq], consts["rope"] if latent else None, layer, seq)
    q, k, v, rq, rk, rv, gates, mix = outs[:8]
    mix = _attention(q, k, v, ctx[:2] if latent else None, mix, layer, batch, seq)
    ret_out = _retention(rq, rk, rv, gates, consts["gm_ret"], consts["ret_tables"], ctx[2] if latent else None,
                         mix, layer, batch, seq)
    x = _mix_ffn(x, ret_out[0], mod, consts["w_out"], lp["norm2_w"], consts["w_up"], lp["conv_w"], lp["conv_b"],
                 consts["w_down"], consts["final_w"], layer, seq, latent, final)
    if latent:
        return x, None
    return x, (outs[8], outs[9], ret_out[1])


def kernel(x_prompt, x_sample, c, cache_attn_k, cache_attn_v, state_ret, c_ctx, w_mod, b_mod, norm1_w, w_in,
           q_norm_w, k_norm_w, w_out, norm2_w, w_up, conv_w, conv_b, w_down, final_norm_w):
    batch, seq, d = x_prompt.shape
    dec_batch, dec_seq, _ = x_sample.shape
    past = cache_attn_k.shape[2]
    assert d == D_MODEL and w_in.shape == (DEPTH, D_MODEL, IN_WIDTH) and w_up.shape == (DEPTH, D_MODEL, 2 * D_FF)
    assert (batch * seq) % FFN_ROWS == 0 and dec_seq % FFN_ROWS == 0
    assert seq % min(ATTN_Q_ROWS, seq) == 0 and dec_seq % min(ATTN_Q_ROWS, dec_seq) == 0
    assert INPROJ_ROWS % seq == 0 and dec_seq % INPROJ_ROWS == 0 and seq % RET_CHUNK == 0 and dec_seq % RET_CHUNK == 0
    assert dec_batch + 1 <= MOD_ROWS and dec_seq % GRID_W == 0

    consts = {
        "gm_attn": _group_mean_matrix(ATTN_WIDTH, HEAD_DIM),
        "gm_ret": _group_mean_matrix(2 * RET_WIDTH, RET_DV),
        "rope": _rope_tables(dec_seq),
        "ret_tables": _retention_tables(),
        "dft": {s: _dft_tables(s, INPROJ_ROWS) for s in {seq, dec_seq}},
        "final_w": final_norm_w.reshape(1, D_MODEL),
        "w_in": w_in.astype(BF16),
        "w_out": w_out.astype(BF16),
        "w_up": w_up.astype(BF16),
        "w_down": w_down.astype(BF16),
    }

    cvec = jnp.zeros((MOD_ROWS, D_MODEL), F32).at[0].set(c_ctx).at[1:1 + dec_batch].set(c)
    mod = _modulation(cvec, w_mod, b_mod).reshape(DEPTH, MOD_ROWS, N_MOD, D_MODEL)

    cache_k = cache_attn_k.reshape(dec_batch, DEPTH, past, KV_WIDTH)
    cache_v = cache_attn_v.reshape(dec_batch, DEPTH, past, KV_WIDTH)

    xp = x_prompt.reshape(batch * seq, D_MODEL)
    xs = x_sample.reshape(dec_batch * dec_seq, D_MODEL)
    new_k, new_v, new_s = [], [], []
    for layer in range(DEPTH):
        lp = {
            "norm1_w": norm1_w[layer].reshape(1, D_MODEL),
            "q_norm_w": jnp.tile(q_norm_w[layer], ATTN_HEADS).reshape(1, ATTN_WIDTH),
            "k_norm_w": jnp.tile(k_norm_w[layer], ATTN_KV_HEADS).reshape(1, KV_WIDTH),
            "norm2_w": norm2_w[layer].reshape(1, D_MODEL),
            "conv_w": conv_w[layer],
            "conv_b": conv_b[layer].reshape(1, 2 * D_FF),
        }
        final = layer == DEPTH - 1
        xp, (k_l, v_l, s_l) = _layer(xp, mod[layer, 0:1], lp, consts, layer, batch, seq, None, final)
        new_k.append(k_l.reshape(batch, seq, ATTN_KV_HEADS, HEAD_DIM))
        new_v.append(v_l.reshape(batch, seq, ATTN_KV_HEADS, HEAD_DIM))
        new_s.append(s_l.reshape(batch, 2, RET_HEADS, RET_DK, RET_DV))
        xs, _ = _layer(xs, mod[layer, 1:1 + dec_batch], lp, consts, layer, dec_batch, dec_seq,
                       (cache_k, cache_v, state_ret), final)
    return (xp.reshape(batch, seq, D_MODEL), xs.reshape(dec_batch, dec_seq, D_MODEL),
            jnp.stack(new_k, axis=1), jnp.stack(new_v, axis=1), jnp.stack(new_s, axis=1))
```

```python
import functools

import jax
import jax.numpy as jnp
import numpy as np
from jax import lax
from jax.experimental import pallas as pl
from jax.experimental.pallas import tpu as pltpu

F32 = jnp.float32
BF16 = jnp.bfloat16

D_MODEL = 1024
DEPTH = 2
GRID_W = 64
NORM_EPS = 1e-6
ATTN_HEADS = 8
ATTN_KV_HEADS = 2
HEAD_DIM = 64
ATTN_WIDTH = ATTN_HEADS * HEAD_DIM
KV_WIDTH = ATTN_KV_HEADS * HEAD_DIM
HEADS_PER_KV = ATTN_HEADS // ATTN_KV_HEADS
GROUP_WIDTH = HEADS_PER_KV * HEAD_DIM
ATTN_LOGIT_SCALE = HEAD_DIM ** -0.5 * 1.4426950408889634
ROPE_THETA = 10000.0
ROPE_AXIS_DIM = HEAD_DIM // 2
ROPE_HALF = ROPE_AXIS_DIM // 2
RET_HEADS = 4
RET_DK = 64
RET_DV = 64
RET_WIDTH = RET_HEADS * RET_DV
RET_CHUNK = 128
RET_DECAY_EXP_FWD = 5.0
RET_DECAY_EXP_BWD = 5.5
FOURIER_GROUPS = 4
FOURIER_DIM = 64
FOURIER_WIDTH = FOURIER_GROUPS * FOURIER_DIM
D_FF = 2816
N_MOD = 6

OFF_Q = 0
OFF_K = OFF_Q + ATTN_WIDTH
OFF_V = OFF_K + KV_WIDTH
OFF_RQ = OFF_V + KV_WIDTH
OFF_RK = OFF_RQ + RET_WIDTH
OFF_RV = OFF_RK + RET_WIDTH
OFF_GF = OFF_RV + RET_WIDTH
OFF_GB = OFF_GF + RET_WIDTH
OFF_FX = OFF_GB + RET_WIDTH
IN_WIDTH = OFF_FX + FOURIER_WIDTH
MIX_OFF_ATTN = 0
MIX_OFF_RET = MIX_OFF_ATTN + ATTN_WIDTH
MIX_OFF_FOURIER = MIX_OFF_RET + RET_WIDTH
MIX_WIDTH = MIX_OFF_FOURIER + FOURIER_WIDTH

V7X_VMEM_BYTES = 64 * 1024 * 1024
VMEM_LIMIT = V7X_VMEM_BYTES - 12 * 1024 * 1024

MOD_ROWS = 16
MOD_COLS = 1536
INPROJ_ROWS = 1024
ATTN_Q_ROWS = 512
FFN_ROWS = 512
F32_SUBLANES = 8
BF16_SUBLANES = 16


def _params(*semantics):
    return pltpu.CompilerParams(dimension_semantics=semantics, vmem_limit_bytes=VMEM_LIMIT)


def _dot(a, b):
    return jnp.dot(a, b, preferred_element_type=F32)


def _dot_nt(a, b):
    return lax.dot_general(a, b, (((1,), (1,)), ((), ())), preferred_element_type=F32)


def _sigmoid(x):
    return 1.0 / (1.0 + jnp.exp(-x))


def _silu(x):
    return x * _sigmoid(x)


def _rms_rows(x, w):
    ms = jnp.mean(x * x, axis=-1, keepdims=True)
    return x * lax.rsqrt(ms + NORM_EPS) * w


def _lane_block(shape, width):
    return lax.broadcasted_iota(jnp.int32, shape, len(shape) - 1) // width


def _group_mean_matrix(width, group):
    idx = np.arange(width) // group
    return jnp.asarray((idx[:, None] == idx[None, :]).astype(np.float32) / group, dtype=BF16)


def _rope_tables(n_tokens):
    pos = np.arange(n_tokens)
    row = (pos // GRID_W).astype(np.float64)
    col = (pos % GRID_W).astype(np.float64)
    freqs = ROPE_THETA ** (-np.arange(ROPE_HALF, dtype=np.float64) / ROPE_HALF)
    d = np.arange(HEAD_DIM)
    coord = np.where((d // ROPE_AXIS_DIM)[None, :] == 0, row[:, None], col[:, None])
    ang = coord * freqs[d % ROPE_HALF][None, :]
    sign = np.where((d & ROPE_HALF) == 0, -1.0, 1.0)[None, :]
    cos = np.tile(np.cos(ang), (1, ATTN_HEADS))
    sin = np.tile(np.sin(ang) * sign, (1, ATTN_HEADS))
    return jnp.asarray(cos, F32), jnp.asarray(sin, F32)


def _dft_tables(seq, rows):
    n = np.arange(rows)
    same_seq = (n[:, None] // seq) == (n[None, :] // seq)
    ang = 2.0 * np.pi * (((n % seq)[:, None] * (n % seq)[None, :]) % seq) / seq
    scale = 1.0 / np.sqrt(seq * FOURIER_DIM)
    position = np.concatenate([np.where(same_seq, np.cos(ang), 0.0), np.where(same_seq, -np.sin(ang), 0.0)], axis=1)
    c = np.arange(FOURIER_WIDTH)
    same = (c[:, None] // FOURIER_DIM) == (c[None, :] // FOURIER_DIM)
    angc = 2.0 * np.pi * (((c % FOURIER_DIM)[:, None] * (c % FOURIER_DIM)[None, :]) % FOURIER_DIM) / FOURIER_DIM
    channel = np.concatenate([np.where(same, np.cos(angc), 0.0), np.where(same, np.sin(angc), 0.0)], axis=1)
    as_bf16 = lambda a: jnp.asarray(a, F32).astype(BF16)
    return as_bf16(channel), as_bf16(position * scale)


def _retention_tables():
    heads = jnp.arange(RET_HEADS, dtype=F32)
    idx = jnp.arange(RET_CHUNK, dtype=F32)
    diff = idx[:, None] - idx[None, :]
    out = []
    for exp0, backward in ((RET_DECAY_EXP_FWD, False), (RET_DECAY_EXP_BWD, True)):
        lg = jnp.log1p(-jnp.exp2(-(exp0 + heads)))
        dd = -diff if backward else diff
        inner = jnp.where(dd[None] >= 0, jnp.exp(jnp.maximum(dd, 0.0)[None] * lg[:, None, None]), 0.0)
        inner = inner.transpose(1, 0, 2).reshape(RET_CHUNK, RET_HEADS * RET_CHUNK)
        q_pow = (RET_CHUNK - idx) if backward else (idx + 1.0)
        k_pow = idx if backward else (RET_CHUNK - 1.0 - idx)
        spread = lambda p: jnp.repeat(jnp.exp(p[:, None] * lg[None, :]), RET_DK, axis=1)
        q_decay, k_decay = spread(q_pow), spread(k_pow)
        chunk_decay = jnp.repeat(jnp.exp(RET_CHUNK * lg), RET_DV)[None, :]
        out.append((inner, q_decay, k_decay, chunk_decay))
    (inner_f, qd_f, kd_f, cd_f), (inner_b, qd_b, kd_b, cd_b) = out
    return (inner_f, inner_b, jnp.concatenate([qd_f, qd_b], axis=1), jnp.concatenate([kd_f, kd_b], axis=1), cd_f, cd_b)


def _mod_kernel(c_ref, w_ref, b_ref, o_ref):
    act = _silu(c_ref[...]).astype(BF16)
    o_ref[...] = _dot(act, w_ref[...].astype(BF16)) + b_ref[...]


def _modulation(cvec, w_mod, b_mod):
    n_cols = w_mod.shape[-1]
    return pl.pallas_call(
        _mod_kernel,
        out_shape=jax.ShapeDtypeStruct((DEPTH, MOD_ROWS, n_cols), F32),
        grid=(DEPTH, n_cols // MOD_COLS),
        in_specs=[
            pl.BlockSpec((MOD_ROWS, D_MODEL), lambda l, j: (0, 0)),
            pl.BlockSpec((None, D_MODEL, MOD_COLS), lambda l, j: (l, 0, j)),
            pl.BlockSpec((None, 1, MOD_COLS), lambda l, j: (l, 0, j)),
        ],
        out_specs=pl.BlockSpec((None, MOD_ROWS, MOD_COLS), lambda l, j: (l, 0, j)),
        compiler_params=_params("parallel", "parallel"),
        name="modulation",
    )(cvec, w_mod, b_mod.reshape(DEPTH, 1, n_cols))


def _swap_rotary_pairs(x):
    width = x.shape[-1]
    lane = lax.broadcasted_iota(jnp.int32, x.shape, 1)
    from_below = pltpu.roll(x, ROPE_HALF, 1)
    from_above = pltpu.roll(x, width - ROPE_HALF, 1)
    return jnp.where((lane & ROPE_HALF) != 0, from_below, from_above)


def _inproj_kernel(*refs, latent):
    x_ref, mod_ref, n1_ref, win_ref, qnw_ref, knw_ref, gm_ref, dft_c_ref, dft_p_ref = refs[:9]
    refs = refs[9:]
    if latent:
        cos_ref, sin_ref = refs[:2]
        refs = refs[2:]
        q_ref, k_ref, v_ref, rq_ref, rk_ref, rv_ref, g_ref, four_ref = refs
    else:
        q_ref, k_ref, v_ref, rq_ref, rk_ref, rv_ref, g_ref, four_ref, k32_ref, v32_ref = refs[-10:]

    shift, scale = mod_ref[0:1, :], mod_ref[1:2, :]
    h = _rms_rows(x_ref[...], n1_ref[...]) * (1.0 + scale) + shift
    hb = h.astype(BF16)
    along = _dot(_dot(hb, win_ref[:, OFF_FX:]).astype(BF16), dft_c_ref[...])
    stacked = jnp.concatenate([along[:, 0:FOURIER_WIDTH], along[:, FOURIER_WIDTH:]], axis=0).astype(BF16)
    four_ref[...] = _dot(dft_p_ref[...], stacked).astype(BF16)

    all_proj = _dot(hb, win_ref[:, 0:OFF_FX])
    proj = lambda off, width: all_proj[:, off:off + width]

    q = proj(OFF_Q, ATTN_WIDTH)
    q = q * lax.rsqrt(_dot((q * q).astype(BF16), gm_ref[...]) + NORM_EPS) * qnw_ref[...]
    k = proj(OFF_K, KV_WIDTH)
    k = k * lax.rsqrt(_dot((k * k).astype(BF16), gm_ref[0:KV_WIDTH, 0:KV_WIDTH]) + NORM_EPS) * knw_ref[...]
    v = proj(OFF_V, KV_WIDTH)
    if latent:
        cos, sin = cos_ref[...], sin_ref[...]
        q = q * cos + _swap_rotary_pairs(q) * sin
        k = k * cos[:, 0:KV_WIDTH] + _swap_rotary_pairs(k) * sin[:, 0:KV_WIDTH]
    else:
        k32_ref[...] = k.reshape(k32_ref.shape)
        v32_ref[...] = v.reshape(v32_ref.shape)
    q_ref[...] = (q * ATTN_LOGIT_SCALE).astype(BF16)
    k_ref[...] = k.astype(BF16)
    v_ref[...] = v.astype(BF16)
    rq_ref[...] = proj(OFF_RQ, RET_WIDTH).astype(BF16)
    rk_ref[...] = (proj(OFF_RK, RET_WIDTH) * RET_DK ** -0.5).astype(BF16)
    rv_ref[...] = proj(OFF_RV, RET_WIDTH).astype(BF16)
    g_ref[...] = proj(OFF_GF, 2 * RET_WIDTH)


def _in_projection(x, mod, norm1_w, w_in, q_norm_w, k_norm_w, group_mean, dft, rope, new_cache, layer, seq):
    n = x.shape[0]
    rows = INPROJ_ROWS
    latent = rope is not None
    row_blk = lambda width, col=0: pl.BlockSpec((rows, width), lambda i: (i, col))
    whole = lambda shape: pl.BlockSpec(shape, lambda i: (0,) * len(shape))
    once = lambda shape: pl.BlockSpec(shape, lambda i: (0,) * len(shape), pipeline_mode=pl.Buffered(1))
    mod_idx = (lambda i: (i * rows // seq, 0, 0)) if latent else (lambda i: (0, 0, 0))
    in_specs = [
        row_blk(D_MODEL),
        pl.BlockSpec((None, N_MOD, D_MODEL), mod_idx),
        whole((1, D_MODEL)),
        pl.BlockSpec((None, D_MODEL, IN_WIDTH), lambda i: (layer, 0, 0), pipeline_mode=pl.Buffered(1)),
        whole((1, ATTN_WIDTH)),
        whole((1, KV_WIDTH)),
        whole((ATTN_WIDTH, ATTN_WIDTH)),
        once(dft[0].shape),
        once(dft[1].shape),
    ]
    args = [x, mod, norm1_w, w_in, q_norm_w, k_norm_w, group_mean, dft[0], dft[1]]
    outs = [(ATTN_WIDTH, BF16), (KV_WIDTH, BF16), (KV_WIDTH, BF16), (RET_WIDTH, BF16), (RET_WIDTH, BF16),
            (RET_WIDTH, BF16), (2 * RET_WIDTH, F32)]
    out_shape = [jax.ShapeDtypeStruct((n, w), dt) for w, dt in outs] + [jax.ShapeDtypeStruct((n, MIX_WIDTH), BF16)]
    out_specs = [row_blk(w) for w, _ in outs] + [row_blk(FOURIER_WIDTH, MIX_OFF_FOURIER // FOURIER_WIDTH)]
    aliases = {}
    if latent:
        pos_blk = pl.BlockSpec((rows, ATTN_WIDTH), lambda i: (i % (seq // rows), 0))
        in_specs += [pos_blk, pos_blk]
        args += list(rope)
    else:
        seqs = rows // seq
        out_shape += [jax.ShapeDtypeStruct((n // seq, DEPTH, seq, KV_WIDTH), F32)] * 2
        out_specs += [pl.BlockSpec((seqs, None, seq, KV_WIDTH), lambda i: (i, layer, 0, 0))] * 2
        if new_cache is not None:
            for j, earlier in enumerate(new_cache):
                aliases[len(args)] = len(out_shape) - 2 + j
                in_specs.append(pl.BlockSpec(memory_space=pl.ANY))
                args.append(earlier)
    return pl.pallas_call(
        functools.partial(_inproj_kernel, latent=latent),
        out_shape=out_shape,
        grid=(n // rows,),
        in_specs=in_specs,
        out_specs=out_specs,
        input_output_aliases=aliases,
        compiler_params=_params("parallel"),
        name="in_projection_latent" if latent else "in_projection_context",
    )(*args)


def _spread_kv(x, group):
    lane = lax.broadcasted_iota(jnp.int32, x.shape, 1)
    other = pltpu.roll(x, HEAD_DIM, 1)
    own = (lane // HEAD_DIM) == group
    pair = jnp.where(own, x, other).astype(BF16)
    return jnp.concatenate([pair, pair], axis=1)


def _attention_kernel(*refs, past):
    if past:
        q_ref, k_ref, v_ref, ck_ref, cv_ref, _, o_ref, kt_ref, vt_ref = refs
    else:
        q_ref, k_ref, v_ref, _, o_ref, kt_ref, vt_ref = refs

    @pl.when(pl.program_id(1) == 0)
    def _():
        for g in range(ATTN_KV_HEADS):
            if past:
                kt_ref[g, 0:past, :] = _spread_kv(ck_ref[...], g)
                vt_ref[g, 0:past, :] = _spread_kv(cv_ref[...], g)
            kt_ref[g, past:, :] = _spread_kv(k_ref[...].astype(F32), g)
            vt_ref[g, past:, :] = _spread_kv(v_ref[...].astype(F32), g)

    rows = q_ref.shape[0]
    block = _lane_block((rows, GROUP_WIDTH), HEAD_DIM)

    def logits(head):
        g, h = divmod(head, HEADS_PER_KV)
        qg = q_ref[:, g * GROUP_WIDTH:(g + 1) * GROUP_WIDTH]
        return _dot_nt(jnp.where(block == h, qg, jnp.zeros_like(qg)), kt_ref[g])

    s_next = logits(0)
    out = None
    for head in range(ATTN_HEADS):
        g, h = divmod(head, HEADS_PER_KV)
        s = s_next
        if head + 1 < ATTN_HEADS:
            s_next = logits(head + 1)
        p = jnp.exp2(s - jnp.max(s, axis=-1, keepdims=True))
        denom = jnp.sum(p, axis=-1, keepdims=True)
        o = _dot(p.astype(BF16), vt_ref[g])
        out = o / denom if h == 0 else jnp.where(block == h, o / denom, out)
        if h == HEADS_PER_KV - 1:
            o_ref[:, g * GROUP_WIDTH:(g + 1) * GROUP_WIDTH] = out.astype(BF16)


def _attention(q, k, v, cache, mix, layer, batch, seq):
    tq = min(ATTN_Q_ROWS, seq)
    nq = seq // tq
    past = 0 if cache is None else cache[0].shape[2]
    own_kv = pl.BlockSpec((seq, KV_WIDTH), lambda b, i: (b, 0))
    in_specs = [pl.BlockSpec((tq, ATTN_WIDTH), lambda b, i: (b * nq + i, 0)), own_kv, own_kv]
    args = [q, k, v]
    if past:
        cached_kv = pl.BlockSpec((None, None, past, KV_WIDTH), lambda b, i: (b, layer, 0, 0))
        in_specs += [cached_kv, cached_kv]
        args += list(cache)
    in_specs.append(pl.BlockSpec(memory_space=pl.ANY))
    args.append(mix)
    return pl.pallas_call(
        functools.partial(_attention_kernel, past=past),
        out_shape=jax.ShapeDtypeStruct(mix.shape, mix.dtype),
        grid=(batch, nq),
        in_specs=in_specs,
        out_specs=pl.BlockSpec((tq, ATTN_WIDTH), lambda b, i: (b * nq + i, MIX_OFF_ATTN // ATTN_WIDTH)),
        input_output_aliases={len(args) - 1: 0},
        scratch_shapes=[pltpu.VMEM((ATTN_KV_HEADS, past + seq, GROUP_WIDTH), BF16)] * 2,
        compiler_params=_params("parallel", "arbitrary"),
        name="attention_latent" if past else "attention_context",
    )(*args)


def _stack_heads(x):
    block = _lane_block(x.shape, RET_DK)
    zero = jnp.zeros_like(x)
    return jnp.concatenate([jnp.where(block == h, x, zero) for h in range(RET_HEADS)], axis=0)


def _head_norm(o, gm):
    mu = _dot(o.astype(BF16), gm)
    d = o - mu
    var = _dot((d * d).astype(BF16), gm)
    return d * lax.rsqrt(var + NORM_EPS)


def _retention_kernel(*refs, has_state, n_chunks):
    rq_ref, rk_ref, rv_ref, g_ref, gm_ref, df_ref, db_ref, qd_ref, kd_ref, cdf_ref, cdb_ref = refs[:11]
    refs = refs[12:]
    if has_state:
        s0_ref, o_ref, both_ref, upd_ref, st_ref = refs
    else:
        o_ref, sfin_ref, both_ref, upd_ref, st_ref = refs[-5:]

    width = RET_WIDTH
    diag = (lax.broadcasted_iota(jnp.int32, (width, width), 0) // RET_DK
            == lax.broadcasted_iota(jnp.int32, (width, width), 1) // RET_DV)
    chunk_rows = lambda c: pl.ds(c * RET_CHUNK, RET_CHUNK)

    for c in range(n_chunks):
        rows = chunk_rows(c)
        qc, kc, vc = rq_ref[rows, :], rk_ref[rows, :], rv_ref[rows, :]
        scores = _dot_nt(qc, _stack_heads(kc))
        v_heads = _stack_heads(vc)
        both_ref[rows, 0:width] = _dot((scores * df_ref[...]).astype(BF16), v_heads)
        both_ref[rows, width:] = _dot((scores * db_ref[...]).astype(BF16), v_heads)
        k32 = kc.astype(F32)
        k_decayed = jnp.concatenate([k32, k32], axis=1) * kd_ref[...]
        upd_ref[c] = _dot(k_decayed.T.astype(BF16), vc)

    for direction, order, cd_ref in ((0, range(n_chunks), cdf_ref), (1, reversed(range(n_chunks)), cdb_ref)):
        if has_state:
            s = s0_ref[direction].reshape(width, RET_DV)
            state = jnp.where(diag, jnp.concatenate([s] * RET_HEADS, axis=1), 0.0)
        else:
            state = jnp.zeros((width, width), F32)
        for c in order:
            st_ref[direction, c] = state.astype(BF16)
            update = upd_ref[c, direction * width:(direction + 1) * width, :]
            state = cd_ref[...] * state + jnp.where(diag, update, 0.0)
        if not has_state:
            folded = state[:, 0:width // 2] + state[:, width // 2:]
            sfin_ref[direction] = folded[:, 0:RET_DV] + folded[:, RET_DV:]

    for c in range(n_chunks):
        rows = chunk_rows(c)
        q32 = rq_ref[rows, :].astype(F32)
        q_decayed = (jnp.concatenate([q32, q32], axis=1) * qd_ref[...]).astype(BF16)
        both_ref[rows, 0:width] += _dot(q_decayed[:, 0:width], st_ref[0, c])
        both_ref[rows, width:] += _dot(q_decayed[:, width:], st_ref[1, c])

    gated = _silu(g_ref[...]) * _head_norm(both_ref[...], gm_ref[...])
    o_ref[...] = (gated[:, 0:width] + gated[:, width:]).astype(BF16)


def _retention(rq, rk, rv, gates, group_mean, tables, state, new_state, mix, layer, batch, seq):
    has_state = state is not None
    n_chunks = seq // RET_CHUNK
    seq_blk = lambda width: pl.BlockSpec((seq, width), lambda b: (b, 0))
    whole = lambda a: pl.BlockSpec(a.shape, lambda b: (0,) * a.ndim)
    in_specs = ([seq_blk(RET_WIDTH)] * 3 + [seq_blk(2 * RET_WIDTH), whole(group_mean)] + [whole(t) for t in tables]
                + [pl.BlockSpec(memory_space=pl.ANY)])
    args = [rq, rk, rv, gates, group_mean] + list(tables) + [mix]
    aliases = {len(args) - 1: 0}
    out_shape = [jax.ShapeDtypeStruct(mix.shape, mix.dtype)]
    out_specs = [pl.BlockSpec((seq, RET_WIDTH), lambda b: (b, MIX_OFF_RET // RET_WIDTH))]
    if has_state:
        in_specs.append(pl.BlockSpec((None, None, 2, RET_HEADS, RET_DK, RET_DV), lambda b: (b, layer, 0, 0, 0, 0)))
        args.append(state)
    else:
        out_shape.append(jax.ShapeDtypeStruct((batch, DEPTH, 2, RET_HEADS * RET_DK, RET_DV), F32))
        out_specs.append(pl.BlockSpec((None, None, 2, RET_HEADS * RET_DK, RET_DV), lambda b: (b, layer, 0, 0, 0)))
        if new_state is not None:
            aliases[len(args)] = 1
            in_specs.append(pl.BlockSpec(memory_space=pl.ANY))
            args.append(new_state)
    return pl.pallas_call(
        functools.partial(_retention_kernel, has_state=has_state, n_chunks=n_chunks),
        out_shape=out_shape,
        grid=(batch,),
        in_specs=in_specs,
        out_specs=out_specs,
        input_output_aliases=aliases,
        scratch_shapes=[pltpu.VMEM((seq, 2 * RET_WIDTH), F32),
                        pltpu.VMEM((n_chunks, 2 * RET_WIDTH, RET_WIDTH), F32),
                        pltpu.VMEM((2, n_chunks, RET_WIDTH, RET_WIDTH), BF16)],
        compiler_params=_params("parallel"),
        name="retention_latent" if has_state else "retention_context",
    )(*args)


def _shift_rows(u, seq):
    rows = u.shape[0]
    pos = lax.broadcasted_iota(jnp.int32, u.shape, 0) % seq
    prev = jnp.where(pos == 0, 0.0, pltpu.roll(u, 1, 0))
    nxt = jnp.where(pos == seq - 1, 0.0, pltpu.roll(u, rows - 1, 0))
    return prev, nxt


def _mix_ffn_kernel(*refs, seq, halo, final):
    if halo:
        (x_ref, x_top_ref, x_bot_ref, mix_ref, mix_top_ref, mix_bot_ref,
         mod_ref, wout_ref, n2_ref, wup_ref, cw_ref, cb_ref, wd_ref, fin_ref, o_ref) = refs
    else:
        x_ref, mix_ref, mod_ref, wout_ref, n2_ref, wup_ref, cw_ref, cb_ref, wd_ref, fin_ref, o_ref = refs
    rows = x_ref.shape[0]
    modulate = lambda x: _rms_rows(x, n2_ref[...]) * (1.0 + mod_ref[4:5, :]) + mod_ref[3:4, :]
    if halo:
        pad = x_top_ref.shape[0]
        skip = mix_top_ref.shape[0] - pad
        tiles_per_seq = seq // rows
        place = pl.program_id(0) % tiles_per_seq
        mixed = _dot(jnp.concatenate([mix_top_ref[...], mix_ref[...], mix_bot_ref[...]], axis=0), wout_ref[...])
        x_ext = jnp.concatenate([x_top_ref[...], x_ref[...], x_bot_ref[...]], axis=0)
        x1_ext = x_ext + mod_ref[2:3, :] * mixed[skip:skip + pad + rows + pad, :]
        x1 = x1_ext[pad:pad + rows, :]
        h2 = modulate(x1_ext)
        row = lax.broadcasted_iota(jnp.int32, h2.shape, 0)
        first_kept = jnp.where(place == 0, pad, 0)
        end_kept = jnp.where(place == tiles_per_seq - 1, pad + rows, pad + rows + pad)
        h2 = jnp.where((row >= first_kept) & (row < end_kept), h2, 0.0)
        u = _dot(h2.astype(BF16), wup_ref[...])
        prev, nxt = pltpu.roll(u, 1, 0), pltpu.roll(u, u.shape[0] - 1, 0)
    else:
        x1 = x_ref[...] + mod_ref[2:3, :] * _dot(mix_ref[...], wout_ref[...])
        u = _dot(modulate(x1).astype(BF16), wup_ref[...])
        prev, nxt = _shift_rows(u, seq)
    u = prev * cw_ref[0:1, :] + u * cw_ref[1:2, :] + nxt * cw_ref[2:3, :] + cb_ref[...]
    if halo:
        u = u[pad:pad + rows, :]
    act = _silu(u[:, 0:D_FF]) * u[:, D_FF:]
    y = x1 + mod_ref[5:6, :] * _dot(act.astype(BF16), wd_ref[...])
    o_ref[...] = _rms_rows(y, fin_ref[...]) if final else y


def _mix_ffn(x, mix, mod, w_out, norm2_w, w_up, conv_w, conv_b, w_down, final_w, layer, seq, per_seq_mod, final):
    n = x.shape[0]
    rows = FFN_ROWS
    halo = seq > rows
    assert (seq % rows == 0) if halo else (rows % seq == 0)
    row_blk = lambda width: pl.BlockSpec((rows, width), lambda i: (i, 0))
    whole = lambda shape: pl.BlockSpec(shape, lambda i: (0,) * len(shape))
    resident = lambda r, c: pl.BlockSpec((None, r, c), lambda i: (layer, 0, 0), pipeline_mode=pl.Buffered(1))
    mod_idx = (lambda i: (i * rows // seq, 0, 0)) if per_seq_mod else (lambda i: (0, 0, 0))

    def with_halo(a, pad):
        width = a.shape[1]
        per_tile, last = rows // pad, n // pad - 1
        return ([row_blk(width),
                 pl.BlockSpec((pad, width), lambda i: (jnp.maximum(i * per_tile - 1, 0), 0)),
                 pl.BlockSpec((pad, width), lambda i: (jnp.minimum((i + 1) * per_tile, last), 0))], [a, a, a])

    if halo:
        x_specs, x_args = with_halo(x, F32_SUBLANES)
        mix_specs, mix_args = with_halo(mix, BF16_SUBLANES)
    else:
        x_specs, x_args, mix_specs, mix_args = [row_blk(D_MODEL)], [x], [row_blk(MIX_WIDTH)], [mix]
    in_specs = x_specs + mix_specs + [
        pl.BlockSpec((None, N_MOD, D_MODEL), mod_idx), resident(MIX_WIDTH, D_MODEL), whole((1, D_MODEL)),
        resident(D_MODEL, 2 * D_FF), whole((3, 2 * D_FF)), whole((1, 2 * D_FF)),
        resident(D_FF, D_MODEL), whole((1, D_MODEL))]
    args = x_args + mix_args + [mod, w_out, norm2_w, w_up, conv_w, conv_b, w_down, final_w]
    return pl.pallas_call(
        functools.partial(_mix_ffn_kernel, seq=seq, halo=halo, final=final),
        out_shape=jax.ShapeDtypeStruct((n, D_MODEL), F32),
        grid=(n // rows,),
        in_specs=in_specs,
        out_specs=row_blk(D_MODEL),
        compiler_params=_params("parallel"),
        name="mix_ffn_seq%d" % seq,
    )(*args)


def _layer(x, mod, lp, consts, layer, batch, seq, ctx, new_ctx, final):
    latent = ctx is not None
    outs = _in_projection(x, mod, lp["norm1_w"], consts["w_in"], lp["q_norm_w"], lp["k_norm_w"], consts["gm_attn"],
                          consts["dft"][seq], consts["rope"] if latent else None,
                          None if new_ctx is None else new_ctx[:2], layer, seq)
    q, k, v, rq, rk, rv, gates, mix = outs[:8]
    mix = _attention(q, k, v, ctx[:2] if latent else None, mix, layer, batch, seq)
    ret_out = _retention(rq, rk, rv, gates, consts["gm_ret"], consts["ret_tables"], ctx[2] if latent else None,
                         None if new_ctx is None else new_ctx[2], mix, layer, batch, seq)
    x = _mix_ffn(x, ret_out[0], mod, consts["w_out"], lp["norm2_w"], consts["w_up"], lp["conv_w"], lp["conv_b"],
                 consts["w_down"], consts["final_w"], layer, seq, latent, final)
    if latent:
        return x, None
    return x, (outs[8], outs[9], ret_out[1])


def kernel(x_prompt, x_sample, c, cache_attn_k, cache_attn_v, state_ret, c_ctx, w_mod, b_mod, norm1_w, w_in,
           q_norm_w, k_norm_w, w_out, norm2_w, w_up, conv_w, conv_b, w_down, final_norm_w):
    batch, seq, d = x_prompt.shape
    dec_batch, dec_seq, _ = x_sample.shape
    past = cache_attn_k.shape[2]
    assert d == D_MODEL and w_in.shape == (DEPTH, D_MODEL, IN_WIDTH) and w_up.shape == (DEPTH, D_MODEL, 2 * D_FF)
    assert (batch * seq) % FFN_ROWS == 0 and dec_seq % FFN_ROWS == 0
    assert seq % min(ATTN_Q_ROWS, seq) == 0 and dec_seq % min(ATTN_Q_ROWS, dec_seq) == 0
    assert INPROJ_ROWS % seq == 0 and dec_seq % INPROJ_ROWS == 0 and seq % RET_CHUNK == 0 and dec_seq % RET_CHUNK == 0
    assert dec_batch + 1 <= MOD_ROWS and dec_seq % GRID_W == 0

    consts = {
        "gm_attn": _group_mean_matrix(ATTN_WIDTH, HEAD_DIM),
        "gm_ret": _group_mean_matrix(2 * RET_WIDTH, RET_DV),
        "rope": _rope_tables(dec_seq),
        "ret_tables": _retention_tables(),
        "dft": {s: _dft_tables(s, INPROJ_ROWS) for s in {seq, dec_seq}},
        "final_w": final_norm_w.reshape(1, D_MODEL),
        "w_in": w_in.astype(BF16),
        "w_out": w_out.astype(BF16),
        "w_up": w_up.astype(BF16),
        "w_down": w_down.astype(BF16),
    }

    cvec = jnp.zeros((MOD_ROWS, D_MODEL), F32).at[0].set(c_ctx).at[1:1 + dec_batch].set(c)
    mod = _modulation(cvec, w_mod, b_mod).reshape(DEPTH, MOD_ROWS, N_MOD, D_MODEL)

    cache_k = cache_attn_k.reshape(dec_batch, DEPTH, past, KV_WIDTH)
    cache_v = cache_attn_v.reshape(dec_batch, DEPTH, past, KV_WIDTH)

    xp = x_prompt.reshape(batch * seq, D_MODEL)
    xs = x_sample.reshape(dec_batch * dec_seq, D_MODEL)
    new_ctx = None
    for layer in range(DEPTH):
        lp = {
            "norm1_w": norm1_w[layer].reshape(1, D_MODEL),
            "q_norm_w": jnp.tile(q_norm_w[layer], ATTN_HEADS).reshape(1, ATTN_WIDTH),
            "k_norm_w": jnp.tile(k_norm_w[layer], ATTN_KV_HEADS).reshape(1, KV_WIDTH),
            "norm2_w": norm2_w[layer].reshape(1, D_MODEL),
            "conv_w": conv_w[layer],
            "conv_b": conv_b[layer].reshape(1, 2 * D_FF),
        }
        final = layer == DEPTH - 1
        xp, new_ctx = _layer(xp, mod[layer, 0:1], lp, consts, layer, batch, seq, None, new_ctx, final)
        xs, _ = _layer(xs, mod[layer, 1:1 + dec_batch], lp, consts, layer, dec_batch, dec_seq,
                       (cache_k, cache_v, state_ret), None, final)
    new_k, new_v, new_s = new_ctx
    return (xp.reshape(batch, seq, D_MODEL), xs.reshape(dec_batch, dec_seq, D_MODEL),
            new_k.reshape(batch, DEPTH, seq, ATTN_KV_HEADS, HEAD_DIM),
            new_v.reshape(batch, DEPTH, seq, ATTN_KV_HEADS, HEAD_DIM),
            new_s.reshape(batch, DEPTH, 2, RET_HEADS, RET_DK, RET_DV))
```

```python
import functools

import jax
import jax.numpy as jnp
import numpy as np
from jax import lax
from jax.experimental import pallas as pl
from jax.experimental.pallas import tpu as pltpu

F32 = jnp.float32
BF16 = jnp.bfloat16

D_MODEL = 1024
DEPTH = 2
GRID_W = 64
NORM_EPS = 1e-6
ATTN_HEADS = 8
ATTN_KV_HEADS = 2
HEAD_DIM = 64
ATTN_WIDTH = ATTN_HEADS * HEAD_DIM
KV_WIDTH = ATTN_KV_HEADS * HEAD_DIM
HEADS_PER_KV = ATTN_HEADS // ATTN_KV_HEADS
GROUP_WIDTH = HEADS_PER_KV * HEAD_DIM
ATTN_LOGIT_SCALE = HEAD_DIM ** -0.5 * 1.4426950408889634
ROPE_THETA = 10000.0
ROPE_AXIS_DIM = HEAD_DIM // 2
ROPE_HALF = ROPE_AXIS_DIM // 2
RET_HEADS = 4
RET_DK = 64
RET_DV = 64
RET_WIDTH = RET_HEADS * RET_DV
RET_CHUNK = 128
RET_DECAY_EXP_FWD = 5.0
RET_DECAY_EXP_BWD = 5.5
FOURIER_GROUPS = 4
FOURIER_DIM = 64
FOURIER_WIDTH = FOURIER_GROUPS * FOURIER_DIM
D_FF = 2816
N_MOD = 6

OFF_Q = 0
OFF_K = OFF_Q + ATTN_WIDTH
OFF_V = OFF_K + KV_WIDTH
OFF_RQ = OFF_V + KV_WIDTH
OFF_RK = OFF_RQ + RET_WIDTH
OFF_RV = OFF_RK + RET_WIDTH
OFF_GF = OFF_RV + RET_WIDTH
OFF_GB = OFF_GF + RET_WIDTH
OFF_FX = OFF_GB + RET_WIDTH
IN_WIDTH = OFF_FX + FOURIER_WIDTH
MIX_OFF_ATTN = 0
MIX_OFF_RET = MIX_OFF_ATTN + ATTN_WIDTH
MIX_OFF_FOURIER = MIX_OFF_RET + RET_WIDTH
MIX_WIDTH = MIX_OFF_FOURIER + FOURIER_WIDTH

V7X_VMEM_BYTES = 64 * 1024 * 1024
VMEM_LIMIT = V7X_VMEM_BYTES - 12 * 1024 * 1024

MOD_ROWS = 16
MOD_COLS = 1536
INPROJ_ROWS = 1024
ATTN_Q_ROWS = 512
FFN_ROWS = 512
F32_SUBLANES = 8
BF16_SUBLANES = 16


def _params(*semantics):
    return pltpu.CompilerParams(dimension_semantics=semantics, vmem_limit_bytes=VMEM_LIMIT)


def _dot(a, b):
    return jnp.dot(a, b, preferred_element_type=F32)


def _group_mean(x, gm):
    xb = x.astype(BF16)
    width, slab = x.shape[1], gm.shape[0]
    if width <= slab:
        return _dot(xb, gm[0:width, 0:width])
    return jnp.concatenate([_dot(xb[:, s:s + slab], gm) for s in range(0, width, slab)], axis=1)


def _dot_nt(a, b):
    return lax.dot_general(a, b, (((1,), (1,)), ((), ())), preferred_element_type=F32)


def _sigmoid(x):
    return 1.0 / (1.0 + jnp.exp(-x))


def _silu(x):
    return x * _sigmoid(x)


def _rms_rows(x, w):
    ms = jnp.mean(x * x, axis=-1, keepdims=True)
    return x * lax.rsqrt(ms + NORM_EPS) * w


def _lane_block(shape, width):
    return lax.broadcasted_iota(jnp.int32, shape, len(shape) - 1) // width


def _group_mean_matrix(width, group):
    idx = np.arange(width) // group
    return jnp.asarray((idx[:, None] == idx[None, :]).astype(np.float32) / group, dtype=BF16)


def _rope_tables(n_tokens):
    pos = np.arange(n_tokens)
    row = (pos // GRID_W).astype(np.float64)
    col = (pos % GRID_W).astype(np.float64)
    freqs = ROPE_THETA ** (-np.arange(ROPE_HALF, dtype=np.float64) / ROPE_HALF)
    d = np.arange(HEAD_DIM)
    coord = np.where((d // ROPE_AXIS_DIM)[None, :] == 0, row[:, None], col[:, None])
    ang = coord * freqs[d % ROPE_HALF][None, :]
    sign = np.where((d & ROPE_HALF) == 0, -1.0, 1.0)[None, :]
    cos = np.tile(np.cos(ang), (1, ATTN_HEADS))
    sin = np.tile(np.sin(ang) * sign, (1, ATTN_HEADS))
    return jnp.asarray(cos, F32), jnp.asarray(sin, F32)


def _dft_tables(seq):
    n = np.arange(seq)
    ang = 2.0 * np.pi * ((n[:, None] * n[None, :]) % seq) / seq
    scale = 1.0 / np.sqrt(seq * FOURIER_DIM)
    position = np.concatenate([np.cos(ang), -np.sin(ang)], axis=1)
    c = np.arange(FOURIER_WIDTH)
    same = (c[:, None] // FOURIER_DIM) == (c[None, :] // FOURIER_DIM)
    angc = 2.0 * np.pi * (((c % FOURIER_DIM)[:, None] * (c % FOURIER_DIM)[None, :]) % FOURIER_DIM) / FOURIER_DIM
    channel = np.concatenate([np.where(same, np.cos(angc), 0.0), np.where(same, np.sin(angc), 0.0)], axis=1)
    as_bf16 = lambda a: jnp.asarray(a, F32).astype(BF16)
    return as_bf16(channel), as_bf16(position * scale)


def _retention_tables():
    heads = jnp.arange(RET_HEADS, dtype=F32)
    idx = jnp.arange(RET_CHUNK, dtype=F32)
    diff = idx[:, None] - idx[None, :]
    out = []
    for exp0, backward in ((RET_DECAY_EXP_FWD, False), (RET_DECAY_EXP_BWD, True)):
        lg = jnp.log1p(-jnp.exp2(-(exp0 + heads)))
        dd = -diff if backward else diff
        inner = jnp.where(dd[None] >= 0, jnp.exp(jnp.maximum(dd, 0.0)[None] * lg[:, None, None]), 0.0)
        inner = inner.transpose(1, 0, 2).reshape(RET_CHUNK, RET_HEADS * RET_CHUNK)
        q_pow = (RET_CHUNK - idx) if backward else (idx + 1.0)
        k_pow = idx if backward else (RET_CHUNK - 1.0 - idx)
        spread = lambda p: jnp.repeat(jnp.exp(p[:, None] * lg[None, :]), RET_DK, axis=1)
        q_decay, k_decay = spread(q_pow), spread(k_pow)
        chunk_decay = jnp.repeat(jnp.exp(RET_CHUNK * lg), RET_DV)[None, :]
        out.append((inner, q_decay, k_decay, chunk_decay))
    (inner_f, qd_f, kd_f, cd_f), (inner_b, qd_b, kd_b, cd_b) = out
    return (inner_f, inner_b, jnp.concatenate([qd_f, qd_b], axis=1), jnp.concatenate([kd_f, kd_b], axis=1), cd_f, cd_b)


def _mod_kernel(c_ref, w_ref, b_ref, o_ref):
    act = _silu(c_ref[...]).astype(BF16)
    o_ref[...] = _dot(act, w_ref[...].astype(BF16)) + b_ref[...]


def _modulation(cvec, w_mod, b_mod):
    n_cols = w_mod.shape[-1]
    return pl.pallas_call(
        _mod_kernel,
        out_shape=jax.ShapeDtypeStruct((DEPTH, MOD_ROWS, n_cols), F32),
        grid=(DEPTH, n_cols // MOD_COLS),
        in_specs=[
            pl.BlockSpec((MOD_ROWS, D_MODEL), lambda l, j: (0, 0)),
            pl.BlockSpec((None, D_MODEL, MOD_COLS), lambda l, j: (l, 0, j)),
            pl.BlockSpec((None, 1, MOD_COLS), lambda l, j: (l, 0, j)),
        ],
        out_specs=pl.BlockSpec((None, MOD_ROWS, MOD_COLS), lambda l, j: (l, 0, j)),
        compiler_params=_params("parallel", "parallel"),
        name="modulation",
    )(cvec, w_mod, b_mod.reshape(DEPTH, 1, n_cols))


def _swap_rotary_pairs(x):
    width = x.shape[-1]
    lane = lax.broadcasted_iota(jnp.int32, x.shape, 1)
    from_below = pltpu.roll(x, ROPE_HALF, 1)
    from_above = pltpu.roll(x, width - ROPE_HALF, 1)
    return jnp.where((lane & ROPE_HALF) != 0, from_below, from_above)


def _inproj_kernel(*refs, latent):
    x_ref, mod_ref, n1_ref, win_ref, qnw_ref, knw_ref, gm_ref, dft_c_ref, dft_p_ref = refs[:9]
    refs = refs[9:]
    if latent:
        cos_ref, sin_ref = refs[:2]
        refs = refs[2:]
        q_ref, k_ref, v_ref, rq_ref, rk_ref, rv_ref, g_ref, four_ref = refs
    else:
        q_ref, k_ref, v_ref, rq_ref, rk_ref, rv_ref, g_ref, four_ref, k32_ref, v32_ref = refs[-10:]

    shift, scale = mod_ref[0:1, :], mod_ref[1:2, :]
    h = _rms_rows(x_ref[...], n1_ref[...]) * (1.0 + scale) + shift
    hb = h.astype(BF16)
    along = _dot(_dot(hb, win_ref[:, OFF_FX:]).astype(BF16), dft_c_ref[...])
    seq = dft_p_ref.shape[0]
    for start in range(0, x_ref.shape[0], seq):
        own = along[start:start + seq, :]
        stacked = jnp.concatenate([own[:, 0:FOURIER_WIDTH], own[:, FOURIER_WIDTH:]], axis=0).astype(BF16)
        four_ref[start:start + seq, :] = _dot(dft_p_ref[...], stacked).astype(BF16)

    all_proj = _dot(hb, win_ref[:, 0:OFF_FX])
    proj = lambda off, width: all_proj[:, off:off + width]

    q = proj(OFF_Q, ATTN_WIDTH)
    q = q * lax.rsqrt(_group_mean(q * q, gm_ref[...]) + NORM_EPS) * qnw_ref[...]
    k = proj(OFF_K, KV_WIDTH)
    k = k * lax.rsqrt(_group_mean(k * k, gm_ref[...]) + NORM_EPS) * knw_ref[...]
    v = proj(OFF_V, KV_WIDTH)
    if latent:
        cos, sin = cos_ref[...], sin_ref[...]
        q = q * cos + _swap_rotary_pairs(q) * sin
        k = k * cos[:, 0:KV_WIDTH] + _swap_rotary_pairs(k) * sin[:, 0:KV_WIDTH]
    else:
        k32_ref[...] = k.reshape(k32_ref.shape)
        v32_ref[...] = v.reshape(v32_ref.shape)
    q_ref[...] = (q * ATTN_LOGIT_SCALE).astype(BF16)
    k_ref[...] = k.astype(BF16)
    v_ref[...] = v.astype(BF16)
    rq_ref[...] = proj(OFF_RQ, RET_WIDTH).astype(BF16)
    rk_ref[...] = (proj(OFF_RK, RET_WIDTH) * RET_DK ** -0.5).astype(BF16)
    rv_ref[...] = proj(OFF_RV, RET_WIDTH).astype(BF16)
    g_ref[...] = proj(OFF_GF, 2 * RET_WIDTH)


def _in_projection(x, mod, norm1_w, w_in, q_norm_w, k_norm_w, group_mean, dft, rope, new_cache, layer, seq):
    n = x.shape[0]
    rows = INPROJ_ROWS
    latent = rope is not None
    row_blk = lambda width, col=0: pl.BlockSpec((rows, width), lambda i: (i, col))
    whole = lambda shape: pl.BlockSpec(shape, lambda i: (0,) * len(shape))
    once = lambda shape: pl.BlockSpec(shape, lambda i: (0,) * len(shape), pipeline_mode=pl.Buffered(1))
    mod_idx = (lambda i: (i * rows // seq, 0, 0)) if latent else (lambda i: (0, 0, 0))
    in_specs = [
        row_blk(D_MODEL),
        pl.BlockSpec((None, N_MOD, D_MODEL), mod_idx),
        whole((1, D_MODEL)),
        pl.BlockSpec((None, D_MODEL, IN_WIDTH), lambda i: (layer, 0, 0), pipeline_mode=pl.Buffered(1)),
        whole((1, ATTN_WIDTH)),
        whole((1, KV_WIDTH)),
        whole(group_mean.shape),
        once(dft[0].shape),
        once(dft[1].shape),
    ]
    args = [x, mod, norm1_w, w_in, q_norm_w, k_norm_w, group_mean, dft[0], dft[1]]
    outs = [(ATTN_WIDTH, BF16), (KV_WIDTH, BF16), (KV_WIDTH, BF16), (RET_WIDTH, BF16), (RET_WIDTH, BF16),
            (RET_WIDTH, BF16), (2 * RET_WIDTH, F32)]
    out_shape = [jax.ShapeDtypeStruct((n, w), dt) for w, dt in outs] + [jax.ShapeDtypeStruct((n, MIX_WIDTH), BF16)]
    out_specs = [row_blk(w) for w, _ in outs] + [row_blk(FOURIER_WIDTH, MIX_OFF_FOURIER // FOURIER_WIDTH)]
    aliases = {}
    if latent:
        pos_blk = pl.BlockSpec((rows, ATTN_WIDTH), lambda i: (i % (seq // rows), 0))
        in_specs += [pos_blk, pos_blk]
        args += list(rope)
    else:
        seqs = rows // seq
        out_shape += [jax.ShapeDtypeStruct((n // seq, DEPTH, seq, KV_WIDTH), F32)] * 2
        out_specs += [pl.BlockSpec((seqs, None, seq, KV_WIDTH), lambda i: (i, layer, 0, 0))] * 2
        if new_cache is not None:
            for j, earlier in enumerate(new_cache):
                aliases[len(args)] = len(out_shape) - 2 + j
                in_specs.append(pl.BlockSpec(memory_space=pl.ANY))
                args.append(earlier)
    return pl.pallas_call(
        functools.partial(_inproj_kernel, latent=latent),
        out_shape=out_shape,
        grid=(n // rows,),
        in_specs=in_specs,
        out_specs=out_specs,
        input_output_aliases=aliases,
        compiler_params=_params("parallel"),
        name="in_projection_latent" if latent else "in_projection_context",
    )(*args)


def _spread_kv(x, group):
    lane = lax.broadcasted_iota(jnp.int32, x.shape, 1)
    other = pltpu.roll(x, HEAD_DIM, 1)
    own = (lane // HEAD_DIM) == group
    pair = jnp.where(own, x, other).astype(BF16)
    return jnp.concatenate([pair, pair], axis=1)


def _attention_kernel(*refs, past):
    if past:
        q_ref, k_ref, v_ref, ck_ref, cv_ref, _, o_ref, kt_ref, vt_ref = refs
    else:
        q_ref, k_ref, v_ref, _, o_ref, kt_ref, vt_ref = refs

    @pl.when(pl.program_id(1) == 0)
    def _():
        for g in range(ATTN_KV_HEADS):
            if past:
                kt_ref[g, 0:past, :] = _spread_kv(ck_ref[...], g)
                vt_ref[g, 0:past, :] = _spread_kv(cv_ref[...], g)
            kt_ref[g, past:, :] = _spread_kv(k_ref[...].astype(F32), g)
            vt_ref[g, past:, :] = _spread_kv(v_ref[...].astype(F32), g)

    rows = q_ref.shape[0]
    block = _lane_block((rows, GROUP_WIDTH), HEAD_DIM)

    def logits(head):
        g, h = divmod(head, HEADS_PER_KV)
        qg = q_ref[:, g * GROUP_WIDTH:(g + 1) * GROUP_WIDTH]
        return _dot_nt(jnp.where(block == h, qg, jnp.zeros_like(qg)), kt_ref[g])

    s_next = logits(0)
    out = None
    for head in range(ATTN_HEADS):
        g, h = divmod(head, HEADS_PER_KV)
        s = s_next
        if head + 1 < ATTN_HEADS:
            s_next = logits(head + 1)
        p = jnp.exp2(s - jnp.max(s, axis=-1, keepdims=True))
        denom = jnp.sum(p, axis=-1, keepdims=True)
        o = _dot(p.astype(BF16), vt_ref[g])
        out = o / denom if h == 0 else jnp.where(block == h, o / denom, out)
        if h == HEADS_PER_KV - 1:
            o_ref[:, g * GROUP_WIDTH:(g + 1) * GROUP_WIDTH] = out.astype(BF16)


def _attention(q, k, v, cache, mix, layer, batch, seq):
    tq = min(ATTN_Q_ROWS, seq)
    nq = seq // tq
    past = 0 if cache is None else cache[0].shape[2]
    own_kv = pl.BlockSpec((seq, KV_WIDTH), lambda b, i: (b, 0))
    in_specs = [pl.BlockSpec((tq, ATTN_WIDTH), lambda b, i: (b * nq + i, 0)), own_kv, own_kv]
    args = [q, k, v]
    if past:
        cached_kv = pl.BlockSpec((None, None, past, KV_WIDTH), lambda b, i: (b, layer, 0, 0))
        in_specs += [cached_kv, cached_kv]
        args += list(cache)
    in_specs.append(pl.BlockSpec(memory_space=pl.ANY))
    args.append(mix)
    return pl.pallas_call(
        functools.partial(_attention_kernel, past=past),
        out_shape=jax.ShapeDtypeStruct(mix.shape, mix.dtype),
        grid=(batch, nq),
        in_specs=in_specs,
        out_specs=pl.BlockSpec((tq, ATTN_WIDTH), lambda b, i: (b * nq + i, MIX_OFF_ATTN // ATTN_WIDTH)),
        input_output_aliases={len(args) - 1: 0},
        scratch_shapes=[pltpu.VMEM((ATTN_KV_HEADS, past + seq, GROUP_WIDTH), BF16)] * 2,
        compiler_params=_params("parallel", "arbitrary"),
        name="attention_latent" if past else "attention_context",
    )(*args)


def _stack_heads(x):
    block = _lane_block(x.shape, RET_DK)
    zero = jnp.zeros_like(x)
    return jnp.concatenate([jnp.where(block == h, x, zero) for h in range(RET_HEADS)], axis=0)


def _head_norm(o, gm):
    d = o - _group_mean(o, gm)
    return d * lax.rsqrt(_group_mean(d * d, gm) + NORM_EPS)


def _retention_kernel(*refs, has_state, n_chunks):
    rq_ref, rk_ref, rv_ref, g_ref, gm_ref, df_ref, db_ref, qd_ref, kd_ref, cdf_ref, cdb_ref = refs[:11]
    refs = refs[12:]
    if has_state:
        s0_ref, o_ref, both_ref, upd_ref, st_ref = refs
    else:
        o_ref, sfin_ref, both_ref, upd_ref, st_ref = refs[-5:]

    width = RET_WIDTH
    diag = (lax.broadcasted_iota(jnp.int32, (width, width), 0) // RET_DK
            == lax.broadcasted_iota(jnp.int32, (width, width), 1) // RET_DV)
    chunk_rows = lambda c: pl.ds(c * RET_CHUNK, RET_CHUNK)

    for c in range(n_chunks):
        rows = chunk_rows(c)
        qc, kc, vc = rq_ref[rows, :], rk_ref[rows, :], rv_ref[rows, :]
        scores = _dot_nt(qc, _stack_heads(kc))
        v_heads = _stack_heads(vc)
        both_ref[rows, 0:width] = _dot((scores * df_ref[...]).astype(BF16), v_heads)
        both_ref[rows, width:] = _dot((scores * db_ref[...]).astype(BF16), v_heads)
        k32 = kc.astype(F32)
        k_decayed = jnp.concatenate([k32, k32], axis=1) * kd_ref[...]
        upd_ref[c] = _dot(k_decayed.T.astype(BF16), vc)

    for direction, order, cd_ref in ((0, range(n_chunks), cdf_ref), (1, reversed(range(n_chunks)), cdb_ref)):
        if has_state:
            s = s0_ref[direction].reshape(width, RET_DV)
            state = jnp.where(diag, jnp.concatenate([s] * RET_HEADS, axis=1), 0.0)
        else:
            state = jnp.zeros((width, width), F32)
        for c in order:
            st_ref[direction, c] = state.astype(BF16)
            update = upd_ref[c, direction * width:(direction + 1) * width, :]
            state = cd_ref[...] * state + jnp.where(diag, update, 0.0)
        if not has_state:
            folded = state[:, 0:width // 2] + state[:, width // 2:]
            sfin_ref[direction] = folded[:, 0:RET_DV] + folded[:, RET_DV:]

    for c in range(n_chunks):
        rows = chunk_rows(c)
        q32 = rq_ref[rows, :].astype(F32)
        q_decayed = (jnp.concatenate([q32, q32], axis=1) * qd_ref[...]).astype(BF16)
        both_ref[rows, 0:width] += _dot(q_decayed[:, 0:width], st_ref[0, c])
        both_ref[rows, width:] += _dot(q_decayed[:, width:], st_ref[1, c])

    gated = _silu(g_ref[...]) * _head_norm(both_ref[...], gm_ref[...])
    o_ref[...] = (gated[:, 0:width] + gated[:, width:]).astype(BF16)


def _retention(rq, rk, rv, gates, group_mean, tables, state, new_state, mix, layer, batch, seq):
    has_state = state is not None
    n_chunks = seq // RET_CHUNK
    seq_blk = lambda width: pl.BlockSpec((seq, width), lambda b: (b, 0))
    whole = lambda a: pl.BlockSpec(a.shape, lambda b: (0,) * a.ndim)
    in_specs = ([seq_blk(RET_WIDTH)] * 3 + [seq_blk(2 * RET_WIDTH), whole(group_mean)] + [whole(t) for t in tables]
                + [pl.BlockSpec(memory_space=pl.ANY)])
    args = [rq, rk, rv, gates, group_mean] + list(tables) + [mix]
    aliases = {len(args) - 1: 0}
    out_shape = [jax.ShapeDtypeStruct(mix.shape, mix.dtype)]
    out_specs = [pl.BlockSpec((seq, RET_WIDTH), lambda b: (b, MIX_OFF_RET // RET_WIDTH))]
    if has_state:
        in_specs.append(pl.BlockSpec((None, None, 2, RET_HEADS, RET_DK, RET_DV), lambda b: (b, layer, 0, 0, 0, 0)))
        args.append(state)
    else:
        out_shape.append(jax.ShapeDtypeStruct((batch, DEPTH, 2, RET_HEADS * RET_DK, RET_DV), F32))
        out_specs.append(pl.BlockSpec((None, None, 2, RET_HEADS * RET_DK, RET_DV), lambda b: (b, layer, 0, 0, 0)))
        if new_state is not None:
            aliases[len(args)] = 1
            in_specs.append(pl.BlockSpec(memory_space=pl.ANY))
            args.append(new_state)
    return pl.pallas_call(
        functools.partial(_retention_kernel, has_state=has_state, n_chunks=n_chunks),
        out_shape=out_shape,
        grid=(batch,),
        in_specs=in_specs,
        out_specs=out_specs,
        input_output_aliases=aliases,
        scratch_shapes=[pltpu.VMEM((seq, 2 * RET_WIDTH), F32),
                        pltpu.VMEM((n_chunks, 2 * RET_WIDTH, RET_WIDTH), F32),
                        pltpu.VMEM((2, n_chunks, RET_WIDTH, RET_WIDTH), BF16)],
        compiler_params=_params("parallel"),
        name="retention_latent" if has_state else "retention_context",
    )(*args)


def _shift_rows(u, seq):
    rows = u.shape[0]
    pos = lax.broadcasted_iota(jnp.int32, u.shape, 0) % seq
    prev = jnp.where(pos == 0, 0.0, pltpu.roll(u, 1, 0))
    nxt = jnp.where(pos == seq - 1, 0.0, pltpu.roll(u, rows - 1, 0))
    return prev, nxt


def _mix_ffn_kernel(*refs, seq, halo, final):
    if halo:
        (x_ref, x_top_ref, x_bot_ref, mix_ref, mix_top_ref, mix_bot_ref,
         mod_ref, wout_ref, n2_ref, wup_ref, cw_ref, cb_ref, wd_ref, fin_ref, o_ref) = refs
    else:
        x_ref, mix_ref, mod_ref, wout_ref, n2_ref, wup_ref, cw_ref, cb_ref, wd_ref, fin_ref, o_ref = refs
    rows = x_ref.shape[0]
    modulate = lambda x: _rms_rows(x, n2_ref[...]) * (1.0 + mod_ref[4:5, :]) + mod_ref[3:4, :]
    if halo:
        pad = x_top_ref.shape[0]
        skip = mix_top_ref.shape[0] - pad
        tiles_per_seq = seq // rows
        place = pl.program_id(0) % tiles_per_seq
        mixed = _dot(jnp.concatenate([mix_top_ref[...], mix_ref[...], mix_bot_ref[...]], axis=0), wout_ref[...])
        x_ext = jnp.concatenate([x_top_ref[...], x_ref[...], x_bot_ref[...]], axis=0)
        x1_ext = x_ext + mod_ref[2:3, :] * mixed[skip:skip + pad + rows + pad, :]
        x1 = x1_ext[pad:pad + rows, :]
        h2 = modulate(x1_ext)
        row = lax.broadcasted_iota(jnp.int32, h2.shape, 0)
        first_kept = jnp.where(place == 0, pad, 0)
        end_kept = jnp.where(place == tiles_per_seq - 1, pad + rows, pad + rows + pad)
        h2 = jnp.where((row >= first_kept) & (row < end_kept), h2, 0.0)
        u = _dot(h2.astype(BF16), wup_ref[...])
        prev, nxt = pltpu.roll(u, 1, 0), pltpu.roll(u, u.shape[0] - 1, 0)
    else:
        x1 = x_ref[...] + mod_ref[2:3, :] * _dot(mix_ref[...], wout_ref[...])
        u = _dot(modulate(x1).astype(BF16), wup_ref[...])
        prev, nxt = _shift_rows(u, seq)
    u = prev * cw_ref[0:1, :] + u * cw_ref[1:2, :] + nxt * cw_ref[2:3, :] + cb_ref[...]
    if halo:
        u = u[pad:pad + rows, :]
    act = _silu(u[:, 0:D_FF]) * u[:, D_FF:]
    y = x1 + mod_ref[5:6, :] * _dot(act.astype(BF16), wd_ref[...])
    o_ref[...] = _rms_rows(y, fin_ref[...]) if final else y


def _mix_ffn(x, mix, mod, w_out, norm2_w, w_up, conv_w, conv_b, w_down, final_w, layer, seq, per_seq_mod, final):
    n = x.shape[0]
    rows = FFN_ROWS
    halo = seq > rows
    assert (seq % rows == 0) if halo else (rows % seq == 0)
    row_blk = lambda width: pl.BlockSpec((rows, width), lambda i: (i, 0))
    whole = lambda shape: pl.BlockSpec(shape, lambda i: (0,) * len(shape))
    resident = lambda r, c: pl.BlockSpec((None, r, c), lambda i: (layer, 0, 0), pipeline_mode=pl.Buffered(1))
    mod_idx = (lambda i: (i * rows // seq, 0, 0)) if per_seq_mod else (lambda i: (0, 0, 0))

    def with_halo(a, pad):
        width = a.shape[1]
        per_tile, last = rows // pad, n // pad - 1
        return ([row_blk(width),
                 pl.BlockSpec((pad, width), lambda i: (jnp.maximum(i * per_tile - 1, 0), 0)),
                 pl.BlockSpec((pad, width), lambda i: (jnp.minimum((i + 1) * per_tile, last), 0))], [a, a, a])

    if halo:
        x_specs, x_args = with_halo(x, F32_SUBLANES)
        mix_specs, mix_args = with_halo(mix, BF16_SUBLANES)
    else:
        x_specs, x_args, mix_specs, mix_args = [row_blk(D_MODEL)], [x], [row_blk(MIX_WIDTH)], [mix]
    in_specs = x_specs + mix_specs + [
        pl.BlockSpec((None, N_MOD, D_MODEL), mod_idx), resident(MIX_WIDTH, D_MODEL), whole((1, D_MODEL)),
        resident(D_MODEL, 2 * D_FF), whole((3, 2 * D_FF)), whole((1, 2 * D_FF)),
        resident(D_FF, D_MODEL), whole((1, D_MODEL))]
    args = x_args + mix_args + [mod, w_out, norm2_w, w_up, conv_w, conv_b, w_down, final_w]
    return pl.pallas_call(
        functools.partial(_mix_ffn_kernel, seq=seq, halo=halo, final=final),
        out_shape=jax.ShapeDtypeStruct((n, D_MODEL), F32),
        grid=(n // rows,),
        in_specs=in_specs,
        out_specs=row_blk(D_MODEL),
        compiler_params=_params("parallel"),
        name="mix_ffn_seq%d" % seq,
    )(*args)


def _layer(x, mod, lp, consts, layer, batch, seq, ctx, new_ctx, final):
    latent = ctx is not None
    outs = _in_projection(x, mod, lp["norm1_w"], consts["w_in"], lp["q_norm_w"], lp["k_norm_w"], consts["group_mean"],
                          consts["dft"][seq], consts["rope"] if latent else None,
                          None if new_ctx is None else new_ctx[:2], layer, seq)
    q, k, v, rq, rk, rv, gates, mix = outs[:8]
    mix = _attention(q, k, v, ctx[:2] if latent else None, mix, layer, batch, seq)
    ret_out = _retention(rq, rk, rv, gates, consts["group_mean"], consts["ret_tables"], ctx[2] if latent else None,
                         None if new_ctx is None else new_ctx[2], mix, layer, batch, seq)
    x = _mix_ffn(x, ret_out[0], mod, consts["w_out"], lp["norm2_w"], consts["w_up"], lp["conv_w"], lp["conv_b"],
                 consts["w_down"], consts["final_w"], layer, seq, latent, final)
    if latent:
        return x, None
    return x, (outs[8], outs[9], ret_out[1])


def kernel(x_prompt, x_sample, c, cache_attn_k, cache_attn_v, state_ret, c_ctx, w_mod, b_mod, norm1_w, w_in,
           q_norm_w, k_norm_w, w_out, norm2_w, w_up, conv_w, conv_b, w_down, final_norm_w):
    batch, seq, d = x_prompt.shape
    dec_batch, dec_seq, _ = x_sample.shape
    past = cache_attn_k.shape[2]
    assert d == D_MODEL and w_in.shape == (DEPTH, D_MODEL, IN_WIDTH) and w_up.shape == (DEPTH, D_MODEL, 2 * D_FF)
    assert (batch * seq) % FFN_ROWS == 0 and dec_seq % FFN_ROWS == 0
    assert seq % min(ATTN_Q_ROWS, seq) == 0 and dec_seq % min(ATTN_Q_ROWS, dec_seq) == 0
    assert INPROJ_ROWS % seq == 0 and dec_seq % INPROJ_ROWS == 0 and seq % RET_CHUNK == 0 and dec_seq % RET_CHUNK == 0
    assert dec_batch + 1 <= MOD_ROWS and dec_seq % GRID_W == 0
    assert HEAD_DIM == RET_DV and GROUP_WIDTH == RET_WIDTH

    consts = {
        "group_mean": _group_mean_matrix(GROUP_WIDTH, HEAD_DIM),
        "rope": _rope_tables(dec_seq),
        "ret_tables": _retention_tables(),
        "dft": {s: _dft_tables(s) for s in {seq, dec_seq}},
        "final_w": final_norm_w.reshape(1, D_MODEL),
        "w_in": w_in.astype(BF16),
        "w_out": w_out.astype(BF16),
        "w_up": w_up.astype(BF16),
        "w_down": w_down.astype(BF16),
    }

    cvec = jnp.zeros((MOD_ROWS, D_MODEL), F32).at[0].set(c_ctx).at[1:1 + dec_batch].set(c)
    mod = _modulation(cvec, w_mod, b_mod).reshape(DEPTH, MOD_ROWS, N_MOD, D_MODEL)

    cache_k = cache_attn_k.reshape(dec_batch, DEPTH, past, KV_WIDTH)
    cache_v = cache_attn_v.reshape(dec_batch, DEPTH, past, KV_WIDTH)

    xp = x_prompt.reshape(batch * seq, D_MODEL)
    xs = x_sample.reshape(dec_batch * dec_seq, D_MODEL)
    new_ctx = None
    for layer in range(DEPTH):
        lp = {
            "norm1_w": norm1_w[layer].reshape(1, D_MODEL),
            "q_norm_w": jnp.tile(q_norm_w[layer], ATTN_HEADS).reshape(1, ATTN_WIDTH),
            "k_norm_w": jnp.tile(k_norm_w[layer], ATTN_KV_HEADS).reshape(1, KV_WIDTH),
            "norm2_w": norm2_w[layer].reshape(1, D_MODEL),
            "conv_w": conv_w[layer],
            "conv_b": conv_b[layer].reshape(1, 2 * D_FF),
        }
        final = layer == DEPTH - 1
        xp, new_ctx = _layer(xp, mod[layer, 0:1], lp, consts, layer, batch, seq, None, new_ctx, final)
        xs, _ = _layer(xs, mod[layer, 1:1 + dec_batch], lp, consts, layer, dec_batch, dec_seq,
                       (cache_k, cache_v, state_ret), None, final)
    new_k, new_v, new_s = new_ctx
    return (xp.reshape(batch, seq, D_MODEL), xs.reshape(dec_batch, dec_seq, D_MODEL),
            new_k.reshape(batch, DEPTH, seq, ATTN_KV_HEADS, HEAD_DIM),
            new_v.reshape(batch, DEPTH, seq, ATTN_KV_HEADS, HEAD_DIM),
            new_s.reshape(batch, DEPTH, 2, RET_HEADS, RET_DK, RET_DV))
```

```python
import functools

import jax
import jax.numpy as jnp
import numpy as np
from jax import lax
from jax.experimental import pallas as pl
from jax.experimental.pallas import tpu as pltpu

F32 = jnp.float32
BF16 = jnp.bfloat16

D_MODEL = 1024
DEPTH = 2
GRID_W = 64
NORM_EPS = 1e-6
ATTN_HEADS = 8
ATTN_KV_HEADS = 2
HEAD_DIM = 64
ATTN_WIDTH = ATTN_HEADS * HEAD_DIM
KV_WIDTH = ATTN_KV_HEADS * HEAD_DIM
HEADS_PER_KV = ATTN_HEADS // ATTN_KV_HEADS
GROUP_WIDTH = HEADS_PER_KV * HEAD_DIM
ATTN_LOGIT_SCALE = HEAD_DIM ** -0.5 * 1.4426950408889634
ROPE_THETA = 10000.0
ROPE_AXIS_DIM = HEAD_DIM // 2
ROPE_HALF = ROPE_AXIS_DIM // 2
RET_HEADS = 4
RET_DK = 64
RET_DV = 64
RET_WIDTH = RET_HEADS * RET_DV
RET_CHUNK = 128
RET_DECAY_EXP_FWD = 5.0
RET_DECAY_EXP_BWD = 5.5
FOURIER_GROUPS = 4
FOURIER_DIM = 64
FOURIER_WIDTH = FOURIER_GROUPS * FOURIER_DIM
D_FF = 2816
N_MOD = 6

OFF_Q = 0
OFF_K = OFF_Q + ATTN_WIDTH
OFF_V = OFF_K + KV_WIDTH
OFF_RQ = OFF_V + KV_WIDTH
OFF_RK = OFF_RQ + RET_WIDTH
OFF_RV = OFF_RK + RET_WIDTH
OFF_GF = OFF_RV + RET_WIDTH
OFF_GB = OFF_GF + RET_WIDTH
OFF_FX = OFF_GB + RET_WIDTH
IN_WIDTH = OFF_FX + FOURIER_WIDTH
MIX_OFF_ATTN = 0
MIX_OFF_RET = MIX_OFF_ATTN + ATTN_WIDTH
MIX_OFF_FOURIER = MIX_OFF_RET + RET_WIDTH
MIX_WIDTH = MIX_OFF_FOURIER + FOURIER_WIDTH

V7X_VMEM_BYTES = 64 * 1024 * 1024
VMEM_LIMIT = V7X_VMEM_BYTES - 12 * 1024 * 1024

MOD_ROWS = 16
MOD_COLS = 1536
INPROJ_ROWS = 1024
ATTN_Q_ROWS = 512
FFN_ROWS = 512
F32_SUBLANES = 8
BF16_SUBLANES = 16


def _params(*semantics):
    return pltpu.CompilerParams(dimension_semantics=semantics, vmem_limit_bytes=VMEM_LIMIT)


def _dot(a, b):
    return jnp.dot(a, b, preferred_element_type=F32)


def _group_mean(x, gm):
    xb = x.astype(BF16)
    width, slab = x.shape[1], gm.shape[0]
    if width <= slab:
        return _dot(xb, gm[0:width, 0:width])
    return jnp.concatenate([_dot(xb[:, s:s + slab], gm) for s in range(0, width, slab)], axis=1)


def _dot_nt(a, b):
    return lax.dot_general(a, b, (((1,), (1,)), ((), ())), preferred_element_type=F32)


def _sigmoid(x):
    return 1.0 / (1.0 + jnp.exp(-x))


def _silu(x):
    return x * _sigmoid(x)


def _rms_rows(x, w):
    ms = jnp.mean(x * x, axis=-1, keepdims=True)
    return x * lax.rsqrt(ms + NORM_EPS) * w


def _lane_block(shape, width):
    return lax.broadcasted_iota(jnp.int32, shape, len(shape) - 1) // width


def _group_mean_matrix(width, group):
    idx = np.arange(width) // group
    return jnp.asarray((idx[:, None] == idx[None, :]).astype(np.float32) / group, dtype=BF16)


def _rope_tables(n_tokens):
    pos = np.arange(n_tokens)
    row = (pos // GRID_W).astype(np.float64)
    col = (pos % GRID_W).astype(np.float64)
    freqs = ROPE_THETA ** (-np.arange(ROPE_HALF, dtype=np.float64) / ROPE_HALF)
    d = np.arange(HEAD_DIM)
    coord = np.where((d // ROPE_AXIS_DIM)[None, :] == 0, row[:, None], col[:, None])
    ang = coord * freqs[d % ROPE_HALF][None, :]
    sign = np.where((d & ROPE_HALF) == 0, -1.0, 1.0)[None, :]
    cos = np.tile(np.cos(ang), (1, ATTN_HEADS))
    sin = np.tile(np.sin(ang) * sign, (1, ATTN_HEADS))
    return jnp.asarray(cos, F32), jnp.asarray(sin, F32)


def _dft_tables(seq):
    n = np.arange(seq)
    ang = 2.0 * np.pi * ((n[:, None] * n[None, :]) % seq) / seq
    scale = 1.0 / np.sqrt(seq * FOURIER_DIM)
    position = np.concatenate([np.cos(ang), -np.sin(ang)], axis=1)
    c = np.arange(FOURIER_WIDTH)
    same = (c[:, None] // FOURIER_DIM) == (c[None, :] // FOURIER_DIM)
    angc = 2.0 * np.pi * (((c % FOURIER_DIM)[:, None] * (c % FOURIER_DIM)[None, :]) % FOURIER_DIM) / FOURIER_DIM
    channel = np.concatenate([np.where(same, np.cos(angc), 0.0), np.where(same, np.sin(angc), 0.0)], axis=1)
    as_bf16 = lambda a: jnp.asarray(a, F32).astype(BF16)
    return as_bf16(channel), as_bf16(position * scale)


def _retention_tables():
    heads = jnp.arange(RET_HEADS, dtype=F32)
    idx = jnp.arange(RET_CHUNK, dtype=F32)
    diff = idx[:, None] - idx[None, :]
    out = []
    for exp0, backward in ((RET_DECAY_EXP_FWD, False), (RET_DECAY_EXP_BWD, True)):
        lg = jnp.log1p(-jnp.exp2(-(exp0 + heads)))
        dd = -diff if backward else diff
        inner = jnp.where(dd[None] >= 0, jnp.exp(jnp.maximum(dd, 0.0)[None] * lg[:, None, None]), 0.0)
        inner = inner.transpose(1, 0, 2).reshape(RET_CHUNK, RET_HEADS * RET_CHUNK)
        q_pow = (RET_CHUNK - idx) if backward else (idx + 1.0)
        k_pow = idx if backward else (RET_CHUNK - 1.0 - idx)
        spread = lambda p: jnp.repeat(jnp.exp(p[:, None] * lg[None, :]), RET_DK, axis=1)
        q_decay, k_decay = spread(q_pow), spread(k_pow)
        chunk_decay = jnp.repeat(jnp.exp(RET_CHUNK * lg), RET_DV)[None, :]
        out.append((inner, q_decay, k_decay, chunk_decay))
    (inner_f, qd_f, kd_f, cd_f), (inner_b, qd_b, kd_b, cd_b) = out
    return (inner_f, inner_b, jnp.concatenate([qd_f, qd_b], axis=1), jnp.concatenate([kd_f, kd_b], axis=1), cd_f, cd_b)


def _mod_kernel(c_ref, w_ref, b_ref, o_ref):
    act = _silu(c_ref[...]).astype(BF16)
    o_ref[...] = _dot(act, w_ref[...].astype(BF16)) + b_ref[...]


def _modulation(cvec, w_mod, b_mod):
    n_cols = w_mod.shape[-1]
    return pl.pallas_call(
        _mod_kernel,
        out_shape=jax.ShapeDtypeStruct((DEPTH, MOD_ROWS, n_cols), F32),
        grid=(DEPTH, n_cols // MOD_COLS),
        in_specs=[
            pl.BlockSpec((MOD_ROWS, D_MODEL), lambda l, j: (0, 0)),
            pl.BlockSpec((None, D_MODEL, MOD_COLS), lambda l, j: (l, 0, j)),
            pl.BlockSpec((None, 1, MOD_COLS), lambda l, j: (l, 0, j)),
        ],
        out_specs=pl.BlockSpec((None, MOD_ROWS, MOD_COLS), lambda l, j: (l, 0, j)),
        compiler_params=_params("parallel", "parallel"),
        name="modulation",
    )(cvec, w_mod, b_mod.reshape(DEPTH, 1, n_cols))


def _swap_rotary_pairs(x):
    width = x.shape[-1]
    lane = lax.broadcasted_iota(jnp.int32, x.shape, 1)
    from_below = pltpu.roll(x, ROPE_HALF, 1)
    from_above = pltpu.roll(x, width - ROPE_HALF, 1)
    return jnp.where((lane & ROPE_HALF) != 0, from_below, from_above)


def _inproj_kernel(*refs, latent):
    x_ref, mod_ref, n1_ref, win_ref, qnw_ref, knw_ref, gm_ref, dft_c_ref, dft_p_ref = refs[:9]
    refs = refs[9:]
    if latent:
        cos_ref, sin_ref = refs[:2]
        refs = refs[2:]
        q_ref, k_ref, v_ref, rq_ref, rk_ref, rv_ref, g_ref, four_ref = refs
    else:
        q_ref, k_ref, v_ref, rq_ref, rk_ref, rv_ref, g_ref, four_ref, k32_ref, v32_ref = refs[-10:]

    shift, scale = mod_ref[0:1, :], mod_ref[1:2, :]
    h = _rms_rows(x_ref[...], n1_ref[...]) * (1.0 + scale) + shift
    hb = h.astype(BF16)
    along = _dot(_dot(hb, win_ref[:, OFF_FX:]).astype(BF16), dft_c_ref[...])
    seq = dft_p_ref.shape[0]
    for start in range(0, x_ref.shape[0], seq):
        own = along[start:start + seq, :]
        stacked = jnp.concatenate([own[:, 0:FOURIER_WIDTH], own[:, FOURIER_WIDTH:]], axis=0).astype(BF16)
        four_ref[start:start + seq, :] = _dot(dft_p_ref[...], stacked).astype(BF16)

    all_proj = _dot(hb, win_ref[:, 0:OFF_FX])
    proj = lambda off, width: all_proj[:, off:off + width]

    q = proj(OFF_Q, ATTN_WIDTH)
    q = q * lax.rsqrt(_group_mean(q * q, gm_ref[...]) + NORM_EPS) * qnw_ref[...]
    k = proj(OFF_K, KV_WIDTH)
    k = k * lax.rsqrt(_group_mean(k * k, gm_ref[...]) + NORM_EPS) * knw_ref[...]
    v = proj(OFF_V, KV_WIDTH)
    if latent:
        cos, sin = cos_ref[...], sin_ref[...]
        q = q * cos + _swap_rotary_pairs(q) * sin
        k = k * cos[:, 0:KV_WIDTH] + _swap_rotary_pairs(k) * sin[:, 0:KV_WIDTH]
    else:
        k32_ref[...] = k.reshape(k32_ref.shape)
        v32_ref[...] = v.reshape(v32_ref.shape)
    q_ref[...] = (q * ATTN_LOGIT_SCALE).astype(BF16)
    k_ref[...] = k.astype(BF16)
    v_ref[...] = v.astype(BF16)
    rq_ref[...] = proj(OFF_RQ, RET_WIDTH).astype(BF16)
    rk_ref[...] = (proj(OFF_RK, RET_WIDTH) * RET_DK ** -0.5).astype(BF16)
    rv_ref[...] = proj(OFF_RV, RET_WIDTH).astype(BF16)
    g_ref[...] = proj(OFF_GF, 2 * RET_WIDTH)


def _in_projection(x, mod, norm1_w, w_in, q_norm_w, k_norm_w, group_mean, dft, rope, new_cache, layer, seq):
    n = x.shape[0]
    rows = INPROJ_ROWS
    latent = rope is not None
    row_blk = lambda width, col=0: pl.BlockSpec((rows, width), lambda i: (i, col))
    whole = lambda shape: pl.BlockSpec(shape, lambda i: (0,) * len(shape))
    once = lambda shape: pl.BlockSpec(shape, lambda i: (0,) * len(shape), pipeline_mode=pl.Buffered(1))
    mod_idx = (lambda i: (i * rows // seq, 0, 0)) if latent else (lambda i: (0, 0, 0))
    in_specs = [
        row_blk(D_MODEL),
        pl.BlockSpec((None, N_MOD, D_MODEL), mod_idx),
        whole((1, D_MODEL)),
        pl.BlockSpec((None, D_MODEL, IN_WIDTH), lambda i: (layer, 0, 0), pipeline_mode=pl.Buffered(1)),
        whole((1, ATTN_WIDTH)),
        whole((1, KV_WIDTH)),
        whole(group_mean.shape),
        once(dft[0].shape),
        once(dft[1].shape),
    ]
    args = [x, mod, norm1_w, w_in, q_norm_w, k_norm_w, group_mean, dft[0], dft[1]]
    outs = [(ATTN_WIDTH, BF16), (KV_WIDTH, BF16), (KV_WIDTH, BF16), (RET_WIDTH, BF16), (RET_WIDTH, BF16),
            (RET_WIDTH, BF16), (2 * RET_WIDTH, F32)]
    out_shape = [jax.ShapeDtypeStruct((n, w), dt) for w, dt in outs] + [jax.ShapeDtypeStruct((n, MIX_WIDTH), BF16)]
    out_specs = [row_blk(w) for w, _ in outs] + [row_blk(FOURIER_WIDTH, MIX_OFF_FOURIER // FOURIER_WIDTH)]
    aliases = {}
    if latent:
        pos_blk = pl.BlockSpec((rows, ATTN_WIDTH), lambda i: (i % (seq // rows), 0))
        in_specs += [pos_blk, pos_blk]
        args += list(rope)
    else:
        seqs = rows // seq
        out_shape += [jax.ShapeDtypeStruct((n // seq, DEPTH, seq, KV_WIDTH), F32)] * 2
        out_specs += [pl.BlockSpec((seqs, None, seq, KV_WIDTH), lambda i: (i, layer, 0, 0))] * 2
        if new_cache is not None:
            for j, earlier in enumerate(new_cache):
                aliases[len(args)] = len(out_shape) - 2 + j
                in_specs.append(pl.BlockSpec(memory_space=pl.ANY))
                args.append(earlier)
    return pl.pallas_call(
        functools.partial(_inproj_kernel, latent=latent),
        out_shape=out_shape,
        grid=(n // rows,),
        in_specs=in_specs,
        out_specs=out_specs,
        input_output_aliases=aliases,
        compiler_params=_params("parallel"),
        name="in_projection_latent" if latent else "in_projection_context",
    )(*args)


def _spread_kv(x, group):
    lane = lax.broadcasted_iota(jnp.int32, x.shape, 1)
    other = pltpu.roll(x, HEAD_DIM, 1)
    own = (lane // HEAD_DIM) == group
    pair = jnp.where(own, x, other).astype(BF16)
    return jnp.concatenate([pair, pair], axis=1)


def _attention_kernel(*refs, past):
    if past:
        q_ref, k_ref, v_ref, ck_ref, cv_ref, _, o_ref, kt_ref, vt_ref = refs
    else:
        q_ref, k_ref, v_ref, _, o_ref, kt_ref, vt_ref = refs

    @pl.when(pl.program_id(1) == 0)
    def _():
        for g in range(ATTN_KV_HEADS):
            if past:
                kt_ref[g, 0:past, :] = _spread_kv(ck_ref[...], g)
                vt_ref[g, 0:past, :] = _spread_kv(cv_ref[...], g)
            kt_ref[g, past:, :] = _spread_kv(k_ref[...].astype(F32), g)
            vt_ref[g, past:, :] = _spread_kv(v_ref[...].astype(F32), g)

    rows = q_ref.shape[0]
    block = _lane_block((rows, GROUP_WIDTH), HEAD_DIM)

    def logits(head):
        g, h = divmod(head, HEADS_PER_KV)
        qg = q_ref[:, g * GROUP_WIDTH:(g + 1) * GROUP_WIDTH]
        return _dot_nt(jnp.where(block == h, qg, jnp.zeros_like(qg)), kt_ref[g])

    s_next = logits(0)
    out = None
    for head in range(ATTN_HEADS):
        g, h = divmod(head, HEADS_PER_KV)
        s = s_next
        if head + 1 < ATTN_HEADS:
            s_next = logits(head + 1)
        p = jnp.exp2(s - jnp.max(s, axis=-1, keepdims=True))
        denom = jnp.sum(p, axis=-1, keepdims=True)
        o = _dot(p.astype(BF16), vt_ref[g])
        out = o / denom if h == 0 else jnp.where(block == h, o / denom, out)
        if h == HEADS_PER_KV - 1:
            o_ref[:, g * GROUP_WIDTH:(g + 1) * GROUP_WIDTH] = out.astype(BF16)


def _attention(q, k, v, cache, mix, layer, batch, seq):
    tq = min(ATTN_Q_ROWS, seq)
    nq = seq // tq
    past = 0 if cache is None else cache[0].shape[2]
    own_kv = pl.BlockSpec((seq, KV_WIDTH), lambda b, i: (b, 0))
    in_specs = [pl.BlockSpec((tq, ATTN_WIDTH), lambda b, i: (b * nq + i, 0)), own_kv, own_kv]
    args = [q, k, v]
    if past:
        cached_kv = pl.BlockSpec((None, None, past, KV_WIDTH), lambda b, i: (b, layer, 0, 0))
        in_specs += [cached_kv, cached_kv]
        args += list(cache)
    in_specs.append(pl.BlockSpec(memory_space=pl.ANY))
    args.append(mix)
    return pl.pallas_call(
        functools.partial(_attention_kernel, past=past),
        out_shape=jax.ShapeDtypeStruct(mix.shape, mix.dtype),
        grid=(batch, nq),
        in_specs=in_specs,
        out_specs=pl.BlockSpec((tq, ATTN_WIDTH), lambda b, i: (b * nq + i, MIX_OFF_ATTN // ATTN_WIDTH)),
        input_output_aliases={len(args) - 1: 0},
        scratch_shapes=[pltpu.VMEM((ATTN_KV_HEADS, past + seq, GROUP_WIDTH), BF16)] * 2,
        compiler_params=_params("parallel", "arbitrary"),
        name="attention_latent" if past else "attention_context",
    )(*args)


def _stack_heads(x):
    block = _lane_block(x.shape, RET_DK)
    zero = jnp.zeros_like(x)
    return jnp.concatenate([jnp.where(block == h, x, zero) for h in range(RET_HEADS)], axis=-2)


def _head_norm(o, gm):
    d = o - _group_mean(o, gm)
    return d * lax.rsqrt(_group_mean(d * d, gm) + NORM_EPS)


def _retention_kernel(*refs, has_state, n_chunks):
    rq_ref, rk_ref, rv_ref, g_ref, gm_ref, df_ref, db_ref, qd_ref, kd_ref, cdf_ref, cdb_ref = refs[:11]
    refs = refs[12:]
    if has_state:
        s0_ref, o_ref, both_ref, upd_ref, st_ref = refs
    else:
        o_ref, sfin_ref, both_ref, upd_ref, st_ref = refs[-5:]

    width = RET_WIDTH
    diag = (lax.broadcasted_iota(jnp.int32, (width, width), 0) // RET_DK
            == lax.broadcasted_iota(jnp.int32, (width, width), 1) // RET_DV)

    chunked = lambda ref: ref[...].reshape(n_chunks, RET_CHUNK, width)
    batched = lambda a, b, contract: lax.dot_general(a, b, (contract, ((0,), (0,))), preferred_element_type=F32)
    q3, k3, v3 = chunked(rq_ref), chunked(rk_ref), chunked(rv_ref)
    scores = batched(q3, _stack_heads(k3), ((2,), (2,)))
    v_heads = _stack_heads(v3)
    seq_rows = n_chunks * RET_CHUNK
    both_ref[:, 0:width] = batched((scores * df_ref[...]).astype(BF16), v_heads, ((2,), (1,))).reshape(seq_rows, width)
    both_ref[:, width:] = batched((scores * db_ref[...]).astype(BF16), v_heads, ((2,), (1,))).reshape(seq_rows, width)
    k32 = k3.astype(F32)
    k_decayed = jnp.concatenate([k32, k32], axis=2) * kd_ref[...]
    upd_ref[...] = batched(jnp.swapaxes(k_decayed, 1, 2).astype(BF16), v3, ((2,), (1,)))

    for direction, order, cd_ref in ((0, range(n_chunks), cdf_ref), (1, reversed(range(n_chunks)), cdb_ref)):
        if has_state:
            s = s0_ref[direction].reshape(width, RET_DV)
            state = jnp.where(diag, jnp.concatenate([s] * RET_HEADS, axis=1), 0.0)
        else:
            state = jnp.zeros((width, width), F32)
        for c in order:
            st_ref[direction, c] = state.astype(BF16)
            update = upd_ref[c, direction * width:(direction + 1) * width, :]
            state = cd_ref[...] * state + jnp.where(diag, update, 0.0)
        if not has_state:
            folded = state[:, 0:width // 2] + state[:, width // 2:]
            sfin_ref[direction] = folded[:, 0:RET_DV] + folded[:, RET_DV:]

    q32 = q3.astype(F32)
    q_decayed = (jnp.concatenate([q32, q32], axis=2) * qd_ref[...]).astype(BF16)
    both_ref[:, 0:width] += batched(q_decayed[:, :, 0:width], st_ref[0], ((2,), (1,))).reshape(seq_rows, width)
    both_ref[:, width:] += batched(q_decayed[:, :, width:], st_ref[1], ((2,), (1,))).reshape(seq_rows, width)

    gated = _silu(g_ref[...]) * _head_norm(both_ref[...], gm_ref[...])
    o_ref[...] = (gated[:, 0:width] + gated[:, width:]).astype(BF16)


def _retention(rq, rk, rv, gates, group_mean, tables, state, new_state, mix, layer, batch, seq):
    has_state = state is not None
    n_chunks = seq // RET_CHUNK
    seq_blk = lambda width: pl.BlockSpec((seq, width), lambda b: (b, 0))
    whole = lambda a: pl.BlockSpec(a.shape, lambda b: (0,) * a.ndim)
    in_specs = ([seq_blk(RET_WIDTH)] * 3 + [seq_blk(2 * RET_WIDTH), whole(group_mean)] + [whole(t) for t in tables]
                + [pl.BlockSpec(memory_space=pl.ANY)])
    args = [rq, rk, rv, gates, group_mean] + list(tables) + [mix]
    aliases = {len(args) - 1: 0}
    out_shape = [jax.ShapeDtypeStruct(mix.shape, mix.dtype)]
    out_specs = [pl.BlockSpec((seq, RET_WIDTH), lambda b: (b, MIX_OFF_RET // RET_WIDTH))]
    if has_state:
        in_specs.append(pl.BlockSpec((None, None, 2, RET_HEADS, RET_DK, RET_DV), lambda b: (b, layer, 0, 0, 0, 0)))
        args.append(state)
    else:
        out_shape.append(jax.ShapeDtypeStruct((batch, DEPTH, 2, RET_HEADS * RET_DK, RET_DV), F32))
        out_specs.append(pl.BlockSpec((None, None, 2, RET_HEADS * RET_DK, RET_DV), lambda b: (b, layer, 0, 0, 0)))
        if new_state is not None:
            aliases[len(args)] = 1
            in_specs.append(pl.BlockSpec(memory_space=pl.ANY))
            args.append(new_state)
    return pl.pallas_call(
        functools.partial(_retention_kernel, has_state=has_state, n_chunks=n_chunks),
        out_shape=out_shape,
        grid=(batch,),
        in_specs=in_specs,
        out_specs=out_specs,
        input_output_aliases=aliases,
        scratch_shapes=[pltpu.VMEM((seq, 2 * RET_WIDTH), F32),
                        pltpu.VMEM((n_chunks, 2 * RET_WIDTH, RET_WIDTH), F32),
                        pltpu.VMEM((2, n_chunks, RET_WIDTH, RET_WIDTH), BF16)],
        compiler_params=_params("parallel"),
        name="retention_latent" if has_state else "retention_context",
    )(*args)


def _shift_rows(u, seq):
    rows = u.shape[0]
    pos = lax.broadcasted_iota(jnp.int32, u.shape, 0) % seq
    prev = jnp.where(pos == 0, 0.0, pltpu.roll(u, 1, 0))
    nxt = jnp.where(pos == seq - 1, 0.0, pltpu.roll(u, rows - 1, 0))
    return prev, nxt


def _mix_ffn_kernel(*refs, seq, halo, final):
    if halo:
        (x_ref, x_top_ref, x_bot_ref, mix_ref, mix_top_ref, mix_bot_ref,
         mod_ref, wout_ref, n2_ref, wup_ref, cw_ref, cb_ref, wd_ref, fin_ref, o_ref) = refs
    else:
        x_ref, mix_ref, mod_ref, wout_ref, n2_ref, wup_ref, cw_ref, cb_ref, wd_ref, fin_ref, o_ref = refs
    rows = x_ref.shape[0]
    modulate = lambda x: _rms_rows(x, n2_ref[...]) * (1.0 + mod_ref[4:5, :]) + mod_ref[3:4, :]
    if halo:
        pad = x_top_ref.shape[0]
        skip = mix_top_ref.shape[0] - pad
        tiles_per_seq = seq // rows
        place = pl.program_id(0) % tiles_per_seq
        mixed = _dot(jnp.concatenate([mix_top_ref[...], mix_ref[...], mix_bot_ref[...]], axis=0), wout_ref[...])
        x_ext = jnp.concatenate([x_top_ref[...], x_ref[...], x_bot_ref[...]], axis=0)
        x1_ext = x_ext + mod_ref[2:3, :] * mixed[skip:skip + pad + rows + pad, :]
        x1 = x1_ext[pad:pad + rows, :]
        h2 = modulate(x1_ext)
        row = lax.broadcasted_iota(jnp.int32, h2.shape, 0)
        first_kept = jnp.where(place == 0, pad, 0)
        end_kept = jnp.where(place == tiles_per_seq - 1, pad + rows, pad + rows + pad)
        h2 = jnp.where((row >= first_kept) & (row < end_kept), h2, 0.0)
        u = _dot(h2.astype(BF16), wup_ref[...])
        prev, nxt = pltpu.roll(u, 1, 0), pltpu.roll(u, u.shape[0] - 1, 0)
    else:
        x1 = x_ref[...] + mod_ref[2:3, :] * _dot(mix_ref[...], wout_ref[...])
        u = _dot(modulate(x1).astype(BF16), wup_ref[...])
        prev, nxt = _shift_rows(u, seq)
    u = prev * cw_ref[0:1, :] + u * cw_ref[1:2, :] + nxt * cw_ref[2:3, :] + cb_ref[...]
    if halo:
        u = u[pad:pad + rows, :]
    act = _silu(u[:, 0:D_FF]) * u[:, D_FF:]
    y = x1 + mod_ref[5:6, :] * _dot(act.astype(BF16), wd_ref[...])
    o_ref[...] = _rms_rows(y, fin_ref[...]) if final else y


def _mix_ffn(x, mix, mod, w_out, norm2_w, w_up, conv_w, conv_b, w_down, final_w, layer, seq, per_seq_mod, final):
    n = x.shape[0]
    rows = FFN_ROWS
    halo = seq > rows
    assert (seq % rows == 0) if halo else (rows % seq == 0)
    row_blk = lambda width: pl.BlockSpec((rows, width), lambda i: (i, 0))
    whole = lambda shape: pl.BlockSpec(shape, lambda i: (0,) * len(shape))
    resident = lambda r, c: pl.BlockSpec((None, r, c), lambda i: (layer, 0, 0), pipeline_mode=pl.Buffered(1))
    mod_idx = (lambda i: (i * rows // seq, 0, 0)) if per_seq_mod else (lambda i: (0, 0, 0))

    def with_halo(a, pad):
        width = a.shape[1]
        per_tile, last = rows // pad, n // pad - 1
        return ([row_blk(width),
                 pl.BlockSpec((pad, width), lambda i: (jnp.maximum(i * per_tile - 1, 0), 0)),
                 pl.BlockSpec((pad, width), lambda i: (jnp.minimum((i + 1) * per_tile, last), 0))], [a, a, a])

    if halo:
        x_specs, x_args = with_halo(x, F32_SUBLANES)
        mix_specs, mix_args = with_halo(mix, BF16_SUBLANES)
    else:
        x_specs, x_args, mix_specs, mix_args = [row_blk(D_MODEL)], [x], [row_blk(MIX_WIDTH)], [mix]
    in_specs = x_specs + mix_specs + [
        pl.BlockSpec((None, N_MOD, D_MODEL), mod_idx), resident(MIX_WIDTH, D_MODEL), whole((1, D_MODEL)),
        resident(D_MODEL, 2 * D_FF), whole((3, 2 * D_FF)), whole((1, 2 * D_FF)),
        resident(D_FF, D_MODEL), whole((1, D_MODEL))]
    args = x_args + mix_args + [mod, w_out, norm2_w, w_up, conv_w, conv_b, w_down, final_w]
    return pl.pallas_call(
        functools.partial(_mix_ffn_kernel, seq=seq, halo=halo, final=final),
        out_shape=jax.ShapeDtypeStruct((n, D_MODEL), F32),
        grid=(n // rows,),
        in_specs=in_specs,
        out_specs=row_blk(D_MODEL),
        compiler_params=_params("parallel"),
        name="mix_ffn_seq%d" % seq,
    )(*args)


def _layer(x, mod, lp, consts, layer, batch, seq, ctx, new_ctx, final):
    latent = ctx is not None
    outs = _in_projection(x, mod, lp["norm1_w"], consts["w_in"], lp["q_norm_w"], lp["k_norm_w"], consts["group_mean"],
                          consts["dft"][seq], consts["rope"] if latent else None,
                          None if new_ctx is None else new_ctx[:2], layer, seq)
    q, k, v, rq, rk, rv, gates, mix = outs[:8]
    mix = _attention(q, k, v, ctx[:2] if latent else None, mix, layer, batch, seq)
    ret_out = _retention(rq, rk, rv, gates, consts["group_mean"], consts["ret_tables"], ctx[2] if latent else None,
                         None if new_ctx is None else new_ctx[2], mix, layer, batch, seq)
    x = _mix_ffn(x, ret_out[0], mod, consts["w_out"], lp["norm2_w"], consts["w_up"], lp["conv_w"], lp["conv_b"],
                 consts["w_down"], consts["final_w"], layer, seq, latent, final)
    if latent:
        return x, None
    return x, (outs[8], outs[9], ret_out[1])


def kernel(x_prompt, x_sample, c, cache_attn_k, cache_attn_v, state_ret, c_ctx, w_mod, b_mod, norm1_w, w_in,
           q_norm_w, k_norm_w, w_out, norm2_w, w_up, conv_w, conv_b, w_down, final_norm_w):
    batch, seq, d = x_prompt.shape
    dec_batch, dec_seq, _ = x_sample.shape
    past = cache_attn_k.shape[2]
    assert d == D_MODEL and w_in.shape == (DEPTH, D_MODEL, IN_WIDTH) and w_up.shape == (DEPTH, D_MODEL, 2 * D_FF)
    assert (batch * seq) % FFN_ROWS == 0 and dec_seq % FFN_ROWS == 0
    assert seq % min(ATTN_Q_ROWS, seq) == 0 and dec_seq % min(ATTN_Q_ROWS, dec_seq) == 0
    assert INPROJ_ROWS % seq == 0 and dec_seq % INPROJ_ROWS == 0 and seq % RET_CHUNK == 0 and dec_seq % RET_CHUNK == 0
    assert dec_batch + 1 <= MOD_ROWS and dec_seq % GRID_W == 0
    assert HEAD_DIM == RET_DV and GROUP_WIDTH == RET_WIDTH

    consts = {
        "group_mean": _group_mean_matrix(GROUP_WIDTH, HEAD_DIM),
        "rope": _rope_tables(dec_seq),
        "ret_tables": _retention_tables(),
        "dft": {s: _dft_tables(s) for s in {seq, dec_seq}},
        "final_w": final_norm_w.reshape(1, D_MODEL),
        "w_in": w_in.astype(BF16),
        "w_out": w_out.astype(BF16),
        "w_up": w_up.astype(BF16),
        "w_down": w_down.astype(BF16),
    }

    cvec = jnp.zeros((MOD_ROWS, D_MODEL), F32).at[0].set(c_ctx).at[1:1 + dec_batch].set(c)
    mod = _modulation(cvec, w_mod, b_mod).reshape(DEPTH, MOD_ROWS, N_MOD, D_MODEL)

    cache_k = cache_attn_k.reshape(dec_batch, DEPTH, past, KV_WIDTH)
    cache_v = cache_attn_v.reshape(dec_batch, DEPTH, past, KV_WIDTH)

    xp = x_prompt.reshape(batch * seq, D_MODEL)
    xs = x_sample.reshape(dec_batch * dec_seq, D_MODEL)
    new_ctx = None
    for layer in range(DEPTH):
        lp = {
            "norm1_w": norm1_w[layer].reshape(1, D_MODEL),
            "q_norm_w": jnp.tile(q_norm_w[layer], ATTN_HEADS).reshape(1, ATTN_WIDTH),
            "k_norm_w": jnp.tile(k_norm_w[layer], ATTN_KV_HEADS).reshape(1, KV_WIDTH),
            "norm2_w": norm2_w[layer].reshape(1, D_MODEL),
            "conv_w": conv_w[layer],
            "conv_b": conv_b[layer].reshape(1, 2 * D_FF),
        }
        final = layer == DEPTH - 1
        xp, new_ctx = _layer(xp, mod[layer, 0:1], lp, consts, layer, batch, seq, None, new_ctx, final)
        xs, _ = _layer(xs, mod[layer, 1:1 + dec_batch], lp, consts, layer, dec_batch, dec_seq,
                       (cache_k, cache_v, state_ret), None, final)
    new_k, new_v, new_s = new_ctx
    return (xp.reshape(batch, seq, D_MODEL), xs.reshape(dec_batch, dec_seq, D_MODEL),
            new_k.reshape(batch, DEPTH, seq, ATTN_KV_HEADS, HEAD_DIM),
            new_v.reshape(batch, DEPTH, seq, ATTN_KV_HEADS, HEAD_DIM),
            new_s.reshape(batch, DEPTH, 2, RET_HEADS, RET_DK, RET_DV))
```

```python
import functools

import jax
import jax.numpy as jnp
import numpy as np
from jax import lax
from jax.experimental import pallas as pl
from jax.experimental.pallas import tpu as pltpu

F32 = jnp.float32
BF16 = jnp.bfloat16

D_MODEL = 1024
DEPTH = 2
GRID_W = 64
NORM_EPS = 1e-6
ATTN_HEADS = 8
ATTN_KV_HEADS = 2
HEAD_DIM = 64
ATTN_WIDTH = ATTN_HEADS * HEAD_DIM
KV_WIDTH = ATTN_KV_HEADS * HEAD_DIM
HEADS_PER_KV = ATTN_HEADS // ATTN_KV_HEADS
GROUP_WIDTH = HEADS_PER_KV * HEAD_DIM
ATTN_LOGIT_SCALE = HEAD_DIM ** -0.5 * 1.4426950408889634
ROPE_THETA = 10000.0
ROPE_AXIS_DIM = HEAD_DIM // 2
ROPE_HALF = ROPE_AXIS_DIM // 2
RET_HEADS = 4
RET_DK = 64
RET_DV = 64
RET_WIDTH = RET_HEADS * RET_DV
RET_CHUNK = 128
RET_DECAY_EXP_FWD = 5.0
RET_DECAY_EXP_BWD = 5.5
FOURIER_GROUPS = 4
FOURIER_DIM = 64
FOURIER_WIDTH = FOURIER_GROUPS * FOURIER_DIM
D_FF = 2816
N_MOD = 6

OFF_Q = 0
OFF_K = OFF_Q + ATTN_WIDTH
OFF_V = OFF_K + KV_WIDTH
OFF_RQ = OFF_V + KV_WIDTH
OFF_RK = OFF_RQ + RET_WIDTH
OFF_RV = OFF_RK + RET_WIDTH
OFF_GF = OFF_RV + RET_WIDTH
OFF_GB = OFF_GF + RET_WIDTH
OFF_FX = OFF_GB + RET_WIDTH
IN_WIDTH = OFF_FX + FOURIER_WIDTH
MIX_OFF_ATTN = 0
MIX_OFF_RET = MIX_OFF_ATTN + ATTN_WIDTH
MIX_OFF_FOURIER = MIX_OFF_RET + RET_WIDTH
MIX_WIDTH = MIX_OFF_FOURIER + FOURIER_WIDTH

V7X_VMEM_BYTES = 64 * 1024 * 1024
VMEM_LIMIT = V7X_VMEM_BYTES - 12 * 1024 * 1024

MOD_ROWS = 16
MOD_COLS = 1536
INPROJ_ROWS = 1024
ATTN_Q_ROWS = 512
ATTN_LOGIT_ELEMS = 1024 * 1024
FFN_ROWS = 512
F32_SUBLANES = 8
BF16_SUBLANES = 16


def _params(*semantics):
    return pltpu.CompilerParams(dimension_semantics=semantics, vmem_limit_bytes=VMEM_LIMIT)


def _dot(a, b):
    return jnp.dot(a, b, preferred_element_type=F32)


def _group_mean(x, gm):
    xb = x.astype(BF16)
    width, slab = x.shape[1], gm.shape[0]
    if width <= slab:
        return _dot(xb, gm[0:width, 0:width])
    return jnp.concatenate([_dot(xb[:, s:s + slab], gm) for s in range(0, width, slab)], axis=1)


def _dot_nt(a, b):
    return lax.dot_general(a, b, (((1,), (1,)), ((), ())), preferred_element_type=F32)


def _sigmoid(x):
    return 1.0 / (1.0 + jnp.exp(-x))


def _silu(x):
    return x * _sigmoid(x)


def _rms_rows(x, w):
    ms = jnp.mean(x * x, axis=-1, keepdims=True)
    return x * lax.rsqrt(ms + NORM_EPS) * w


def _lane_block(shape, width):
    return lax.broadcasted_iota(jnp.int32, shape, len(shape) - 1) // width


def _group_mean_matrix(width, group):
    idx = np.arange(width) // group
    return jnp.asarray((idx[:, None] == idx[None, :]).astype(np.float32) / group, dtype=BF16)


def _rope_tables(n_tokens):
    pos = np.arange(n_tokens)
    row = (pos // GRID_W).astype(np.float64)
    col = (pos % GRID_W).astype(np.float64)
    freqs = ROPE_THETA ** (-np.arange(ROPE_HALF, dtype=np.float64) / ROPE_HALF)
    d = np.arange(HEAD_DIM)
    coord = np.where((d // ROPE_AXIS_DIM)[None, :] == 0, row[:, None], col[:, None])
    ang = coord * freqs[d % ROPE_HALF][None, :]
    sign = np.where((d & ROPE_HALF) == 0, -1.0, 1.0)[None, :]
    cos = np.tile(np.cos(ang), (1, ATTN_HEADS))
    sin = np.tile(np.sin(ang) * sign, (1, ATTN_HEADS))
    return jnp.asarray(cos, F32), jnp.asarray(sin, F32)


def _dft_tables(seq):
    n = np.arange(seq)
    ang = 2.0 * np.pi * ((n[:, None] * n[None, :]) % seq) / seq
    scale = 1.0 / np.sqrt(seq * FOURIER_DIM)
    position = np.concatenate([np.cos(ang), -np.sin(ang)], axis=1)
    c = np.arange(FOURIER_WIDTH)
    same = (c[:, None] // FOURIER_DIM) == (c[None, :] // FOURIER_DIM)
    angc = 2.0 * np.pi * (((c % FOURIER_DIM)[:, None] * (c % FOURIER_DIM)[None, :]) % FOURIER_DIM) / FOURIER_DIM
    channel = np.concatenate([np.where(same, np.cos(angc), 0.0), np.where(same, np.sin(angc), 0.0)], axis=1)
    as_bf16 = lambda a: jnp.asarray(a, F32).astype(BF16)
    return as_bf16(channel), as_bf16(position * scale)


def _retention_tables():
    heads = jnp.arange(RET_HEADS, dtype=F32)
    idx = jnp.arange(RET_CHUNK, dtype=F32)
    diff = idx[:, None] - idx[None, :]
    out = []
    for exp0, backward in ((RET_DECAY_EXP_FWD, False), (RET_DECAY_EXP_BWD, True)):
        lg = jnp.log1p(-jnp.exp2(-(exp0 + heads)))
        dd = -diff if backward else diff
        inner = jnp.where(dd[None] >= 0, jnp.exp(jnp.maximum(dd, 0.0)[None] * lg[:, None, None]), 0.0)
        inner = inner.transpose(1, 0, 2).reshape(RET_CHUNK, RET_HEADS * RET_CHUNK)
        q_pow = (RET_CHUNK - idx) if backward else (idx + 1.0)
        k_pow = idx if backward else (RET_CHUNK - 1.0 - idx)
        spread = lambda p: jnp.repeat(jnp.exp(p[:, None] * lg[None, :]), RET_DK, axis=1)
        q_decay, k_decay = spread(q_pow), spread(k_pow)
        chunk_decay = jnp.repeat(jnp.exp(RET_CHUNK * lg), RET_DV)[None, :]
        out.append((inner, q_decay, k_decay, chunk_decay))
    (inner_f, qd_f, kd_f, cd_f), (inner_b, qd_b, kd_b, cd_b) = out
    return (inner_f, inner_b, jnp.concatenate([qd_f, qd_b], axis=1), jnp.concatenate([kd_f, kd_b], axis=1), cd_f, cd_b)


def _mod_kernel(c_ref, w_ref, b_ref, o_ref):
    act = _silu(c_ref[...]).astype(BF16)
    o_ref[...] = _dot(act, w_ref[...].astype(BF16)) + b_ref[...]


def _modulation(cvec, w_mod, b_mod):
    n_cols = w_mod.shape[-1]
    return pl.pallas_call(
        _mod_kernel,
        out_shape=jax.ShapeDtypeStruct((DEPTH, MOD_ROWS, n_cols), F32),
        grid=(DEPTH, n_cols // MOD_COLS),
        in_specs=[
            pl.BlockSpec((MOD_ROWS, D_MODEL), lambda l, j: (0, 0)),
            pl.BlockSpec((None, D_MODEL, MOD_COLS), lambda l, j: (l, 0, j)),
            pl.BlockSpec((None, 1, MOD_COLS), lambda l, j: (l, 0, j)),
        ],
        out_specs=pl.BlockSpec((None, MOD_ROWS, MOD_COLS), lambda l, j: (l, 0, j)),
        compiler_params=_params("parallel", "parallel"),
        name="modulation",
    )(cvec, w_mod, b_mod.reshape(DEPTH, 1, n_cols))


def _swap_rotary_pairs(x):
    width = x.shape[-1]
    lane = lax.broadcasted_iota(jnp.int32, x.shape, 1)
    from_below = pltpu.roll(x, ROPE_HALF, 1)
    from_above = pltpu.roll(x, width - ROPE_HALF, 1)
    return jnp.where((lane & ROPE_HALF) != 0, from_below, from_above)


def _inproj_kernel(*refs, latent):
    x_ref, mod_ref, n1_ref, win_ref, qnw_ref, knw_ref, gm_ref, dft_c_ref, dft_p_ref = refs[:9]
    refs = refs[9:]
    if latent:
        cos_ref, sin_ref = refs[:2]
        refs = refs[2:]
        q_ref, k_ref, v_ref, rq_ref, rk_ref, rv_ref, g_ref, four_ref = refs
    else:
        q_ref, k_ref, v_ref, rq_ref, rk_ref, rv_ref, g_ref, four_ref, k32_ref, v32_ref = refs[-10:]

    shift, scale = mod_ref[0:1, :], mod_ref[1:2, :]
    h = _rms_rows(x_ref[...], n1_ref[...]) * (1.0 + scale) + shift
    hb = h.astype(BF16)
    along = _dot(_dot(hb, win_ref[:, OFF_FX:]).astype(BF16), dft_c_ref[...])
    seq = dft_p_ref.shape[0]
    for start in range(0, x_ref.shape[0], seq):
        own = along[start:start + seq, :]
        stacked = jnp.concatenate([own[:, 0:FOURIER_WIDTH], own[:, FOURIER_WIDTH:]], axis=0).astype(BF16)
        four_ref[start:start + seq, :] = _dot(dft_p_ref[...], stacked).astype(BF16)

    all_proj = _dot(hb, win_ref[:, 0:OFF_FX])
    proj = lambda off, width: all_proj[:, off:off + width]

    q = proj(OFF_Q, ATTN_WIDTH)
    q = q * lax.rsqrt(_group_mean(q * q, gm_ref[...]) + NORM_EPS) * qnw_ref[...]
    k = proj(OFF_K, KV_WIDTH)
    k = k * lax.rsqrt(_group_mean(k * k, gm_ref[...]) + NORM_EPS) * knw_ref[...]
    v = proj(OFF_V, KV_WIDTH)
    if latent:
        cos, sin = cos_ref[...], sin_ref[...]
        q = q * cos + _swap_rotary_pairs(q) * sin
        k = k * cos[:, 0:KV_WIDTH] + _swap_rotary_pairs(k) * sin[:, 0:KV_WIDTH]
    else:
        k32_ref[...] = k.reshape(k32_ref.shape)
        v32_ref[...] = v.reshape(v32_ref.shape)
    q_ref[...] = (q * ATTN_LOGIT_SCALE).astype(BF16)
    k_ref[...] = k.astype(BF16)
    v_ref[...] = v.astype(BF16)
    rq_ref[...] = proj(OFF_RQ, RET_WIDTH).astype(BF16)
    rk_ref[...] = (proj(OFF_RK, RET_WIDTH) * RET_DK ** -0.5).astype(BF16)
    rv_ref[...] = proj(OFF_RV, RET_WIDTH).astype(BF16)
    g_ref[...] = proj(OFF_GF, 2 * RET_WIDTH)


def _in_projection(x, mod, norm1_w, w_in, q_norm_w, k_norm_w, group_mean, dft, rope, new_cache, layer, seq):
    n = x.shape[0]
    rows = INPROJ_ROWS
    latent = rope is not None
    row_blk = lambda width, col=0: pl.BlockSpec((rows, width), lambda i: (i, col))
    whole = lambda shape: pl.BlockSpec(shape, lambda i: (0,) * len(shape))
    once = lambda shape: pl.BlockSpec(shape, lambda i: (0,) * len(shape), pipeline_mode=pl.Buffered(1))
    mod_idx = (lambda i: (i * rows // seq, 0, 0)) if latent else (lambda i: (0, 0, 0))
    in_specs = [
        row_blk(D_MODEL),
        pl.BlockSpec((None, N_MOD, D_MODEL), mod_idx),
        whole((1, D_MODEL)),
        pl.BlockSpec((None, D_MODEL, IN_WIDTH), lambda i: (layer, 0, 0), pipeline_mode=pl.Buffered(1)),
        whole((1, ATTN_WIDTH)),
        whole((1, KV_WIDTH)),
        whole(group_mean.shape),
        once(dft[0].shape),
        once(dft[1].shape),
    ]
    args = [x, mod, norm1_w, w_in, q_norm_w, k_norm_w, group_mean, dft[0], dft[1]]
    outs = [(ATTN_WIDTH, BF16), (KV_WIDTH, BF16), (KV_WIDTH, BF16), (RET_WIDTH, BF16), (RET_WIDTH, BF16),
            (RET_WIDTH, BF16), (2 * RET_WIDTH, F32)]
    out_shape = [jax.ShapeDtypeStruct((n, w), dt) for w, dt in outs] + [jax.ShapeDtypeStruct((n, MIX_WIDTH), BF16)]
    out_specs = [row_blk(w) for w, _ in outs] + [row_blk(FOURIER_WIDTH, MIX_OFF_FOURIER // FOURIER_WIDTH)]
    aliases = {}
    if latent:
        pos_blk = pl.BlockSpec((rows, ATTN_WIDTH), lambda i: (i % (seq // rows), 0))
        in_specs += [pos_blk, pos_blk]
        args += list(rope)
    else:
        seqs = rows // seq
        out_shape += [jax.ShapeDtypeStruct((n // seq, DEPTH, seq, KV_WIDTH), F32)] * 2
        out_specs += [pl.BlockSpec((seqs, None, seq, KV_WIDTH), lambda i: (i, layer, 0, 0))] * 2
        if new_cache is not None:
            for j, earlier in enumerate(new_cache):
                aliases[len(args)] = len(out_shape) - 2 + j
                in_specs.append(pl.BlockSpec(memory_space=pl.ANY))
                args.append(earlier)
    return pl.pallas_call(
        functools.partial(_inproj_kernel, latent=latent),
        out_shape=out_shape,
        grid=(n // rows,),
        in_specs=in_specs,
        out_specs=out_specs,
        input_output_aliases=aliases,
        compiler_params=_params("parallel"),
        name="in_projection_latent" if latent else "in_projection_context",
    )(*args)


def _spread_kv(x, group):
    lane = lax.broadcasted_iota(jnp.int32, x.shape, 1)
    other = pltpu.roll(x, HEAD_DIM, 1)
    own = (lane // HEAD_DIM) == group
    pair = jnp.where(own, x, other).astype(BF16)
    return jnp.concatenate([pair, pair], axis=1)


def _attention_kernel(*refs, past, heads_per_dot):
    if past:
        q_ref, k_ref, v_ref, ck_ref, cv_ref, _, o_ref, kt_ref, vt_ref = refs
    else:
        q_ref, k_ref, v_ref, _, o_ref, kt_ref, vt_ref = refs

    @pl.when(pl.program_id(1) == 0)
    def _():
        for g in range(ATTN_KV_HEADS):
            if past:
                kt_ref[g, 0:past, :] = _spread_kv(ck_ref[...], g)
                vt_ref[g, 0:past, :] = _spread_kv(cv_ref[...], g)
            kt_ref[g, past:, :] = _spread_kv(k_ref[...].astype(F32), g)
            vt_ref[g, past:, :] = _spread_kv(v_ref[...].astype(F32), g)

    rows = q_ref.shape[0]
    block = _lane_block((rows, GROUP_WIDTH), HEAD_DIM)

    def logits(unit):
        g, first = divmod(unit * heads_per_dot, HEADS_PER_KV)
        qg = q_ref[:, g * GROUP_WIDTH:(g + 1) * GROUP_WIDTH]
        zero = jnp.zeros_like(qg)
        stacked = jnp.concatenate([jnp.where(block == first + j, qg, zero) for j in range(heads_per_dot)], axis=0)
        return _dot_nt(stacked, kt_ref[g])

    n_units = ATTN_HEADS // heads_per_dot
    s_next = logits(0)
    out = None
    for unit in range(n_units):
        g, first = divmod(unit * heads_per_dot, HEADS_PER_KV)
        s = s_next
        if unit + 1 < n_units:
            s_next = logits(unit + 1)
        p = jnp.exp2(s - jnp.max(s, axis=-1, keepdims=True))
        denom = jnp.sum(p, axis=-1, keepdims=True)
        o = _dot(p.astype(BF16), vt_ref[g]) / denom
        for j in range(heads_per_dot):
            h = first + j
            piece = o[j * rows:(j + 1) * rows, :]
            out = piece if h == 0 else jnp.where(block == h, piece, out)
        if first + heads_per_dot == HEADS_PER_KV:
            o_ref[:, g * GROUP_WIDTH:(g + 1) * GROUP_WIDTH] = out.astype(BF16)


def _attention(q, k, v, cache, mix, layer, batch, seq):
    tq = min(ATTN_Q_ROWS, seq)
    nq = seq // tq
    past = 0 if cache is None else cache[0].shape[2]
    heads_per_dot = max(1, min(HEADS_PER_KV, ATTN_LOGIT_ELEMS // (tq * (past + seq))))
    assert HEADS_PER_KV % heads_per_dot == 0
    own_kv = pl.BlockSpec((seq, KV_WIDTH), lambda b, i: (b, 0))
    in_specs = [pl.BlockSpec((tq, ATTN_WIDTH), lambda b, i: (b * nq + i, 0)), own_kv, own_kv]
    args = [q, k, v]
    if past:
        cached_kv = pl.BlockSpec((None, None, past, KV_WIDTH), lambda b, i: (b, layer, 0, 0))
        in_specs += [cached_kv, cached_kv]
        args += list(cache)
    in_specs.append(pl.BlockSpec(memory_space=pl.ANY))
    args.append(mix)
    return pl.pallas_call(
        functools.partial(_attention_kernel, past=past, heads_per_dot=heads_per_dot),
        out_shape=jax.ShapeDtypeStruct(mix.shape, mix.dtype),
        grid=(batch, nq),
        in_specs=in_specs,
        out_specs=pl.BlockSpec((tq, ATTN_WIDTH), lambda b, i: (b * nq + i, MIX_OFF_ATTN // ATTN_WIDTH)),
        input_output_aliases={len(args) - 1: 0},
        scratch_shapes=[pltpu.VMEM((ATTN_KV_HEADS, past + seq, GROUP_WIDTH), BF16)] * 2,
        compiler_params=_params("parallel", "arbitrary"),
        name="attention_latent" if past else "attention_context",
    )(*args)


def _stack_heads(x):
    block = _lane_block(x.shape, RET_DK)
    zero = jnp.zeros_like(x)
    return jnp.concatenate([jnp.where(block == h, x, zero) for h in range(RET_HEADS)], axis=-2)


def _head_norm(o, gm):
    d = o - _group_mean(o, gm)
    return d * lax.rsqrt(_group_mean(d * d, gm) + NORM_EPS)


def _retention_kernel(*refs, has_state, n_chunks):
    rq_ref, rk_ref, rv_ref, g_ref, gm_ref, df_ref, db_ref, qd_ref, kd_ref, cdf_ref, cdb_ref = refs[:11]
    refs = refs[12:]
    if has_state:
        s0_ref, o_ref, both_ref, upd_ref, st_ref = refs
    else:
        o_ref, sfin_ref, both_ref, upd_ref, st_ref = refs[-5:]

    width = RET_WIDTH
    diag = (lax.broadcasted_iota(jnp.int32, (width, width), 0) // RET_DK
            == lax.broadcasted_iota(jnp.int32, (width, width), 1) // RET_DV)

    chunked = lambda ref: ref[...].reshape(n_chunks, RET_CHUNK, width)
    batched = lambda a, b, contract: lax.dot_general(a, b, (contract, ((0,), (0,))), preferred_element_type=F32)
    q3, k3, v3 = chunked(rq_ref), chunked(rk_ref), chunked(rv_ref)
    scores = batched(q3, _stack_heads(k3), ((2,), (2,)))
    v_heads = _stack_heads(v3)
    seq_rows = n_chunks * RET_CHUNK
    both_ref[:, 0:width] = batched((scores * df_ref[...]).astype(BF16), v_heads, ((2,), (1,))).reshape(seq_rows, width)
    both_ref[:, width:] = batched((scores * db_ref[...]).astype(BF16), v_heads, ((2,), (1,))).reshape(seq_rows, width)
    k32 = k3.astype(F32)
    k_decayed = jnp.concatenate([k32, k32], axis=2) * kd_ref[...]
    upd_ref[...] = batched(jnp.swapaxes(k_decayed, 1, 2).astype(BF16), v3, ((2,), (1,)))

    for direction, order, cd_ref in ((0, range(n_chunks), cdf_ref), (1, reversed(range(n_chunks)), cdb_ref)):
        if has_state:
            s = s0_ref[direction].reshape(width, RET_DV)
            state = jnp.where(diag, jnp.concatenate([s] * RET_HEADS, axis=1), 0.0)
        else:
            state = jnp.zeros((width, width), F32)
        for c in order:
            st_ref[direction, c] = state.astype(BF16)
            update = upd_ref[c, direction * width:(direction + 1) * width, :]
            state = cd_ref[...] * state + jnp.where(diag, update, 0.0)
        if not has_state:
            folded = state[:, 0:width // 2] + state[:, width // 2:]
            sfin_ref[direction] = folded[:, 0:RET_DV] + folded[:, RET_DV:]

    q32 = q3.astype(F32)
    q_decayed = (jnp.concatenate([q32, q32], axis=2) * qd_ref[...]).astype(BF16)
    both_ref[:, 0:width] += batched(q_decayed[:, :, 0:width], st_ref[0], ((2,), (1,))).reshape(seq_rows, width)
    both_ref[:, width:] += batched(q_decayed[:, :, width:], st_ref[1], ((2,), (1,))).reshape(seq_rows, width)

    gated = _silu(g_ref[...]) * _head_norm(both_ref[...], gm_ref[...])
    o_ref[...] = (gated[:, 0:width] + gated[:, width:]).astype(BF16)


def _retention(rq, rk, rv, gates, group_mean, tables, state, new_state, mix, layer, batch, seq):
    has_state = state is not None
    n_chunks = seq // RET_CHUNK
    seq_blk = lambda width: pl.BlockSpec((seq, width), lambda b: (b, 0))
    whole = lambda a: pl.BlockSpec(a.shape, lambda b: (0,) * a.ndim)
    in_specs = ([seq_blk(RET_WIDTH)] * 3 + [seq_blk(2 * RET_WIDTH), whole(group_mean)] + [whole(t) for t in tables]
                + [pl.BlockSpec(memory_space=pl.ANY)])
    args = [rq, rk, rv, gates, group_mean] + list(tables) + [mix]
    aliases = {len(args) - 1: 0}
    out_shape = [jax.ShapeDtypeStruct(mix.shape, mix.dtype)]
    out_specs = [pl.BlockSpec((seq, RET_WIDTH), lambda b: (b, MIX_OFF_RET // RET_WIDTH))]
    if has_state:
        in_specs.append(pl.BlockSpec((None, None, 2, RET_HEADS, RET_DK, RET_DV), lambda b: (b, layer, 0, 0, 0, 0)))
        args.append(state)
    else:
        out_shape.append(jax.ShapeDtypeStruct((batch, DEPTH, 2, RET_HEADS * RET_DK, RET_DV), F32))
        out_specs.append(pl.BlockSpec((None, None, 2, RET_HEADS * RET_DK, RET_DV), lambda b: (b, layer, 0, 0, 0)))
        if new_state is not None:
            aliases[len(args)] = 1
            in_specs.append(pl.BlockSpec(memory_space=pl.ANY))
            args.append(new_state)
    return pl.pallas_call(
        functools.partial(_retention_kernel, has_state=has_state, n_chunks=n_chunks),
        out_shape=out_shape,
        grid=(batch,),
        in_specs=in_specs,
        out_specs=out_specs,
        input_output_aliases=aliases,
        scratch_shapes=[pltpu.VMEM((seq, 2 * RET_WIDTH), F32),
                        pltpu.VMEM((n_chunks, 2 * RET_WIDTH, RET_WIDTH), F32),
                        pltpu.VMEM((2, n_chunks, RET_WIDTH, RET_WIDTH), BF16)],
        compiler_params=_params("parallel"),
        name="retention_latent" if has_state else "retention_context",
    )(*args)


def _shift_rows(u, seq):
    rows = u.shape[0]
    pos = lax.broadcasted_iota(jnp.int32, u.shape, 0) % seq
    prev = jnp.where(pos == 0, 0.0, pltpu.roll(u, 1, 0))
    nxt = jnp.where(pos == seq - 1, 0.0, pltpu.roll(u, rows - 1, 0))
    return prev, nxt


def _mix_ffn_kernel(*refs, seq, halo, final):
    if halo:
        (x_ref, x_top_ref, x_bot_ref, mix_ref, mix_top_ref, mix_bot_ref,
         mod_ref, wout_ref, n2_ref, wup_ref, cw_ref, cb_ref, wd_ref, fin_ref, o_ref) = refs
    else:
        x_ref, mix_ref, mod_ref, wout_ref, n2_ref, wup_ref, cw_ref, cb_ref, wd_ref, fin_ref, o_ref = refs
    rows = x_ref.shape[0]
    modulate = lambda x: _rms_rows(x, n2_ref[...]) * (1.0 + mod_ref[4:5, :]) + mod_ref[3:4, :]
    if halo:
        pad = x_top_ref.shape[0]
        skip = mix_top_ref.shape[0] - pad
        tiles_per_seq = seq // rows
        place = pl.program_id(0) % tiles_per_seq
        mixed = _dot(jnp.concatenate([mix_top_ref[...], mix_ref[...], mix_bot_ref[...]], axis=0), wout_ref[...])
        x_ext = jnp.concatenate([x_top_ref[...], x_ref[...], x_bot_ref[...]], axis=0)
        x1_ext = x_ext + mod_ref[2:3, :] * mixed[skip:skip + pad + rows + pad, :]
        x1 = x1_ext[pad:pad + rows, :]
        h2 = modulate(x1_ext)
        row = lax.broadcasted_iota(jnp.int32, h2.shape, 0)
        first_kept = jnp.where(place == 0, pad, 0)
        end_kept = jnp.where(place == tiles_per_seq - 1, pad + rows, pad + rows + pad)
        h2 = jnp.where((row >= first_kept) & (row < end_kept), h2, 0.0)
        u = _dot(h2.astype(BF16), wup_ref[...])
        prev, nxt = pltpu.roll(u, 1, 0), pltpu.roll(u, u.shape[0] - 1, 0)
    else:
        x1 = x_ref[...] + mod_ref[2:3, :] * _dot(mix_ref[...], wout_ref[...])
        u = _dot(modulate(x1).astype(BF16), wup_ref[...])
        prev, nxt = _shift_rows(u, seq)
    u = prev * cw_ref[0:1, :] + u * cw_ref[1:2, :] + nxt * cw_ref[2:3, :] + cb_ref[...]
    if halo:
        u = u[pad:pad + rows, :]
    act = _silu(u[:, 0:D_FF]) * u[:, D_FF:]
    y = x1 + mod_ref[5:6, :] * _dot(act.astype(BF16), wd_ref[...])
    o_ref[...] = _rms_rows(y, fin_ref[...]) if final else y


def _mix_ffn(x, mix, mod, w_out, norm2_w, w_up, conv_w, conv_b, w_down, final_w, layer, seq, per_seq_mod, final):
    n = x.shape[0]
    rows = FFN_ROWS
    halo = seq > rows
    assert (seq % rows == 0) if halo else (rows % seq == 0)
    row_blk = lambda width: pl.BlockSpec((rows, width), lambda i: (i, 0))
    whole = lambda shape: pl.BlockSpec(shape, lambda i: (0,) * len(shape))
    resident = lambda r, c: pl.BlockSpec((None, r, c), lambda i: (layer, 0, 0), pipeline_mode=pl.Buffered(1))
    mod_idx = (lambda i: (i * rows // seq, 0, 0)) if per_seq_mod else (lambda i: (0, 0, 0))

    def with_halo(a, pad):
        width = a.shape[1]
        per_tile, last = rows // pad, n // pad - 1
        return ([row_blk(width),
                 pl.BlockSpec((pad, width), lambda i: (jnp.maximum(i * per_tile - 1, 0), 0)),
                 pl.BlockSpec((pad, width), lambda i: (jnp.minimum((i + 1) * per_tile, last), 0))], [a, a, a])

    if halo:
        x_specs, x_args = with_halo(x, F32_SUBLANES)
        mix_specs, mix_args = with_halo(mix, BF16_SUBLANES)
    else:
        x_specs, x_args, mix_specs, mix_args = [row_blk(D_MODEL)], [x], [row_blk(MIX_WIDTH)], [mix]
    in_specs = x_specs + mix_specs + [
        pl.BlockSpec((None, N_MOD, D_MODEL), mod_idx), resident(MIX_WIDTH, D_MODEL), whole((1, D_MODEL)),
        resident(D_MODEL, 2 * D_FF), whole((3, 2 * D_FF)), whole((1, 2 * D_FF)),
        resident(D_FF, D_MODEL), whole((1, D_MODEL))]
    args = x_args + mix_args + [mod, w_out, norm2_w, w_up, conv_w, conv_b, w_down, final_w]
    return pl.pallas_call(
        functools.partial(_mix_ffn_kernel, seq=seq, halo=halo, final=final),
        out_shape=jax.ShapeDtypeStruct((n, D_MODEL), F32),
        grid=(n // rows,),
        in_specs=in_specs,
        out_specs=row_blk(D_MODEL),
        compiler_params=_params("parallel"),
        name="mix_ffn_seq%d" % seq,
    )(*args)


def _layer(x, mod, lp, consts, layer, batch, seq, ctx, new_ctx, final):
    latent = ctx is not None
    outs = _in_projection(x, mod, lp["norm1_w"], consts["w_in"], lp["q_norm_w"], lp["k_norm_w"], consts["group_mean"],
                          consts["dft"][seq], consts["rope"] if latent else None,
                          None if new_ctx is None else new_ctx[:2], layer, seq)
    q, k, v, rq, rk, rv, gates, mix = outs[:8]
    mix = _attention(q, k, v, ctx[:2] if latent else None, mix, layer, batch, seq)
    ret_out = _retention(rq, rk, rv, gates, consts["group_mean"], consts["ret_tables"], ctx[2] if latent else None,
                         None if new_ctx is None else new_ctx[2], mix, layer, batch, seq)
    x = _mix_ffn(x, ret_out[0], mod, consts["w_out"], lp["norm2_w"], consts["w_up"], lp["conv_w"], lp["conv_b"],
                 consts["w_down"], consts["final_w"], layer, seq, latent, final)
    if latent:
        return x, None
    return x, (outs[8], outs[9], ret_out[1])


def kernel(x_prompt, x_sample, c, cache_attn_k, cache_attn_v, state_ret, c_ctx, w_mod, b_mod, norm1_w, w_in,
           q_norm_w, k_norm_w, w_out, norm2_w, w_up, conv_w, conv_b, w_down, final_norm_w):
    batch, seq, d = x_prompt.shape
    dec_batch, dec_seq, _ = x_sample.shape
    past = cache_attn_k.shape[2]
    assert d == D_MODEL and w_in.shape == (DEPTH, D_MODEL, IN_WIDTH) and w_up.shape == (DEPTH, D_MODEL, 2 * D_FF)
    assert (batch * seq) % FFN_ROWS == 0 and dec_seq % FFN_ROWS == 0
    assert seq % min(ATTN_Q_ROWS, seq) == 0 and dec_seq % min(ATTN_Q_ROWS, dec_seq) == 0
    assert INPROJ_ROWS % seq == 0 and dec_seq % INPROJ_ROWS == 0 and seq % RET_CHUNK == 0 and dec_seq % RET_CHUNK == 0
    assert dec_batch + 1 <= MOD_ROWS and dec_seq % GRID_W == 0
    assert HEAD_DIM == RET_DV and GROUP_WIDTH == RET_WIDTH

    consts = {
        "group_mean": _group_mean_matrix(GROUP_WIDTH, HEAD_DIM),
        "rope": _rope_tables(dec_seq),
        "ret_tables": _retention_tables(),
        "dft": {s: _dft_tables(s) for s in {seq, dec_seq}},
        "final_w": final_norm_w.reshape(1, D_MODEL),
        "w_in": w_in.astype(BF16),
        "w_out": w_out.astype(BF16),
        "w_up": w_up.astype(BF16),
        "w_down": w_down.astype(BF16),
    }

    cvec = jnp.zeros((MOD_ROWS, D_MODEL), F32).at[0].set(c_ctx).at[1:1 + dec_batch].set(c)
    mod = _modulation(cvec, w_mod, b_mod).reshape(DEPTH, MOD_ROWS, N_MOD, D_MODEL)

    cache_k = cache_attn_k.reshape(dec_batch, DEPTH, past, KV_WIDTH)
    cache_v = cache_attn_v.reshape(dec_batch, DEPTH, past, KV_WIDTH)

    xp = x_prompt.reshape(batch * seq, D_MODEL)
    xs = x_sample.reshape(dec_batch * dec_seq, D_MODEL)
    new_ctx = None
    for layer in range(DEPTH):
        lp = {
            "norm1_w": norm1_w[layer].reshape(1, D_MODEL),
            "q_norm_w": jnp.tile(q_norm_w[layer], ATTN_HEADS).reshape(1, ATTN_WIDTH),
            "k_norm_w": jnp.tile(k_norm_w[layer], ATTN_KV_HEADS).reshape(1, KV_WIDTH),
            "norm2_w": norm2_w[layer].reshape(1, D_MODEL),
            "conv_w": conv_w[layer],
            "conv_b": conv_b[layer].reshape(1, 2 * D_FF),
        }
        final = layer == DEPTH - 1
        xp, new_ctx = _layer(xp, mod[layer, 0:1], lp, consts, layer, batch, seq, None, new_ctx, final)
        xs, _ = _layer(xs, mod[layer, 1:1 + dec_batch], lp, consts, layer, dec_batch, dec_seq,
                       (cache_k, cache_v, state_ret), None, final)
    new_k, new_v, new_s = new_ctx
    return (xp.reshape(batch, seq, D_MODEL), xs.reshape(dec_batch, dec_seq, D_MODEL),
            new_k.reshape(batch, DEPTH, seq, ATTN_KV_HEADS, HEAD_DIM),
            new_v.reshape(batch, DEPTH, seq, ATTN_KV_HEADS, HEAD_DIM),
            new_s.reshape(batch, DEPTH, 2, RET_HEADS, RET_DK, RET_DV))
```

```python
import functools

import jax
import jax.numpy as jnp
import numpy as np
from jax import lax
from jax.experimental import pallas as pl
from jax.experimental.pallas import tpu as pltpu

F32 = jnp.float32
BF16 = jnp.bfloat16

D_MODEL = 1024
DEPTH = 2
GRID_W = 64
NORM_EPS = 1e-6
ATTN_HEADS = 8
ATTN_KV_HEADS = 2
HEAD_DIM = 64
ATTN_WIDTH = ATTN_HEADS * HEAD_DIM
KV_WIDTH = ATTN_KV_HEADS * HEAD_DIM
HEADS_PER_KV = ATTN_HEADS // ATTN_KV_HEADS
GROUP_WIDTH = HEADS_PER_KV * HEAD_DIM
ATTN_LOGIT_SCALE = HEAD_DIM ** -0.5 * 1.4426950408889634
ROPE_THETA = 10000.0
ROPE_AXIS_DIM = HEAD_DIM // 2
ROPE_HALF = ROPE_AXIS_DIM // 2
RET_HEADS = 4
RET_DK = 64
RET_DV = 64
RET_WIDTH = RET_HEADS * RET_DV
RET_CHUNK = 128
RET_DECAY_EXP_FWD = 5.0
RET_DECAY_EXP_BWD = 5.5
FOURIER_GROUPS = 4
FOURIER_DIM = 64
FOURIER_WIDTH = FOURIER_GROUPS * FOURIER_DIM
D_FF = 2816
N_MOD = 6

OFF_Q = 0
OFF_K = OFF_Q + ATTN_WIDTH
OFF_V = OFF_K + KV_WIDTH
OFF_RQ = OFF_V + KV_WIDTH
OFF_RK = OFF_RQ + RET_WIDTH
OFF_RV = OFF_RK + RET_WIDTH
OFF_GF = OFF_RV + RET_WIDTH
OFF_GB = OFF_GF + RET_WIDTH
OFF_FX = OFF_GB + RET_WIDTH
IN_WIDTH = OFF_FX + FOURIER_WIDTH
MIX_OFF_ATTN = 0
MIX_OFF_RET = MIX_OFF_ATTN + ATTN_WIDTH
MIX_OFF_FOURIER = MIX_OFF_RET + RET_WIDTH
MIX_WIDTH = MIX_OFF_FOURIER + FOURIER_WIDTH

V7X_VMEM_BYTES = 64 * 1024 * 1024
VMEM_LIMIT = V7X_VMEM_BYTES - 8 * 1024 * 1024

MOD_ROWS = 16
MOD_COLS = 1536
INPROJ_ROWS = 1024
ATTN_Q_ROWS = 1024
ATTN_LOGIT_ELEMS = 1024 * 1024
FFN_ROWS = 512
F32_SUBLANES = 8
BF16_SUBLANES = 16


def _params(*semantics):
    return pltpu.CompilerParams(dimension_semantics=semantics, vmem_limit_bytes=VMEM_LIMIT)


def _dot(a, b):
    return jnp.dot(a, b, preferred_element_type=F32)


def _group_mean(x, gm):
    xb = x.astype(BF16)
    width, slab = x.shape[1], gm.shape[0]
    if width <= slab:
        return _dot(xb, gm[0:width, 0:width])
    return jnp.concatenate([_dot(xb[:, s:s + slab], gm) for s in range(0, width, slab)], axis=1)


def _dot_nt(a, b):
    return lax.dot_general(a, b, (((1,), (1,)), ((), ())), preferred_element_type=F32)


def _sigmoid(x):
    return 1.0 / (1.0 + jnp.exp(-x))


def _silu(x):
    return x * _sigmoid(x)


def _rms_rows(x, w):
    ms = jnp.mean(x * x, axis=-1, keepdims=True)
    return x * lax.rsqrt(ms + NORM_EPS) * w


def _lane_block(shape, width):
    return lax.broadcasted_iota(jnp.int32, shape, len(shape) - 1) // width


def _group_mean_matrix(width, group):
    idx = np.arange(width) // group
    return jnp.asarray((idx[:, None] == idx[None, :]).astype(np.float32) / group, dtype=BF16)


def _rope_tables(n_tokens):
    pos = np.arange(n_tokens)
    row = (pos // GRID_W).astype(np.float64)
    col = (pos % GRID_W).astype(np.float64)
    freqs = ROPE_THETA ** (-np.arange(ROPE_HALF, dtype=np.float64) / ROPE_HALF)
    d = np.arange(HEAD_DIM)
    coord = np.where((d // ROPE_AXIS_DIM)[None, :] == 0, row[:, None], col[:, None])
    ang = coord * freqs[d % ROPE_HALF][None, :]
    sign = np.where((d & ROPE_HALF) == 0, -1.0, 1.0)[None, :]
    cos = np.tile(np.cos(ang), (1, ATTN_HEADS))
    sin = np.tile(np.sin(ang) * sign, (1, ATTN_HEADS))
    return jnp.asarray(cos, F32), jnp.asarray(sin, F32)


def _dft_tables(seq):
    n = np.arange(seq)
    ang = 2.0 * np.pi * ((n[:, None] * n[None, :]) % seq) / seq
    scale = 1.0 / np.sqrt(seq * FOURIER_DIM)
    position = np.concatenate([np.cos(ang), -np.sin(ang)], axis=1)
    c = np.arange(FOURIER_WIDTH)
    same = (c[:, None] // FOURIER_DIM) == (c[None, :] // FOURIER_DIM)
    angc = 2.0 * np.pi * (((c % FOURIER_DIM)[:, None] * (c % FOURIER_DIM)[None, :]) % FOURIER_DIM) / FOURIER_DIM
    channel = np.concatenate([np.where(same, np.cos(angc), 0.0), np.where(same, np.sin(angc), 0.0)], axis=1)
    as_bf16 = lambda a: jnp.asarray(a, F32).astype(BF16)
    return as_bf16(channel), as_bf16(position * scale)


def _retention_tables():
    heads = jnp.arange(RET_HEADS, dtype=F32)
    idx = jnp.arange(RET_CHUNK, dtype=F32)
    diff = idx[:, None] - idx[None, :]
    out = []
    for exp0, backward in ((RET_DECAY_EXP_FWD, False), (RET_DECAY_EXP_BWD, True)):
        lg = jnp.log1p(-jnp.exp2(-(exp0 + heads)))
        dd = -diff if backward else diff
        inner = jnp.where(dd[None] >= 0, jnp.exp(jnp.maximum(dd, 0.0)[None] * lg[:, None, None]), 0.0)
        inner = inner.transpose(1, 0, 2).reshape(RET_CHUNK, RET_HEADS * RET_CHUNK)
        q_pow = (RET_CHUNK - idx) if backward else (idx + 1.0)
        k_pow = idx if backward else (RET_CHUNK - 1.0 - idx)
        spread = lambda p: jnp.repeat(jnp.exp(p[:, None] * lg[None, :]), RET_DK, axis=1)
        q_decay, k_decay = spread(q_pow), spread(k_pow)
        chunk_decay = jnp.repeat(jnp.exp(RET_CHUNK * lg), RET_DV)[None, :]
        out.append((inner, q_decay, k_decay, chunk_decay))
    (inner_f, qd_f, kd_f, cd_f), (inner_b, qd_b, kd_b, cd_b) = out
    return (inner_f, inner_b, jnp.concatenate([qd_f, qd_b], axis=1), jnp.concatenate([kd_f, kd_b], axis=1), cd_f, cd_b)


def _mod_kernel(c_ref, w_ref, b_ref, o_ref):
    act = _silu(c_ref[...]).astype(BF16)
    o_ref[...] = _dot(act, w_ref[...].astype(BF16)) + b_ref[...]


def _modulation(cvec, w_mod, b_mod):
    n_cols = w_mod.shape[-1]
    return pl.pallas_call(
        _mod_kernel,
        out_shape=jax.ShapeDtypeStruct((DEPTH, MOD_ROWS, n_cols), F32),
        grid=(DEPTH, n_cols // MOD_COLS),
        in_specs=[
            pl.BlockSpec((MOD_ROWS, D_MODEL), lambda l, j: (0, 0)),
            pl.BlockSpec((None, D_MODEL, MOD_COLS), lambda l, j: (l, 0, j)),
            pl.BlockSpec((None, 1, MOD_COLS), lambda l, j: (l, 0, j)),
        ],
        out_specs=pl.BlockSpec((None, MOD_ROWS, MOD_COLS), lambda l, j: (l, 0, j)),
        compiler_params=_params("parallel", "parallel"),
        name="modulation",
    )(cvec, w_mod, b_mod.reshape(DEPTH, 1, n_cols))


def _swap_rotary_pairs(x):
    width = x.shape[-1]
    lane = lax.broadcasted_iota(jnp.int32, x.shape, 1)
    from_below = pltpu.roll(x, ROPE_HALF, 1)
    from_above = pltpu.roll(x, width - ROPE_HALF, 1)
    return jnp.where((lane & ROPE_HALF) != 0, from_below, from_above)


def _inproj_kernel(*refs, latent):
    x_ref, mod_ref, n1_ref, win_ref, qnw_ref, knw_ref, gm_ref, dft_c_ref, dft_p_ref = refs[:9]
    refs = refs[9:]
    if latent:
        cos_ref, sin_ref = refs[:2]
        refs = refs[2:]
        q_ref, k_ref, v_ref, rq_ref, rk_ref, rv_ref, g_ref, four_ref = refs
    else:
        q_ref, k_ref, v_ref, rq_ref, rk_ref, rv_ref, g_ref, four_ref, k32_ref, v32_ref = refs[-10:]

    shift, scale = mod_ref[0:1, :], mod_ref[1:2, :]
    h = _rms_rows(x_ref[...], n1_ref[...]) * (1.0 + scale) + shift
    hb = h.astype(BF16)
    along = _dot(_dot(hb, win_ref[:, OFF_FX:]).astype(BF16), dft_c_ref[...])
    seq = dft_p_ref.shape[0]
    for start in range(0, x_ref.shape[0], seq):
        own = along[start:start + seq, :]
        stacked = jnp.concatenate([own[:, 0:FOURIER_WIDTH], own[:, FOURIER_WIDTH:]], axis=0).astype(BF16)
        four_ref[start:start + seq, :] = _dot(dft_p_ref[...], stacked).astype(BF16)

    all_proj = _dot(hb, win_ref[:, 0:OFF_FX])
    proj = lambda off, width: all_proj[:, off:off + width]

    q = proj(OFF_Q, ATTN_WIDTH)
    q = q * lax.rsqrt(_group_mean(q * q, gm_ref[...]) + NORM_EPS) * qnw_ref[...]
    k = proj(OFF_K, KV_WIDTH)
    k = k * lax.rsqrt(_group_mean(k * k, gm_ref[...]) + NORM_EPS) * knw_ref[...]
    v = proj(OFF_V, KV_WIDTH)
    if latent:
        cos, sin = cos_ref[...], sin_ref[...]
        q = q * cos + _swap_rotary_pairs(q) * sin
        k = k * cos[:, 0:KV_WIDTH] + _swap_rotary_pairs(k) * sin[:, 0:KV_WIDTH]
    else:
        k32_ref[...] = k.reshape(k32_ref.shape)
        v32_ref[...] = v.reshape(v32_ref.shape)
    q_ref[...] = (q * ATTN_LOGIT_SCALE).astype(BF16)
    k_ref[...] = k.astype(BF16)
    v_ref[...] = v.astype(BF16)
    rq_ref[...] = proj(OFF_RQ, RET_WIDTH).astype(BF16)
    rk_ref[...] = (proj(OFF_RK, RET_WIDTH) * RET_DK ** -0.5).astype(BF16)
    rv_ref[...] = proj(OFF_RV, RET_WIDTH).astype(BF16)
    g_ref[...] = proj(OFF_GF, 2 * RET_WIDTH)


def _in_projection(x, mod, norm1_w, w_in, q_norm_w, k_norm_w, group_mean, dft, rope, new_cache, layer, seq):
    n = x.shape[0]
    rows = INPROJ_ROWS
    latent = rope is not None
    row_blk = lambda width, col=0: pl.BlockSpec((rows, width), lambda i: (i, col))
    whole = lambda shape: pl.BlockSpec(shape, lambda i: (0,) * len(shape))
    once = lambda shape: pl.BlockSpec(shape, lambda i: (0,) * len(shape), pipeline_mode=pl.Buffered(1))
    mod_idx = (lambda i: (i * rows // seq, 0, 0)) if latent else (lambda i: (0, 0, 0))
    in_specs = [
        row_blk(D_MODEL),
        pl.BlockSpec((None, N_MOD, D_MODEL), mod_idx),
        whole((1, D_MODEL)),
        pl.BlockSpec((None, D_MODEL, IN_WIDTH), lambda i: (layer, 0, 0), pipeline_mode=pl.Buffered(1)),
        whole((1, ATTN_WIDTH)),
        whole((1, KV_WIDTH)),
        whole(group_mean.shape),
        once(dft[0].shape),
        once(dft[1].shape),
    ]
    args = [x, mod, norm1_w, w_in, q_norm_w, k_norm_w, group_mean, dft[0], dft[1]]
    outs = [(ATTN_WIDTH, BF16), (KV_WIDTH, BF16), (KV_WIDTH, BF16), (RET_WIDTH, BF16), (RET_WIDTH, BF16),
            (RET_WIDTH, BF16), (2 * RET_WIDTH, F32)]
    out_shape = [jax.ShapeDtypeStruct((n, w), dt) for w, dt in outs] + [jax.ShapeDtypeStruct((n, MIX_WIDTH), BF16)]
    out_specs = [row_blk(w) for w, _ in outs] + [row_blk(FOURIER_WIDTH, MIX_OFF_FOURIER // FOURIER_WIDTH)]
    aliases = {}
    if latent:
        pos_blk = pl.BlockSpec((rows, ATTN_WIDTH), lambda i: (i % (seq // rows), 0))
        in_specs += [pos_blk, pos_blk]
        args += list(rope)
    else:
        seqs = rows // seq
        out_shape += [jax.ShapeDtypeStruct((n // seq, DEPTH, seq, KV_WIDTH), F32)] * 2
        out_specs += [pl.BlockSpec((seqs, None, seq, KV_WIDTH), lambda i: (i, layer, 0, 0))] * 2
        if new_cache is not None:
            for j, earlier in enumerate(new_cache):
                aliases[len(args)] = len(out_shape) - 2 + j
                in_specs.append(pl.BlockSpec(memory_space=pl.ANY))
                args.append(earlier)
    return pl.pallas_call(
        functools.partial(_inproj_kernel, latent=latent),
        out_shape=out_shape,
        grid=(n // rows,),
        in_specs=in_specs,
        out_specs=out_specs,
        input_output_aliases=aliases,
        compiler_params=_params("parallel"),
        name="in_projection_latent" if latent else "in_projection_context",
    )(*args)


def _spread_kv(x, group):
    lane = lax.broadcasted_iota(jnp.int32, x.shape, 1)
    other = pltpu.roll(x, HEAD_DIM, 1)
    own = (lane // HEAD_DIM) == group
    pair = jnp.where(own, x, other).astype(BF16)
    return jnp.concatenate([pair, pair], axis=1)


def _attention_kernel(*refs, past, heads_per_dot):
    if past:
        q_ref, k_ref, v_ref, ck_ref, cv_ref, _, o_ref, kt_ref, vt_ref = refs
    else:
        q_ref, k_ref, v_ref, _, o_ref, kt_ref, vt_ref = refs

    @pl.when(pl.program_id(1) == 0)
    def _():
        for g in range(ATTN_KV_HEADS):
            if past:
                kt_ref[g, 0:past, :] = _spread_kv(ck_ref[...], g)
                vt_ref[g, 0:past, :] = _spread_kv(cv_ref[...], g)
            kt_ref[g, past:, :] = _spread_kv(k_ref[...].astype(F32), g)
            vt_ref[g, past:, :] = _spread_kv(v_ref[...].astype(F32), g)

    rows = q_ref.shape[0]
    block = _lane_block((rows, GROUP_WIDTH), HEAD_DIM)

    def logits(unit):
        g, first = divmod(unit * heads_per_dot, HEADS_PER_KV)
        qg = q_ref[:, g * GROUP_WIDTH:(g + 1) * GROUP_WIDTH]
        zero = jnp.zeros_like(qg)
        stacked = jnp.concatenate([jnp.where(block == first + j, qg, zero) for j in range(heads_per_dot)], axis=0)
        return _dot_nt(stacked, kt_ref[g])

    n_units = ATTN_HEADS // heads_per_dot
    s_next = logits(0)
    out = None
    for unit in range(n_units):
        g, first = divmod(unit * heads_per_dot, HEADS_PER_KV)
        s = s_next
        if unit + 1 < n_units:
            s_next = logits(unit + 1)
        p = jnp.exp2(s - jnp.max(s, axis=-1, keepdims=True))
        denom = jnp.sum(p, axis=-1, keepdims=True)
        o = _dot(p.astype(BF16), vt_ref[g]) / denom
        for j in range(heads_per_dot):
            h = first + j
            piece = o[j * rows:(j + 1) * rows, :]
            out = piece if h == 0 else jnp.where(block == h, piece, out)
        if first + heads_per_dot == HEADS_PER_KV:
            o_ref[:, g * GROUP_WIDTH:(g + 1) * GROUP_WIDTH] = out.astype(BF16)


def _attention(q, k, v, cache, mix, layer, batch, seq):
    tq = min(ATTN_Q_ROWS, seq)
    nq = seq // tq
    past = 0 if cache is None else cache[0].shape[2]
    heads_per_dot = max(1, min(HEADS_PER_KV, ATTN_LOGIT_ELEMS // (tq * (past + seq))))
    assert HEADS_PER_KV % heads_per_dot == 0
    own_kv = pl.BlockSpec((seq, KV_WIDTH), lambda b, i: (b, 0))
    in_specs = [pl.BlockSpec((tq, ATTN_WIDTH), lambda b, i: (b * nq + i, 0)), own_kv, own_kv]
    args = [q, k, v]
    if past:
        cached_kv = pl.BlockSpec((None, None, past, KV_WIDTH), lambda b, i: (b, layer, 0, 0))
        in_specs += [cached_kv, cached_kv]
        args += list(cache)
    in_specs.append(pl.BlockSpec(memory_space=pl.ANY))
    args.append(mix)
    return pl.pallas_call(
        functools.partial(_attention_kernel, past=past, heads_per_dot=heads_per_dot),
        out_shape=jax.ShapeDtypeStruct(mix.shape, mix.dtype),
        grid=(batch, nq),
        in_specs=in_specs,
        out_specs=pl.BlockSpec((tq, ATTN_WIDTH), lambda b, i: (b * nq + i, MIX_OFF_ATTN // ATTN_WIDTH)),
        input_output_aliases={len(args) - 1: 0},
        scratch_shapes=[pltpu.VMEM((ATTN_KV_HEADS, past + seq, GROUP_WIDTH), BF16)] * 2,
        compiler_params=_params("parallel", "arbitrary"),
        name="attention_latent" if past else "attention_context",
    )(*args)


def _stack_heads(x):
    block = _lane_block(x.shape, RET_DK)
    zero = jnp.zeros_like(x)
    return jnp.concatenate([jnp.where(block == h, x, zero) for h in range(RET_HEADS)], axis=-2)


def _head_norm(o, gm):
    d = o - _group_mean(o, gm)
    return d * lax.rsqrt(_group_mean(d * d, gm) + NORM_EPS)


def _retention_kernel(*refs, has_state, n_chunks):
    rq_ref, rk_ref, rv_ref, g_ref, gm_ref, df_ref, db_ref, qd_ref, kd_ref, cdf_ref, cdb_ref = refs[:11]
    refs = refs[12:]
    if has_state:
        s0_ref, o_ref, both_ref, upd_ref, st_ref = refs
    else:
        o_ref, sfin_ref, both_ref, upd_ref, st_ref = refs[-5:]

    width = RET_WIDTH
    diag = (lax.broadcasted_iota(jnp.int32, (width, width), 0) // RET_DK
            == lax.broadcasted_iota(jnp.int32, (width, width), 1) // RET_DV)

    chunked = lambda ref: ref[...].reshape(n_chunks, RET_CHUNK, width)
    batched = lambda a, b, contract: lax.dot_general(a, b, (contract, ((0,), (0,))), preferred_element_type=F32)
    q3, k3, v3 = chunked(rq_ref), chunked(rk_ref), chunked(rv_ref)
    scores = batched(q3, _stack_heads(k3), ((2,), (2,)))
    v_heads = _stack_heads(v3)
    seq_rows = n_chunks * RET_CHUNK
    both_ref[:, 0:width] = batched((scores * df_ref[...]).astype(BF16), v_heads, ((2,), (1,))).reshape(seq_rows, width)
    both_ref[:, width:] = batched((scores * db_ref[...]).astype(BF16), v_heads, ((2,), (1,))).reshape(seq_rows, width)
    k32 = k3.astype(F32)
    k_decayed = jnp.concatenate([k32, k32], axis=2) * kd_ref[...]
    upd_ref[...] = batched(jnp.swapaxes(k_decayed, 1, 2).astype(BF16), v3, ((2,), (1,)))

    for direction, order, cd_ref in ((0, range(n_chunks), cdf_ref), (1, reversed(range(n_chunks)), cdb_ref)):
        if has_state:
            s = s0_ref[direction].reshape(width, RET_DV)
            state = jnp.where(diag, jnp.concatenate([s] * RET_HEADS, axis=1), 0.0)
        else:
            state = jnp.zeros((width, width), F32)
        for c in order:
            st_ref[direction, c] = state.astype(BF16)
            update = upd_ref[c, direction * width:(direction + 1) * width, :]
            state = cd_ref[...] * state + jnp.where(diag, update, 0.0)
        if not has_state:
            folded = state[:, 0:width // 2] + state[:, width // 2:]
            sfin_ref[direction] = folded[:, 0:RET_DV] + folded[:, RET_DV:]

    q32 = q3.astype(F32)
    q_decayed = (jnp.concatenate([q32, q32], axis=2) * qd_ref[...]).astype(BF16)
    both_ref[:, 0:width] += batched(q_decayed[:, :, 0:width], st_ref[0], ((2,), (1,))).reshape(seq_rows, width)
    both_ref[:, width:] += batched(q_decayed[:, :, width:], st_ref[1], ((2,), (1,))).reshape(seq_rows, width)

    gated = _silu(g_ref[...]) * _head_norm(both_ref[...], gm_ref[...])
    o_ref[...] = (gated[:, 0:width] + gated[:, width:]).astype(BF16)


def _retention(rq, rk, rv, gates, group_mean, tables, state, new_state, mix, layer, batch, seq):
    has_state = state is not None
    n_chunks = seq // RET_CHUNK
    seq_blk = lambda width: pl.BlockSpec((seq, width), lambda b: (b, 0))
    whole = lambda a: pl.BlockSpec(a.shape, lambda b: (0,) * a.ndim)
    in_specs = ([seq_blk(RET_WIDTH)] * 3 + [seq_blk(2 * RET_WIDTH), whole(group_mean)] + [whole(t) for t in tables]
                + [pl.BlockSpec(memory_space=pl.ANY)])
    args = [rq, rk, rv, gates, group_mean] + list(tables) + [mix]
    aliases = {len(args) - 1: 0}
    out_shape = [jax.ShapeDtypeStruct(mix.shape, mix.dtype)]
    out_specs = [pl.BlockSpec((seq, RET_WIDTH), lambda b: (b, MIX_OFF_RET // RET_WIDTH))]
    if has_state:
        in_specs.append(pl.BlockSpec((None, None, 2, RET_HEADS, RET_DK, RET_DV), lambda b: (b, layer, 0, 0, 0, 0)))
        args.append(state)
    else:
        out_shape.append(jax.ShapeDtypeStruct((batch, DEPTH, 2, RET_HEADS * RET_DK, RET_DV), F32))
        out_specs.append(pl.BlockSpec((None, None, 2, RET_HEADS * RET_DK, RET_DV), lambda b: (b, layer, 0, 0, 0)))
        if new_state is not None:
            aliases[len(args)] = 1
            in_specs.append(pl.BlockSpec(memory_space=pl.ANY))
            args.append(new_state)
    return pl.pallas_call(
        functools.partial(_retention_kernel, has_state=has_state, n_chunks=n_chunks),
        out_shape=out_shape,
        grid=(batch,),
        in_specs=in_specs,
        out_specs=out_specs,
        input_output_aliases=aliases,
        scratch_shapes=[pltpu.VMEM((seq, 2 * RET_WIDTH), F32),
                        pltpu.VMEM((n_chunks, 2 * RET_WIDTH, RET_WIDTH), F32),
                        pltpu.VMEM((2, n_chunks, RET_WIDTH, RET_WIDTH), BF16)],
        compiler_params=_params("parallel"),
        name="retention_latent" if has_state else "retention_context",
    )(*args)


def _shift_rows(u, seq):
    rows = u.shape[0]
    pos = lax.broadcasted_iota(jnp.int32, u.shape, 0) % seq
    prev = jnp.where(pos == 0, 0.0, pltpu.roll(u, 1, 0))
    nxt = jnp.where(pos == seq - 1, 0.0, pltpu.roll(u, rows - 1, 0))
    return prev, nxt


def _mix_ffn_kernel(*refs, seq, halo, final):
    if halo:
        (x_ref, x_top_ref, x_bot_ref, mix_ref, mix_top_ref, mix_bot_ref,
         mod_ref, wout_ref, n2_ref, wup_ref, cw_ref, cb_ref, wd_ref, fin_ref, o_ref) = refs
    else:
        x_ref, mix_ref, mod_ref, wout_ref, n2_ref, wup_ref, cw_ref, cb_ref, wd_ref, fin_ref, o_ref = refs
    rows = x_ref.shape[0]
    modulate = lambda x: _rms_rows(x, n2_ref[...]) * (1.0 + mod_ref[4:5, :]) + mod_ref[3:4, :]
    if halo:
        pad = x_top_ref.shape[0]
        skip = mix_top_ref.shape[0] - pad
        tiles_per_seq = seq // rows
        place = pl.program_id(0) % tiles_per_seq
        mixed = _dot(jnp.concatenate([mix_top_ref[...], mix_ref[...], mix_bot_ref[...]], axis=0), wout_ref[...])
        x_ext = jnp.concatenate([x_top_ref[...], x_ref[...], x_bot_ref[...]], axis=0)
        x1_ext = x_ext + mod_ref[2:3, :] * mixed[skip:skip + pad + rows + pad, :]
        x1 = x1_ext[pad:pad + rows, :]
        h2 = modulate(x1_ext)
        row = lax.broadcasted_iota(jnp.int32, h2.shape, 0)
        first_kept = jnp.where(place == 0, pad, 0)
        end_kept = jnp.where(place == tiles_per_seq - 1, pad + rows, pad + rows + pad)
        h2 = jnp.where((row >= first_kept) & (row < end_kept), h2, 0.0)
        u = _dot(h2.astype(BF16), wup_ref[...])
        prev, nxt = pltpu.roll(u, 1, 0), pltpu.roll(u, u.shape[0] - 1, 0)
    else:
        x1 = x_ref[...] + mod_ref[2:3, :] * _dot(mix_ref[...], wout_ref[...])
        u = _dot(modulate(x1).astype(BF16), wup_ref[...])
        prev, nxt = _shift_rows(u, seq)
    u = prev * cw_ref[0:1, :] + u * cw_ref[1:2, :] + nxt * cw_ref[2:3, :] + cb_ref[...]
    if halo:
        u = u[pad:pad + rows, :]
    act = _silu(u[:, 0:D_FF]) * u[:, D_FF:]
    y = x1 + mod_ref[5:6, :] * _dot(act.astype(BF16), wd_ref[...])
    o_ref[...] = _rms_rows(y, fin_ref[...]) if final else y


def _mix_ffn(x, mix, mod, w_out, norm2_w, w_up, conv_w, conv_b, w_down, final_w, layer, seq, per_seq_mod, final):
    n = x.shape[0]
    rows = FFN_ROWS
    halo = seq > rows
    assert (seq % rows == 0) if halo else (rows % seq == 0)
    row_blk = lambda width: pl.BlockSpec((rows, width), lambda i: (i, 0))
    whole = lambda shape: pl.BlockSpec(shape, lambda i: (0,) * len(shape))
    resident = lambda r, c: pl.BlockSpec((None, r, c), lambda i: (layer, 0, 0), pipeline_mode=pl.Buffered(1))
    mod_idx = (lambda i: (i * rows // seq, 0, 0)) if per_seq_mod else (lambda i: (0, 0, 0))

    def with_halo(a, pad):
        width = a.shape[1]
        per_tile, last = rows // pad, n // pad - 1
        return ([row_blk(width),
                 pl.BlockSpec((pad, width), lambda i: (jnp.maximum(i * per_tile - 1, 0), 0)),
                 pl.BlockSpec((pad, width), lambda i: (jnp.minimum((i + 1) * per_tile, last), 0))], [a, a, a])

    if halo:
        x_specs, x_args = with_halo(x, F32_SUBLANES)
        mix_specs, mix_args = with_halo(mix, BF16_SUBLANES)
    else:
        x_specs, x_args, mix_specs, mix_args = [row_blk(D_MODEL)], [x], [row_blk(MIX_WIDTH)], [mix]
    in_specs = x_specs + mix_specs + [
        pl.BlockSpec((None, N_MOD, D_MODEL), mod_idx), resident(MIX_WIDTH, D_MODEL), whole((1, D_MODEL)),
        resident(D_MODEL, 2 * D_FF), whole((3, 2 * D_FF)), whole((1, 2 * D_FF)),
        resident(D_FF, D_MODEL), whole((1, D_MODEL))]
    args = x_args + mix_args + [mod, w_out, norm2_w, w_up, conv_w, conv_b, w_down, final_w]
    return pl.pallas_call(
        functools.partial(_mix_ffn_kernel, seq=seq, halo=halo, final=final),
        out_shape=jax.ShapeDtypeStruct((n, D_MODEL), F32),
        grid=(n // rows,),
        in_specs=in_specs,
        out_specs=row_blk(D_MODEL),
        compiler_params=_params("parallel"),
        name="mix_ffn_seq%d" % seq,
    )(*args)


def _layer(x, mod, lp, consts, layer, batch, seq, ctx, new_ctx, final):
    latent = ctx is not None
    outs = _in_projection(x, mod, lp["norm1_w"], consts["w_in"], lp["q_norm_w"], lp["k_norm_w"], consts["group_mean"],
                          consts["dft"][seq], consts["rope"] if latent else None,
                          None if new_ctx is None else new_ctx[:2], layer, seq)
    q, k, v, rq, rk, rv, gates, mix = outs[:8]
    mix = _attention(q, k, v, ctx[:2] if latent else None, mix, layer, batch, seq)
    ret_out = _retention(rq, rk, rv, gates, consts["group_mean"], consts["ret_tables"], ctx[2] if latent else None,
                         None if new_ctx is None else new_ctx[2], mix, layer, batch, seq)
    x = _mix_ffn(x, ret_out[0], mod, consts["w_out"], lp["norm2_w"], consts["w_up"], lp["conv_w"], lp["conv_b"],
                 consts["w_down"], consts["final_w"], layer, seq, latent, final)
    if latent:
        return x, None
    return x, (outs[8], outs[9], ret_out[1])


def kernel(x_prompt, x_sample, c, cache_attn_k, cache_attn_v, state_ret, c_ctx, w_mod, b_mod, norm1_w, w_in,
           q_norm_w, k_norm_w, w_out, norm2_w, w_up, conv_w, conv_b, w_down, final_norm_w):
    batch, seq, d = x_prompt.shape
    dec_batch, dec_seq, _ = x_sample.shape
    past = cache_attn_k.shape[2]
    assert d == D_MODEL and w_in.shape == (DEPTH, D_MODEL, IN_WIDTH) and w_up.shape == (DEPTH, D_MODEL, 2 * D_FF)
    assert (batch * seq) % FFN_ROWS == 0 and dec_seq % FFN_ROWS == 0
    assert seq % min(ATTN_Q_ROWS, seq) == 0 and dec_seq % min(ATTN_Q_ROWS, dec_seq) == 0
    assert INPROJ_ROWS % seq == 0 and dec_seq % INPROJ_ROWS == 0 and seq % RET_CHUNK == 0 and dec_seq % RET_CHUNK == 0
    assert dec_batch + 1 <= MOD_ROWS and dec_seq % GRID_W == 0
    assert HEAD_DIM == RET_DV and GROUP_WIDTH == RET_WIDTH

    consts = {
        "group_mean": _group_mean_matrix(GROUP_WIDTH, HEAD_DIM),
        "rope": _rope_tables(dec_seq),
        "ret_tables": _retention_tables(),
        "dft": {s: _dft_tables(s) for s in {seq, dec_seq}},
        "final_w": final_norm_w.reshape(1, D_MODEL),
        "w_in": w_in.astype(BF16),
        "w_out": w_out.astype(BF16),
        "w_up": w_up.astype(BF16),
        "w_down": w_down.astype(BF16),
    }

    cvec = jnp.zeros((MOD_ROWS, D_MODEL), F32).at[0].set(c_ctx).at[1:1 + dec_batch].set(c)
    mod = _modulation(cvec, w_mod, b_mod).reshape(DEPTH, MOD_ROWS, N_MOD, D_MODEL)

    cache_k = cache_attn_k.reshape(dec_batch, DEPTH, past, KV_WIDTH)
    cache_v = cache_attn_v.reshape(dec_batch, DEPTH, past, KV_WIDTH)

    xp = x_prompt.reshape(batch * seq, D_MODEL)
    xs = x_sample.reshape(dec_batch * dec_seq, D_MODEL)
    new_ctx = None
    for layer in range(DEPTH):
        lp = {
            "norm1_w": norm1_w[layer].reshape(1, D_MODEL),
            "q_norm_w": jnp.tile(q_norm_w[layer], ATTN_HEADS).reshape(1, ATTN_WIDTH),
            "k_norm_w": jnp.tile(k_norm_w[layer], ATTN_KV_HEADS).reshape(1, KV_WIDTH),
            "norm2_w": norm2_w[layer].reshape(1, D_MODEL),
            "conv_w": conv_w[layer],
            "conv_b": conv_b[layer].reshape(1, 2 * D_FF),
        }
        final = layer == DEPTH - 1
        xp, new_ctx = _layer(xp, mod[layer, 0:1], lp, consts, layer, batch, seq, None, new_ctx, final)
        xs, _ = _layer(xs, mod[layer, 1:1 + dec_batch], lp, consts, layer, dec_batch, dec_seq,
                       (cache_k, cache_v, state_ret), None, final)
    new_k, new_v, new_s = new_ctx
    return (xp.reshape(batch, seq, D_MODEL), xs.reshape(dec_batch, dec_seq, D_MODEL),
            new_k.reshape(batch, DEPTH, seq, ATTN_KV_HEADS, HEAD_DIM),
            new_v.reshape(batch, DEPTH, seq, ATTN_KV_HEADS, HEAD_DIM),
            new_s.reshape(batch, DEPTH, 2, RET_HEADS, RET_DK, RET_DV))
```

```python
import functools

import jax
import jax.numpy as jnp
import numpy as np
from jax import lax
from jax.experimental import pallas as pl
from jax.experimental.pallas import tpu as pltpu

F32 = jnp.float32
BF16 = jnp.bfloat16

D_MODEL = 1024
DEPTH = 2
GRID_W = 64
NORM_EPS = 1e-6
ATTN_HEADS = 8
ATTN_KV_HEADS = 2
HEAD_DIM = 64
ATTN_WIDTH = ATTN_HEADS * HEAD_DIM
KV_WIDTH = ATTN_KV_HEADS * HEAD_DIM
HEADS_PER_KV = ATTN_HEADS // ATTN_KV_HEADS
GROUP_WIDTH = HEADS_PER_KV * HEAD_DIM
ATTN_LOGIT_SCALE = HEAD_DIM ** -0.5 * 1.4426950408889634
ROPE_THETA = 10000.0
ROPE_AXIS_DIM = HEAD_DIM // 2
ROPE_HALF = ROPE_AXIS_DIM // 2
RET_HEADS = 4
RET_DK = 64
RET_DV = 64
RET_WIDTH = RET_HEADS * RET_DV
RET_CHUNK = 128
RET_DECAY_EXP_FWD = 5.0
RET_DECAY_EXP_BWD = 5.5
FOURIER_GROUPS = 4
FOURIER_DIM = 64
FOURIER_WIDTH = FOURIER_GROUPS * FOURIER_DIM
D_FF = 2816
N_MOD = 6

OFF_Q = 0
OFF_K = OFF_Q + ATTN_WIDTH
OFF_V = OFF_K + KV_WIDTH
OFF_RQ = OFF_V + KV_WIDTH
OFF_RK = OFF_RQ + RET_WIDTH
OFF_RV = OFF_RK + RET_WIDTH
OFF_GF = OFF_RV + RET_WIDTH
OFF_GB = OFF_GF + RET_WIDTH
OFF_FX = OFF_GB + RET_WIDTH
IN_WIDTH = OFF_FX + FOURIER_WIDTH
MIX_OFF_ATTN = 0
MIX_OFF_RET = MIX_OFF_ATTN + ATTN_WIDTH
MIX_OFF_FOURIER = MIX_OFF_RET + RET_WIDTH
MIX_WIDTH = MIX_OFF_FOURIER + FOURIER_WIDTH

V7X_VMEM_BYTES = 64 * 1024 * 1024
VMEM_LIMIT = V7X_VMEM_BYTES - 8 * 1024 * 1024

MOD_ROWS = 16
MOD_COLS = 1536
INPROJ_ROWS = 1024
ATTN_Q_ROWS = 1024
ATTN_LOGIT_ELEMS = 1024 * 1024
FFN_ROWS = 512
F32_SUBLANES = 8
BF16_SUBLANES = 16


def _params(*semantics):
    return pltpu.CompilerParams(dimension_semantics=semantics, vmem_limit_bytes=VMEM_LIMIT)


def _dot(a, b):
    return jnp.dot(a, b, preferred_element_type=F32)


def _group_mean(x, gm):
    xb = x.astype(BF16)
    width, slab = x.shape[1], gm.shape[0]
    if width <= slab:
        return _dot(xb, gm[0:width, 0:width])
    return jnp.concatenate([_dot(xb[:, s:s + slab], gm) for s in range(0, width, slab)], axis=1)


def _dot_nt(a, b):
    return lax.dot_general(a, b, (((1,), (1,)), ((), ())), preferred_element_type=F32)


def _sigmoid(x):
    return 1.0 / (1.0 + jnp.exp(-x))


def _silu(x):
    return x * _sigmoid(x)


def _rms_rows(x, w):
    ms = jnp.mean(x * x, axis=-1, keepdims=True)
    return x * lax.rsqrt(ms + NORM_EPS) * w


def _lane_block(shape, width):
    return lax.broadcasted_iota(jnp.int32, shape, len(shape) - 1) // width


def _group_mean_matrix(width, group):
    idx = np.arange(width) // group
    return jnp.asarray((idx[:, None] == idx[None, :]).astype(np.float32) / group, dtype=BF16)


def _rope_tables(n_tokens):
    pos = np.arange(n_tokens)
    row = (pos // GRID_W).astype(np.float64)
    col = (pos % GRID_W).astype(np.float64)
    freqs = ROPE_THETA ** (-np.arange(ROPE_HALF, dtype=np.float64) / ROPE_HALF)
    d = np.arange(HEAD_DIM)
    coord = np.where((d // ROPE_AXIS_DIM)[None, :] == 0, row[:, None], col[:, None])
    ang = coord * freqs[d % ROPE_HALF][None, :]
    sign = np.where((d & ROPE_HALF) == 0, -1.0, 1.0)[None, :]
    cos = np.tile(np.cos(ang), (1, ATTN_HEADS))
    sin = np.tile(np.sin(ang) * sign, (1, ATTN_HEADS))
    return jnp.asarray(cos, F32), jnp.asarray(sin, F32)


def _dft_tables(seq):
    n = np.arange(seq)
    ang = 2.0 * np.pi * ((n[:, None] * n[None, :]) % seq) / seq
    scale = 1.0 / np.sqrt(seq * FOURIER_DIM)
    position = np.concatenate([np.cos(ang), -np.sin(ang)], axis=1)
    c = np.arange(FOURIER_WIDTH)
    same = (c[:, None] // FOURIER_DIM) == (c[None, :] // FOURIER_DIM)
    angc = 2.0 * np.pi * (((c % FOURIER_DIM)[:, None] * (c % FOURIER_DIM)[None, :]) % FOURIER_DIM) / FOURIER_DIM
    channel = np.concatenate([np.where(same, np.cos(angc), 0.0), np.where(same, np.sin(angc), 0.0)], axis=1)
    as_bf16 = lambda a: jnp.asarray(a, F32).astype(BF16)
    return as_bf16(channel), as_bf16(position * scale)


def _retention_tables():
    heads = jnp.arange(RET_HEADS, dtype=F32)
    idx = jnp.arange(RET_CHUNK, dtype=F32)
    diff = idx[:, None] - idx[None, :]
    out = []
    for exp0, backward in ((RET_DECAY_EXP_FWD, False), (RET_DECAY_EXP_BWD, True)):
        lg = jnp.log1p(-jnp.exp2(-(exp0 + heads)))
        dd = -diff if backward else diff
        inner = jnp.where(dd[None] >= 0, jnp.exp(jnp.maximum(dd, 0.0)[None] * lg[:, None, None]), 0.0)
        inner = inner.transpose(1, 0, 2).reshape(RET_CHUNK, RET_HEADS * RET_CHUNK)
        q_pow = (RET_CHUNK - idx) if backward else (idx + 1.0)
        k_pow = idx if backward else (RET_CHUNK - 1.0 - idx)
        spread = lambda p: jnp.repeat(jnp.exp(p[:, None] * lg[None, :]), RET_DK, axis=1)
        q_decay, k_decay = spread(q_pow), spread(k_pow)
        chunk_decay = jnp.repeat(jnp.exp(RET_CHUNK * lg), RET_DV)[None, :]
        out.append((inner, q_decay, k_decay, chunk_decay))
    (inner_f, qd_f, kd_f, cd_f), (inner_b, qd_b, kd_b, cd_b) = out
    return (inner_f, inner_b, jnp.concatenate([qd_f, qd_b], axis=1), jnp.concatenate([kd_f, kd_b], axis=1), cd_f, cd_b)


def _mod_kernel(c_ref, w_ref, b_ref, o_ref):
    act = _silu(c_ref[...]).astype(BF16)
    o_ref[...] = _dot(act, w_ref[...].astype(BF16)) + b_ref[...]


def _modulation(cvec, w_mod, b_mod):
    n_cols = w_mod.shape[-1]
    return pl.pallas_call(
        _mod_kernel,
        out_shape=jax.ShapeDtypeStruct((DEPTH, MOD_ROWS, n_cols), F32),
        grid=(DEPTH, n_cols // MOD_COLS),
        in_specs=[
            pl.BlockSpec((MOD_ROWS, D_MODEL), lambda l, j: (0, 0)),
            pl.BlockSpec((None, D_MODEL, MOD_COLS), lambda l, j: (l, 0, j)),
            pl.BlockSpec((None, 1, MOD_COLS), lambda l, j: (l, 0, j)),
        ],
        out_specs=pl.BlockSpec((None, MOD_ROWS, MOD_COLS), lambda l, j: (l, 0, j)),
        compiler_params=_params("parallel", "parallel"),
        name="modulation",
    )(cvec, w_mod, b_mod.reshape(DEPTH, 1, n_cols))


def _swap_rotary_pairs(x):
    width = x.shape[-1]
    lane = lax.broadcasted_iota(jnp.int32, x.shape, 1)
    from_below = pltpu.roll(x, ROPE_HALF, 1)
    from_above = pltpu.roll(x, width - ROPE_HALF, 1)
    return jnp.where((lane & ROPE_HALF) != 0, from_below, from_above)


def _inproj_kernel(*refs, latent):
    x_ref, mod_ref, n1_ref, win_ref, qnw_ref, knw_ref, gm_ref = refs[:7]
    refs = refs[7:]
    if latent:
        cos_ref, sin_ref = refs[:2]
        refs = refs[2:]
        q_ref, k_ref, v_ref, rq_ref, rk_ref, rv_ref, g_ref, fx_ref = refs
    else:
        q_ref, k_ref, v_ref, rq_ref, rk_ref, rv_ref, g_ref, fx_ref, k32_ref, v32_ref = refs[-10:]

    shift, scale = mod_ref[0:1, :], mod_ref[1:2, :]
    h = _rms_rows(x_ref[...], n1_ref[...]) * (1.0 + scale) + shift
    all_proj = _dot(h.astype(BF16), win_ref[...])
    proj = lambda off, width: all_proj[:, off:off + width]

    q = proj(OFF_Q, ATTN_WIDTH)
    q = q * lax.rsqrt(_group_mean(q * q, gm_ref[...]) + NORM_EPS) * qnw_ref[...]
    k = proj(OFF_K, KV_WIDTH)
    k = k * lax.rsqrt(_group_mean(k * k, gm_ref[...]) + NORM_EPS) * knw_ref[...]
    v = proj(OFF_V, KV_WIDTH)
    if latent:
        cos, sin = cos_ref[...], sin_ref[...]
        q = q * cos + _swap_rotary_pairs(q) * sin
        k = k * cos[:, 0:KV_WIDTH] + _swap_rotary_pairs(k) * sin[:, 0:KV_WIDTH]
    else:
        k32_ref[...] = k.reshape(k32_ref.shape)
        v32_ref[...] = v.reshape(v32_ref.shape)
    q_ref[...] = (q * ATTN_LOGIT_SCALE).astype(BF16)
    k_ref[...] = k.astype(BF16)
    v_ref[...] = v.astype(BF16)
    rq_ref[...] = proj(OFF_RQ, RET_WIDTH).astype(BF16)
    rk_ref[...] = (proj(OFF_RK, RET_WIDTH) * RET_DK ** -0.5).astype(BF16)
    rv_ref[...] = proj(OFF_RV, RET_WIDTH).astype(BF16)
    g_ref[...] = proj(OFF_GF, 2 * RET_WIDTH)
    fx_ref[...] = proj(OFF_FX, FOURIER_WIDTH).astype(BF16)


def _in_projection(x, mod, norm1_w, w_in, q_norm_w, k_norm_w, group_mean, rope, new_cache, layer, seq):
    n = x.shape[0]
    rows = INPROJ_ROWS
    latent = rope is not None
    row_blk = lambda width: pl.BlockSpec((rows, width), lambda i: (i, 0))
    whole = lambda shape: pl.BlockSpec(shape, lambda i: (0,) * len(shape))
    mod_idx = (lambda i: (i * rows // seq, 0, 0)) if latent else (lambda i: (0, 0, 0))
    in_specs = [
        row_blk(D_MODEL),
        pl.BlockSpec((None, N_MOD, D_MODEL), mod_idx),
        whole((1, D_MODEL)),
        pl.BlockSpec((None, D_MODEL, IN_WIDTH), lambda i: (layer, 0, 0), pipeline_mode=pl.Buffered(1)),
        whole((1, ATTN_WIDTH)),
        whole((1, KV_WIDTH)),
        whole(group_mean.shape),
    ]
    args = [x, mod, norm1_w, w_in, q_norm_w, k_norm_w, group_mean]
    outs = [(ATTN_WIDTH, BF16), (KV_WIDTH, BF16), (KV_WIDTH, BF16), (RET_WIDTH, BF16), (RET_WIDTH, BF16),
            (RET_WIDTH, BF16), (2 * RET_WIDTH, F32), (FOURIER_WIDTH, BF16)]
    out_shape = [jax.ShapeDtypeStruct((n, w), dt) for w, dt in outs]
    out_specs = [row_blk(w) for w, _ in outs]
    aliases = {}
    if latent:
        pos_blk = pl.BlockSpec((rows, ATTN_WIDTH), lambda i: (i % (seq // rows), 0))
        in_specs += [pos_blk, pos_blk]
        args += list(rope)
    else:
        seqs = rows // seq
        out_shape += [jax.ShapeDtypeStruct((n // seq, DEPTH, seq, KV_WIDTH), F32)] * 2
        out_specs += [pl.BlockSpec((seqs, None, seq, KV_WIDTH), lambda i: (i, layer, 0, 0))] * 2
        if new_cache is not None:
            for j, earlier in enumerate(new_cache):
                aliases[len(args)] = len(out_shape) - 2 + j
                in_specs.append(pl.BlockSpec(memory_space=pl.ANY))
                args.append(earlier)
    return pl.pallas_call(
        functools.partial(_inproj_kernel, latent=latent),
        out_shape=out_shape,
        grid=(n // rows,),
        in_specs=in_specs,
        out_specs=out_specs,
        input_output_aliases=aliases,
        compiler_params=_params("parallel"),
        name="in_projection_latent" if latent else "in_projection_context",
    )(*args)


def _spread_kv(x, group):
    lane = lax.broadcasted_iota(jnp.int32, x.shape, 1)
    other = pltpu.roll(x, HEAD_DIM, 1)
    own = (lane // HEAD_DIM) == group
    pair = jnp.where(own, x, other).astype(BF16)
    return jnp.concatenate([pair, pair], axis=1)


def _attention_kernel(*refs, past, heads_per_dot):
    if past:
        q_ref, k_ref, v_ref, ck_ref, cv_ref, o_ref, kt_ref, vt_ref = refs
    else:
        q_ref, k_ref, v_ref, o_ref, kt_ref, vt_ref = refs

    @pl.when(pl.program_id(1) == 0)
    def _():
        for g in range(ATTN_KV_HEADS):
            if past:
                kt_ref[g, 0:past, :] = _spread_kv(ck_ref[...], g)
                vt_ref[g, 0:past, :] = _spread_kv(cv_ref[...], g)
            kt_ref[g, past:, :] = _spread_kv(k_ref[...].astype(F32), g)
            vt_ref[g, past:, :] = _spread_kv(v_ref[...].astype(F32), g)

    rows = q_ref.shape[0]
    block = _lane_block((rows, GROUP_WIDTH), HEAD_DIM)

    def logits(unit):
        g, first = divmod(unit * heads_per_dot, HEADS_PER_KV)
        qg = q_ref[:, g * GROUP_WIDTH:(g + 1) * GROUP_WIDTH]
        zero = jnp.zeros_like(qg)
        stacked = jnp.concatenate([jnp.where(block == first + j, qg, zero) for j in range(heads_per_dot)], axis=0)
        return _dot_nt(stacked, kt_ref[g])

    n_units = ATTN_HEADS // heads_per_dot
    s_next = logits(0)
    out = None
    for unit in range(n_units):
        g, first = divmod(unit * heads_per_dot, HEADS_PER_KV)
        s = s_next
        if unit + 1 < n_units:
            s_next = logits(unit + 1)
        p = jnp.exp2(s - jnp.max(s, axis=-1, keepdims=True))
        denom = jnp.sum(p, axis=-1, keepdims=True)
        o = _dot(p.astype(BF16), vt_ref[g]) / denom
        for j in range(heads_per_dot):
            h = first + j
            piece = o[j * rows:(j + 1) * rows, :]
            out = piece if h == 0 else jnp.where(block == h, piece, out)
        if first + heads_per_dot == HEADS_PER_KV:
            o_ref[:, g * GROUP_WIDTH:(g + 1) * GROUP_WIDTH] = out.astype(BF16)


def _attention(q, k, v, cache, layer, batch, seq):
    n = q.shape[0]
    tq = min(ATTN_Q_ROWS, seq)
    nq = seq // tq
    past = 0 if cache is None else cache[0].shape[2]
    heads_per_dot = max(1, min(HEADS_PER_KV, ATTN_LOGIT_ELEMS // (tq * (past + seq))))
    assert HEADS_PER_KV % heads_per_dot == 0
    own_kv = pl.BlockSpec((seq, KV_WIDTH), lambda b, i: (b, 0))
    in_specs = [pl.BlockSpec((tq, ATTN_WIDTH), lambda b, i: (b * nq + i, 0)), own_kv, own_kv]
    args = [q, k, v]
    if past:
        cached_kv = pl.BlockSpec((None, None, past, KV_WIDTH), lambda b, i: (b, layer, 0, 0))
        in_specs += [cached_kv, cached_kv]
        args += list(cache)
    return pl.pallas_call(
        functools.partial(_attention_kernel, past=past, heads_per_dot=heads_per_dot),
        out_shape=jax.ShapeDtypeStruct((n, MIX_WIDTH), BF16),
        grid=(batch, nq),
        in_specs=in_specs,
        out_specs=pl.BlockSpec((tq, ATTN_WIDTH), lambda b, i: (b * nq + i, MIX_OFF_ATTN // ATTN_WIDTH)),
        scratch_shapes=[pltpu.VMEM((ATTN_KV_HEADS, past + seq, GROUP_WIDTH), BF16)] * 2,
        compiler_params=_params("parallel", "arbitrary"),
        name="attention_latent" if past else "attention_context",
    )(*args)


def _stack_heads(x):
    block = _lane_block(x.shape, RET_DK)
    zero = jnp.zeros_like(x)
    return jnp.concatenate([jnp.where(block == h, x, zero) for h in range(RET_HEADS)], axis=-2)


def _head_norm(o, gm):
    d = o - _group_mean(o, gm)
    return d * lax.rsqrt(_group_mean(d * d, gm) + NORM_EPS)


def _retention_kernel(*refs, has_state, n_chunks):
    (rq_ref, rk_ref, rv_ref, g_ref, gm_ref, df_ref, db_ref, qd_ref, kd_ref, cdf_ref, cdb_ref,
     fx_ref, dft_c_ref, dft_p_ref) = refs[:14]
    refs = refs[15:]
    if has_state:
        s0_ref, o_ref, both_ref, upd_ref, st_ref = refs
    else:
        o_ref, sfin_ref, both_ref, upd_ref, st_ref = refs[-5:]

    width = RET_WIDTH
    along = _dot(fx_ref[...], dft_c_ref[...])
    stacked = jnp.concatenate([along[:, 0:FOURIER_WIDTH], along[:, FOURIER_WIDTH:]], axis=0).astype(BF16)
    o_ref[:, width:] = _dot(dft_p_ref[...], stacked).astype(BF16)

    diag = (lax.broadcasted_iota(jnp.int32, (width, width), 0) // RET_DK
            == lax.broadcasted_iota(jnp.int32, (width, width), 1) // RET_DV)

    chunked = lambda ref: ref[...].reshape(n_chunks, RET_CHUNK, width)
    batched = lambda a, b, contract: lax.dot_general(a, b, (contract, ((0,), (0,))), preferred_element_type=F32)
    q3, k3, v3 = chunked(rq_ref), chunked(rk_ref), chunked(rv_ref)
    scores = batched(q3, _stack_heads(k3), ((2,), (2,)))
    v_heads = _stack_heads(v3)
    seq_rows = n_chunks * RET_CHUNK
    both_ref[:, 0:width] = batched((scores * df_ref[...]).astype(BF16), v_heads, ((2,), (1,))).reshape(seq_rows, width)
    both_ref[:, width:] = batched((scores * db_ref[...]).astype(BF16), v_heads, ((2,), (1,))).reshape(seq_rows, width)
    k32 = k3.astype(F32)
    k_decayed = jnp.concatenate([k32, k32], axis=2) * kd_ref[...]
    upd_ref[...] = batched(jnp.swapaxes(k_decayed, 1, 2).astype(BF16), v3, ((2,), (1,)))

    for direction, order, cd_ref in ((0, range(n_chunks), cdf_ref), (1, reversed(range(n_chunks)), cdb_ref)):
        if has_state:
            s = s0_ref[direction].reshape(width, RET_DV)
            state = jnp.where(diag, jnp.concatenate([s] * RET_HEADS, axis=1), 0.0)
        else:
            state = jnp.zeros((width, width), F32)
        for c in order:
            st_ref[direction, c] = state.astype(BF16)
            update = upd_ref[c, direction * width:(direction + 1) * width, :]
            state = cd_ref[...] * state + jnp.where(diag, update, 0.0)
        if not has_state:
            folded = state[:, 0:width // 2] + state[:, width // 2:]
            sfin_ref[direction] = folded[:, 0:RET_DV] + folded[:, RET_DV:]

    q32 = q3.astype(F32)
    q_decayed = (jnp.concatenate([q32, q32], axis=2) * qd_ref[...]).astype(BF16)
    both_ref[:, 0:width] += batched(q_decayed[:, :, 0:width], st_ref[0], ((2,), (1,))).reshape(seq_rows, width)
    both_ref[:, width:] += batched(q_decayed[:, :, width:], st_ref[1], ((2,), (1,))).reshape(seq_rows, width)

    gated = _silu(g_ref[...]) * _head_norm(both_ref[...], gm_ref[...])
    o_ref[:, 0:width] = (gated[:, 0:width] + gated[:, width:]).astype(BF16)


def _retention_fourier(rq, rk, rv, gates, fx, group_mean, tables, dft, state, new_state, mix, layer, batch, seq):
    has_state = state is not None
    n_chunks = seq // RET_CHUNK
    seq_blk = lambda width: pl.BlockSpec((seq, width), lambda b: (b, 0))
    whole = lambda a: pl.BlockSpec(a.shape, lambda b: (0,) * a.ndim)
    in_specs = ([seq_blk(RET_WIDTH)] * 3 + [seq_blk(2 * RET_WIDTH), whole(group_mean)] + [whole(t) for t in tables]
                + [seq_blk(FOURIER_WIDTH), whole(dft[0]), whole(dft[1]), pl.BlockSpec(memory_space=pl.ANY)])
    args = [rq, rk, rv, gates, group_mean] + list(tables) + [fx, dft[0], dft[1], mix]
    aliases = {len(args) - 1: 0}
    out_shape = [jax.ShapeDtypeStruct(mix.shape, mix.dtype)]
    assert MIX_OFF_FOURIER == MIX_OFF_RET + RET_WIDTH and MIX_OFF_RET % (RET_WIDTH + FOURIER_WIDTH) == 0
    out_specs = [pl.BlockSpec((seq, RET_WIDTH + FOURIER_WIDTH), lambda b: (b, MIX_OFF_RET // (RET_WIDTH + FOURIER_WIDTH)))]
    if has_state:
        in_specs.append(pl.BlockSpec((None, None, 2, RET_HEADS, RET_DK, RET_DV), lambda b: (b, layer, 0, 0, 0, 0)))
        args.append(state)
    else:
        out_shape.append(jax.ShapeDtypeStruct((batch, DEPTH, 2, RET_HEADS * RET_DK, RET_DV), F32))
        out_specs.append(pl.BlockSpec((None, None, 2, RET_HEADS * RET_DK, RET_DV), lambda b: (b, layer, 0, 0, 0)))
        if new_state is not None:
            aliases[len(args)] = 1
            in_specs.append(pl.BlockSpec(memory_space=pl.ANY))
            args.append(new_state)
    return pl.pallas_call(
        functools.partial(_retention_kernel, has_state=has_state, n_chunks=n_chunks),
        out_shape=out_shape,
        grid=(batch,),
        in_specs=in_specs,
        out_specs=out_specs,
        input_output_aliases=aliases,
        scratch_shapes=[pltpu.VMEM((seq, 2 * RET_WIDTH), F32),
                        pltpu.VMEM((n_chunks, 2 * RET_WIDTH, RET_WIDTH), F32),
                        pltpu.VMEM((2, n_chunks, RET_WIDTH, RET_WIDTH), BF16)],
        compiler_params=_params("parallel"),
        name="retention_fourier_latent" if has_state else "retention_fourier_context",
    )(*args)


def _shift_rows(u, seq):
    rows = u.shape[0]
    pos = lax.broadcasted_iota(jnp.int32, u.shape, 0) % seq
    prev = jnp.where(pos == 0, 0.0, pltpu.roll(u, 1, 0))
    nxt = jnp.where(pos == seq - 1, 0.0, pltpu.roll(u, rows - 1, 0))
    return prev, nxt


def _mix_ffn_kernel(*refs, seq, halo, final):
    if halo:
        (x_ref, x_top_ref, x_bot_ref, mix_ref, mix_top_ref, mix_bot_ref,
         mod_ref, wout_ref, n2_ref, wup_ref, cw_ref, cb_ref, wd_ref, fin_ref, o_ref) = refs
    else:
        x_ref, mix_ref, mod_ref, wout_ref, n2_ref, wup_ref, cw_ref, cb_ref, wd_ref, fin_ref, o_ref = refs
    rows = x_ref.shape[0]
    modulate = lambda x: _rms_rows(x, n2_ref[...]) * (1.0 + mod_ref[4:5, :]) + mod_ref[3:4, :]
    if halo:
        pad = x_top_ref.shape[0]
        skip = mix_top_ref.shape[0] - pad
        tiles_per_seq = seq // rows
        place = pl.program_id(0) % tiles_per_seq
        mixed = _dot(jnp.concatenate([mix_top_ref[...], mix_ref[...], mix_bot_ref[...]], axis=0), wout_ref[...])
        x_ext = jnp.concatenate([x_top_ref[...], x_ref[...], x_bot_ref[...]], axis=0)
        x1_ext = x_ext + mod_ref[2:3, :] * mixed[skip:skip + pad + rows + pad, :]
        x1 = x1_ext[pad:pad + rows, :]
        h2 = modulate(x1_ext)
        row = lax.broadcasted_iota(jnp.int32, h2.shape, 0)
        first_kept = jnp.where(place == 0, pad, 0)
        end_kept = jnp.where(place == tiles_per_seq - 1, pad + rows, pad + rows + pad)
        h2 = jnp.where((row >= first_kept) & (row < end_kept), h2, 0.0)
        u = _dot(h2.astype(BF16), wup_ref[...])
        prev, nxt = pltpu.roll(u, 1, 0), pltpu.roll(u, u.shape[0] - 1, 0)
    else:
        x1 = x_ref[...] + mod_ref[2:3, :] * _dot(mix_ref[...], wout_ref[...])
        u = _dot(modulate(x1).astype(BF16), wup_ref[...])
        prev, nxt = _shift_rows(u, seq)
    u = prev * cw_ref[0:1, :] + u * cw_ref[1:2, :] + nxt * cw_ref[2:3, :] + cb_ref[...]
    if halo:
        u = u[pad:pad + rows, :]
    act = _silu(u[:, 0:D_FF]) * u[:, D_FF:]
    y = x1 + mod_ref[5:6, :] * _dot(act.astype(BF16), wd_ref[...])
    o_ref[...] = _rms_rows(y, fin_ref[...]) if final else y


def _mix_ffn(x, mix, mod, w_out, norm2_w, w_up, conv_w, conv_b, w_down, final_w, layer, seq, per_seq_mod, final):
    n = x.shape[0]
    rows = FFN_ROWS
    halo = seq > rows
    assert (seq % rows == 0) if halo else (rows % seq == 0)
    row_blk = lambda width: pl.BlockSpec((rows, width), lambda i: (i, 0))
    whole = lambda shape: pl.BlockSpec(shape, lambda i: (0,) * len(shape))
    resident = lambda r, c: pl.BlockSpec((None, r, c), lambda i: (layer, 0, 0), pipeline_mode=pl.Buffered(1))
    mod_idx = (lambda i: (i * rows // seq, 0, 0)) if per_seq_mod else (lambda i: (0, 0, 0))

    def with_halo(a, pad):
        width = a.shape[1]
        per_tile, last = rows // pad, n // pad - 1
        return ([row_blk(width),
                 pl.BlockSpec((pad, width), lambda i: (jnp.maximum(i * per_tile - 1, 0), 0)),
                 pl.BlockSpec((pad, width), lambda i: (jnp.minimum((i + 1) * per_tile, last), 0))], [a, a, a])

    if halo:
        x_specs, x_args = with_halo(x, F32_SUBLANES)
        mix_specs, mix_args = with_halo(mix, BF16_SUBLANES)
    else:
        x_specs, x_args, mix_specs, mix_args = [row_blk(D_MODEL)], [x], [row_blk(MIX_WIDTH)], [mix]
    in_specs = x_specs + mix_specs + [
        pl.BlockSpec((None, N_MOD, D_MODEL), mod_idx), resident(MIX_WIDTH, D_MODEL), whole((1, D_MODEL)),
        resident(D_MODEL, 2 * D_FF), whole((3, 2 * D_FF)), whole((1, 2 * D_FF)),
        resident(D_FF, D_MODEL), whole((1, D_MODEL))]
    args = x_args + mix_args + [mod, w_out, norm2_w, w_up, conv_w, conv_b, w_down, final_w]
    return pl.pallas_call(
        functools.partial(_mix_ffn_kernel, seq=seq, halo=halo, final=final),
        out_shape=jax.ShapeDtypeStruct((n, D_MODEL), F32),
        grid=(n // rows,),
        in_specs=in_specs,
        out_specs=row_blk(D_MODEL),
        compiler_params=_params("parallel"),
        name="mix_ffn_seq%d" % seq,
    )(*args)


def _layer(x, mod, lp, consts, layer, batch, seq, ctx, new_ctx, final):
    latent = ctx is not None
    outs = _in_projection(x, mod, lp["norm1_w"], consts["w_in"], lp["q_norm_w"], lp["k_norm_w"], consts["group_mean"],
                          consts["rope"] if latent else None, None if new_ctx is None else new_ctx[:2], layer, seq)
    q, k, v, rq, rk, rv, gates, fx = outs[:8]
    mix = _attention(q, k, v, ctx[:2] if latent else None, layer, batch, seq)
    ret_out = _retention_fourier(rq, rk, rv, gates, fx, consts["group_mean"], consts["ret_tables"], consts["dft"][seq],
                                 ctx[2] if latent else None, None if new_ctx is None else new_ctx[2], mix,
                                 layer, batch, seq)
    x = _mix_ffn(x, ret_out[0], mod, consts["w_out"], lp["norm2_w"], consts["w_up"], lp["conv_w"], lp["conv_b"],
                 consts["w_down"], consts["final_w"], layer, seq, latent, final)
    if latent:
        return x, None
    return x, (outs[8], outs[9], ret_out[1])


def kernel(x_prompt, x_sample, c, cache_attn_k, cache_attn_v, state_ret, c_ctx, w_mod, b_mod, norm1_w, w_in,
           q_norm_w, k_norm_w, w_out, norm2_w, w_up, conv_w, conv_b, w_down, final_norm_w):
    batch, seq, d = x_prompt.shape
    dec_batch, dec_seq, _ = x_sample.shape
    past = cache_attn_k.shape[2]
    assert d == D_MODEL and w_in.shape == (DEPTH, D_MODEL, IN_WIDTH) and w_up.shape == (DEPTH, D_MODEL, 2 * D_FF)
    assert (batch * seq) % FFN_ROWS == 0 and dec_seq % FFN_ROWS == 0
    assert seq % min(ATTN_Q_ROWS, seq) == 0 and dec_seq % min(ATTN_Q_ROWS, dec_seq) == 0
    assert INPROJ_ROWS % seq == 0 and dec_seq % INPROJ_ROWS == 0 and seq % RET_CHUNK == 0 and dec_seq % RET_CHUNK == 0
    assert dec_batch + 1 <= MOD_ROWS and dec_seq % GRID_W == 0
    assert HEAD_DIM == RET_DV and GROUP_WIDTH == RET_WIDTH

    consts = {
        "group_mean": _group_mean_matrix(GROUP_WIDTH, HEAD_DIM),
        "rope": _rope_tables(dec_seq),
        "ret_tables": _retention_tables(),
        "dft": {s: _dft_tables(s) for s in {seq, dec_seq}},
        "final_w": final_norm_w.reshape(1, D_MODEL),
        "w_in": w_in.astype(BF16),
        "w_out": w_out.astype(BF16),
        "w_up": w_up.astype(BF16),
        "w_down": w_down.astype(BF16),
    }

    cvec = jnp.zeros((MOD_ROWS, D_MODEL), F32).at[0].set(c_ctx).at[1:1 + dec_batch].set(c)
    mod = _modulation(cvec, w_mod, b_mod).reshape(DEPTH, MOD_ROWS, N_MOD, D_MODEL)

    cache_k = cache_attn_k.reshape(dec_batch, DEPTH, past, KV_WIDTH)
    cache_v = cache_attn_v.reshape(dec_batch, DEPTH, past, KV_WIDTH)

    xp = x_prompt.reshape(batch * seq, D_MODEL)
    xs = x_sample.reshape(dec_batch * dec_seq, D_MODEL)
    new_ctx = None
    for layer in range(DEPTH):
        lp = {
            "norm1_w": norm1_w[layer].reshape(1, D_MODEL),
            "q_norm_w": jnp.tile(q_norm_w[layer], ATTN_HEADS).reshape(1, ATTN_WIDTH),
            "k_norm_w": jnp.tile(k_norm_w[layer], ATTN_KV_HEADS).reshape(1, KV_WIDTH),
            "norm2_w": norm2_w[layer].reshape(1, D_MODEL),
            "conv_w": conv_w[layer],
            "conv_b": conv_b[layer].reshape(1, 2 * D_FF),
        }
        final = layer == DEPTH - 1
        xp, new_ctx = _layer(xp, mod[layer, 0:1], lp, consts, layer, batch, seq, None, new_ctx, final)
        xs, _ = _layer(xs, mod[layer, 1:1 + dec_batch], lp, consts, layer, dec_batch, dec_seq,
                       (cache_k, cache_v, state_ret), None, final)
    new_k, new_v, new_s = new_ctx
    return (xp.reshape(batch, seq, D_MODEL), xs.reshape(dec_batch, dec_seq, D_MODEL),
            new_k.reshape(batch, DEPTH, seq, ATTN_KV_HEADS, HEAD_DIM),
            new_v.reshape(batch, DEPTH, seq, ATTN_KV_HEADS, HEAD_DIM),
            new_s.reshape(batch, DEPTH, 2, RET_HEADS, RET_DK, RET_DV))
```

```python
import functools

import jax
import jax.numpy as jnp
import numpy as np
from jax import lax
from jax.experimental import pallas as pl
from jax.experimental.pallas import tpu as pltpu

F32 = jnp.float32
BF16 = jnp.bfloat16

D_MODEL = 1024
DEPTH = 2
GRID_W = 64
NORM_EPS = 1e-6
ATTN_HEADS = 8
ATTN_KV_HEADS = 2
HEAD_DIM = 64
ATTN_WIDTH = ATTN_HEADS * HEAD_DIM
KV_WIDTH = ATTN_KV_HEADS * HEAD_DIM
HEADS_PER_KV = ATTN_HEADS // ATTN_KV_HEADS
GROUP_WIDTH = HEADS_PER_KV * HEAD_DIM
ATTN_LOGIT_SCALE = HEAD_DIM ** -0.5 * 1.4426950408889634
ROPE_THETA = 10000.0
ROPE_AXIS_DIM = HEAD_DIM // 2
ROPE_HALF = ROPE_AXIS_DIM // 2
RET_HEADS = 4
RET_DK = 64
RET_DV = 64
RET_WIDTH = RET_HEADS * RET_DV
RET_CHUNK = 128
RET_DECAY_EXP_FWD = 5.0
RET_DECAY_EXP_BWD = 5.5
FOURIER_GROUPS = 4
FOURIER_DIM = 64
FOURIER_WIDTH = FOURIER_GROUPS * FOURIER_DIM
D_FF = 2816
N_MOD = 6

OFF_Q = 0
OFF_K = OFF_Q + ATTN_WIDTH
OFF_V = OFF_K + KV_WIDTH
OFF_RQ = OFF_V + KV_WIDTH
OFF_RK = OFF_RQ + RET_WIDTH
OFF_RV = OFF_RK + RET_WIDTH
OFF_GF = OFF_RV + RET_WIDTH
OFF_GB = OFF_GF + RET_WIDTH
OFF_FX = OFF_GB + RET_WIDTH
IN_WIDTH = OFF_FX + FOURIER_WIDTH
MIX_OFF_ATTN = 0
MIX_OFF_RET = MIX_OFF_ATTN + ATTN_WIDTH
MIX_OFF_FOURIER = MIX_OFF_RET + RET_WIDTH
MIX_WIDTH = MIX_OFF_FOURIER + FOURIER_WIDTH

V7X_VMEM_BYTES = 64 * 1024 * 1024
VMEM_LIMIT = V7X_VMEM_BYTES - 8 * 1024 * 1024

MOD_ROWS = 16
MOD_COLS = 1536
INPROJ_ROWS = 1024
ATTN_Q_ROWS = 1024
ATTN_LOGIT_ELEMS = 1024 * 1024
FFN_ROWS = 512
F32_SUBLANES = 8
BF16_SUBLANES = 16


def _params(*semantics):
    return pltpu.CompilerParams(dimension_semantics=semantics, vmem_limit_bytes=VMEM_LIMIT)


def _dot(a, b):
    return jnp.dot(a, b, preferred_element_type=F32)


def _group_mean(x, gm):
    xb = x.astype(BF16)
    width, slab = x.shape[1], gm.shape[0]
    if width <= slab:
        return _dot(xb, gm[0:width, 0:width])
    return jnp.concatenate([_dot(xb[:, s:s + slab], gm) for s in range(0, width, slab)], axis=1)


def _dot_nt(a, b):
    return lax.dot_general(a, b, (((1,), (1,)), ((), ())), preferred_element_type=F32)


def _sigmoid(x):
    return 1.0 / (1.0 + jnp.exp(-x))


def _silu(x):
    return x * _sigmoid(x)


def _rms_rows(x, w):
    ms = jnp.mean(x * x, axis=-1, keepdims=True)
    return x * lax.rsqrt(ms + NORM_EPS) * w


def _lane_block(shape, width):
    return lax.broadcasted_iota(jnp.int32, shape, len(shape) - 1) // width


def _group_mean_matrix(width, group):
    idx = np.arange(width) // group
    return jnp.asarray((idx[:, None] == idx[None, :]).astype(np.float32) / group, dtype=BF16)


def _rope_tables(n_tokens):
    pos = np.arange(n_tokens)
    row = (pos // GRID_W).astype(np.float64)
    col = (pos % GRID_W).astype(np.float64)
    freqs = ROPE_THETA ** (-np.arange(ROPE_HALF, dtype=np.float64) / ROPE_HALF)
    d = np.arange(HEAD_DIM)
    coord = np.where((d // ROPE_AXIS_DIM)[None, :] == 0, row[:, None], col[:, None])
    ang = coord * freqs[d % ROPE_HALF][None, :]
    sign = np.where((d & ROPE_HALF) == 0, -1.0, 1.0)[None, :]
    cos = np.tile(np.cos(ang), (1, ATTN_HEADS))
    sin = np.tile(np.sin(ang) * sign, (1, ATTN_HEADS))
    return jnp.asarray(cos, F32), jnp.asarray(sin, F32)


def _dft_tables(seq):
    n = np.arange(seq)
    ang = 2.0 * np.pi * ((n[:, None] * n[None, :]) % seq) / seq
    scale = 1.0 / np.sqrt(seq * FOURIER_DIM)
    position = np.concatenate([np.cos(ang), -np.sin(ang)], axis=1)
    c = np.arange(FOURIER_WIDTH)
    same = (c[:, None] // FOURIER_DIM) == (c[None, :] // FOURIER_DIM)
    angc = 2.0 * np.pi * (((c % FOURIER_DIM)[:, None] * (c % FOURIER_DIM)[None, :]) % FOURIER_DIM) / FOURIER_DIM
    channel = np.concatenate([np.where(same, np.cos(angc), 0.0), np.where(same, np.sin(angc), 0.0)], axis=1)
    as_bf16 = lambda a: jnp.asarray(a, F32).astype(BF16)
    return as_bf16(channel), as_bf16(position * scale)


def _retention_tables():
    heads = jnp.arange(RET_HEADS, dtype=F32)
    idx = jnp.arange(RET_CHUNK, dtype=F32)
    diff = idx[:, None] - idx[None, :]
    out = []
    for exp0, backward in ((RET_DECAY_EXP_FWD, False), (RET_DECAY_EXP_BWD, True)):
        lg = jnp.log1p(-jnp.exp2(-(exp0 + heads)))
        dd = -diff if backward else diff
        inner = jnp.where(dd[None] >= 0, jnp.exp(jnp.maximum(dd, 0.0)[None] * lg[:, None, None]), 0.0)
        inner = inner.transpose(1, 0, 2).reshape(RET_CHUNK, RET_HEADS * RET_CHUNK)
        q_pow = (RET_CHUNK - idx) if backward else (idx + 1.0)
        k_pow = idx if backward else (RET_CHUNK - 1.0 - idx)
        spread = lambda p: jnp.repeat(jnp.exp(p[:, None] * lg[None, :]), RET_DK, axis=1)
        q_decay, k_decay = spread(q_pow), spread(k_pow)
        chunk_decay = jnp.repeat(jnp.exp(RET_CHUNK * lg), RET_DV)[None, :]
        out.append((inner, q_decay, k_decay, chunk_decay))
    (inner_f, qd_f, kd_f, cd_f), (inner_b, qd_b, kd_b, cd_b) = out
    return (inner_f, inner_b, jnp.concatenate([qd_f, qd_b], axis=1), jnp.concatenate([kd_f, kd_b], axis=1), cd_f, cd_b)


def _mod_kernel(c_ref, w_ref, b_ref, o_ref):
    act = _silu(c_ref[...]).astype(BF16)
    o_ref[...] = _dot(act, w_ref[...].astype(BF16)) + b_ref[...]


def _modulation(cvec, w_mod, b_mod):
    n_cols = w_mod.shape[-1]
    return pl.pallas_call(
        _mod_kernel,
        out_shape=jax.ShapeDtypeStruct((DEPTH, MOD_ROWS, n_cols), F32),
        grid=(DEPTH, n_cols // MOD_COLS),
        in_specs=[
            pl.BlockSpec((MOD_ROWS, D_MODEL), lambda l, j: (0, 0)),
            pl.BlockSpec((None, D_MODEL, MOD_COLS), lambda l, j: (l, 0, j)),
            pl.BlockSpec((None, 1, MOD_COLS), lambda l, j: (l, 0, j)),
        ],
        out_specs=pl.BlockSpec((None, MOD_ROWS, MOD_COLS), lambda l, j: (l, 0, j)),
        compiler_params=_params("parallel", "parallel"),
        name="modulation",
    )(cvec, w_mod, b_mod.reshape(DEPTH, 1, n_cols))


def _swap_rotary_pairs(x):
    width = x.shape[-1]
    lane = lax.broadcasted_iota(jnp.int32, x.shape, 1)
    from_below = pltpu.roll(x, ROPE_HALF, 1)
    from_above = pltpu.roll(x, width - ROPE_HALF, 1)
    return jnp.where((lane & ROPE_HALF) != 0, from_below, from_above)


def _inproj_kernel(*refs, latent, layer, starts_cache_output):
    x_ref, mod_ref, n1_ref, win_ref, qnw_ref, knw_ref, gm_ref = refs[:7]
    refs = refs[7:]
    if latent:
        cos_ref, sin_ref = refs[:2]
        refs = refs[2:]
        q_ref, k_ref, v_ref, rq_ref, rk_ref, rv_ref, g_ref, fx_ref = refs
    else:
        q_ref, k_ref, v_ref, rq_ref, rk_ref, rv_ref, g_ref, fx_ref, k32_ref, v32_ref = refs[-10:]

    shift, scale = mod_ref[0:1, :], mod_ref[1:2, :]
    h = _rms_rows(x_ref[...], n1_ref[...]) * (1.0 + scale) + shift
    all_proj = _dot(h.astype(BF16), win_ref[...])
    proj = lambda off, width: all_proj[:, off:off + width]

    q = proj(OFF_Q, ATTN_WIDTH)
    q = q * lax.rsqrt(_group_mean(q * q, gm_ref[...]) + NORM_EPS) * qnw_ref[...]
    k = proj(OFF_K, KV_WIDTH)
    k = k * lax.rsqrt(_group_mean(k * k, gm_ref[...]) + NORM_EPS) * knw_ref[...]
    v = proj(OFF_V, KV_WIDTH)
    if latent:
        cos, sin = cos_ref[...], sin_ref[...]
        q = q * cos + _swap_rotary_pairs(q) * sin
        k = k * cos[:, 0:KV_WIDTH] + _swap_rotary_pairs(k) * sin[:, 0:KV_WIDTH]
    else:
        for ref, val in ((k32_ref, k), (v32_ref, v)):
            if starts_cache_output:
                for other in range(ref.shape[1]):
                    if other != layer:
                        ref[:, other] = jnp.zeros((ref.shape[0],) + ref.shape[2:], F32)
                ref[:, layer] = val.reshape((ref.shape[0],) + ref.shape[2:])
            else:
                ref[...] = val.reshape(ref.shape)
    q_ref[...] = (q * ATTN_LOGIT_SCALE).astype(BF16)
    k_ref[...] = k.astype(BF16)
    v_ref[...] = v.astype(BF16)
    rq_ref[...] = proj(OFF_RQ, RET_WIDTH).astype(BF16)
    rk_ref[...] = (proj(OFF_RK, RET_WIDTH) * RET_DK ** -0.5).astype(BF16)
    rv_ref[...] = proj(OFF_RV, RET_WIDTH).astype(BF16)
    g_ref[...] = proj(OFF_GF, 2 * RET_WIDTH)
    fx_ref[...] = proj(OFF_FX, FOURIER_WIDTH).astype(BF16)


def _in_projection(x, mod, norm1_w, w_in, q_norm_w, k_norm_w, group_mean, rope, new_cache, layer, seq):
    n = x.shape[0]
    rows = INPROJ_ROWS
    latent = rope is not None
    row_blk = lambda width: pl.BlockSpec((rows, width), lambda i: (i, 0))
    whole = lambda shape: pl.BlockSpec(shape, lambda i: (0,) * len(shape))
    mod_idx = (lambda i: (i * rows // seq, 0, 0)) if latent else (lambda i: (0, 0, 0))
    in_specs = [
        row_blk(D_MODEL),
        pl.BlockSpec((None, N_MOD, D_MODEL), mod_idx),
        whole((1, D_MODEL)),
        pl.BlockSpec((None, D_MODEL, IN_WIDTH), lambda i: (layer, 0, 0), pipeline_mode=pl.Buffered(1)),
        whole((1, ATTN_WIDTH)),
        whole((1, KV_WIDTH)),
        whole(group_mean.shape),
    ]
    args = [x, mod, norm1_w, w_in, q_norm_w, k_norm_w, group_mean]
    outs = [(ATTN_WIDTH, BF16), (KV_WIDTH, BF16), (KV_WIDTH, BF16), (RET_WIDTH, BF16), (RET_WIDTH, BF16),
            (RET_WIDTH, BF16), (2 * RET_WIDTH, F32), (FOURIER_WIDTH, BF16)]
    out_shape = [jax.ShapeDtypeStruct((n, w), dt) for w, dt in outs]
    out_specs = [row_blk(w) for w, _ in outs]
    aliases = {}
    if latent:
        pos_blk = pl.BlockSpec((rows, ATTN_WIDTH), lambda i: (i % (seq // rows), 0))
        in_specs += [pos_blk, pos_blk]
        args += list(rope)
    else:
        seqs = rows // seq
        out_shape += [jax.ShapeDtypeStruct((n // seq, DEPTH, seq, KV_WIDTH), F32)] * 2
        if new_cache is None:
            out_specs += [pl.BlockSpec((seqs, DEPTH, seq, KV_WIDTH), lambda i: (i, 0, 0, 0))] * 2
        else:
            out_specs += [pl.BlockSpec((seqs, None, seq, KV_WIDTH), lambda i: (i, layer, 0, 0))] * 2
            for j, earlier in enumerate(new_cache):
                aliases[len(args)] = len(out_shape) - 2 + j
                in_specs.append(pl.BlockSpec(memory_space=pl.ANY))
                args.append(earlier)
    return pl.pallas_call(
        functools.partial(_inproj_kernel, latent=latent, layer=layer,
                          starts_cache_output=not latent and new_cache is None),
        out_shape=out_shape,
        grid=(n // rows,),
        in_specs=in_specs,
        out_specs=out_specs,
        input_output_aliases=aliases,
        compiler_params=_params("parallel"),
        name="in_projection_latent" if latent else "in_projection_context",
    )(*args)


def _spread_kv(x, group):
    lane = lax.broadcasted_iota(jnp.int32, x.shape, 1)
    other = pltpu.roll(x, HEAD_DIM, 1)
    own = (lane // HEAD_DIM) == group
    pair = jnp.where(own, x, other).astype(BF16)
    return jnp.concatenate([pair, pair], axis=1)


def _attention_kernel(*refs, past, heads_per_dot):
    if past:
        q_ref, k_ref, v_ref, ck_ref, cv_ref, o_ref, kt_ref, vt_ref = refs
    else:
        q_ref, k_ref, v_ref, o_ref, kt_ref, vt_ref = refs

    @pl.when(pl.program_id(1) == 0)
    def _():
        for g in range(ATTN_KV_HEADS):
            if past:
                kt_ref[g, 0:past, :] = _spread_kv(ck_ref[...], g)
                vt_ref[g, 0:past, :] = _spread_kv(cv_ref[...], g)
            kt_ref[g, past:, :] = _spread_kv(k_ref[...].astype(F32), g)
            vt_ref[g, past:, :] = _spread_kv(v_ref[...].astype(F32), g)

    rows = q_ref.shape[0]
    block = _lane_block((rows, GROUP_WIDTH), HEAD_DIM)

    def logits(unit):
        g, first = divmod(unit * heads_per_dot, HEADS_PER_KV)
        qg = q_ref[:, g * GROUP_WIDTH:(g + 1) * GROUP_WIDTH]
        zero = jnp.zeros_like(qg)
        stacked = jnp.concatenate([jnp.where(block == first + j, qg, zero) for j in range(heads_per_dot)], axis=0)
        return _dot_nt(stacked, kt_ref[g])

    n_units = ATTN_HEADS // heads_per_dot
    s_next = logits(0)
    out = None
    for unit in range(n_units):
        g, first = divmod(unit * heads_per_dot, HEADS_PER_KV)
        s = s_next
        if unit + 1 < n_units:
            s_next = logits(unit + 1)
        p = jnp.exp2(s - jnp.max(s, axis=-1, keepdims=True))
        denom = jnp.sum(p, axis=-1, keepdims=True)
        o = _dot(p.astype(BF16), vt_ref[g]) / denom
        for j in range(heads_per_dot):
            h = first + j
            piece = o[j * rows:(j + 1) * rows, :]
            out = piece if h == 0 else jnp.where(block == h, piece, out)
        if first + heads_per_dot == HEADS_PER_KV:
            o_ref[:, g * GROUP_WIDTH:(g + 1) * GROUP_WIDTH] = out.astype(BF16)


def _attention(q, k, v, cache, layer, batch, seq):
    n = q.shape[0]
    tq = min(ATTN_Q_ROWS, seq)
    nq = seq // tq
    past = 0 if cache is None else cache[0].shape[2]
    heads_per_dot = max(1, min(HEADS_PER_KV, ATTN_LOGIT_ELEMS // (tq * (past + seq))))
    assert HEADS_PER_KV % heads_per_dot == 0
    own_kv = pl.BlockSpec((seq, KV_WIDTH), lambda b, i: (b, 0))
    in_specs = [pl.BlockSpec((tq, ATTN_WIDTH), lambda b, i: (b * nq + i, 0)), own_kv, own_kv]
    args = [q, k, v]
    if past:
        cached_kv = pl.BlockSpec((None, None, past, KV_WIDTH), lambda b, i: (b, layer, 0, 0))
        in_specs += [cached_kv, cached_kv]
        args += list(cache)
    return pl.pallas_call(
        functools.partial(_attention_kernel, past=past, heads_per_dot=heads_per_dot),
        out_shape=jax.ShapeDtypeStruct((n, ATTN_WIDTH), BF16),
        grid=(batch, nq),
        in_specs=in_specs,
        out_specs=pl.BlockSpec((tq, ATTN_WIDTH), lambda b, i: (b * nq + i, 0)),
        scratch_shapes=[pltpu.VMEM((ATTN_KV_HEADS, past + seq, GROUP_WIDTH), BF16)] * 2,
        compiler_params=_params("parallel", "arbitrary"),
        name="attention_latent" if past else "attention_context",
    )(*args)


def _stack_heads(x):
    block = _lane_block(x.shape, RET_DK)
    zero = jnp.zeros_like(x)
    return jnp.concatenate([jnp.where(block == h, x, zero) for h in range(RET_HEADS)], axis=-2)


def _head_norm(o, gm):
    d = o - _group_mean(o, gm)
    return d * lax.rsqrt(_group_mean(d * d, gm) + NORM_EPS)


def _retention_kernel(*refs, has_state, n_chunks, layer, starts_state_output):
    (rq_ref, rk_ref, rv_ref, g_ref, gm_ref, df_ref, db_ref, qd_ref, kd_ref, cdf_ref, cdb_ref,
     fx_ref, dft_c_ref, dft_p_ref) = refs[:14]
    refs = refs[14:]
    if has_state:
        s0_ref, o_ref, both_ref, upd_ref, st_ref = refs
    else:
        o_ref, sfin_ref, both_ref, upd_ref, st_ref = refs[-5:]
        if starts_state_output:
            for other in range(sfin_ref.shape[0]):
                if other != layer:
                    sfin_ref[other] = jnp.zeros(sfin_ref.shape[1:], F32)
            sfin_ref = sfin_ref.at[layer]

    width = RET_WIDTH
    along = _dot(fx_ref[...], dft_c_ref[...])
    stacked = jnp.concatenate([along[:, 0:FOURIER_WIDTH], along[:, FOURIER_WIDTH:]], axis=0).astype(BF16)
    o_ref[:, width:] = _dot(dft_p_ref[...], stacked).astype(BF16)

    diag = (lax.broadcasted_iota(jnp.int32, (width, width), 0) // RET_DK
            == lax.broadcasted_iota(jnp.int32, (width, width), 1) // RET_DV)

    chunked = lambda ref: ref[...].reshape(n_chunks, RET_CHUNK, width)
    batched = lambda a, b, contract: lax.dot_general(a, b, (contract, ((0,), (0,))), preferred_element_type=F32)
    q3, k3, v3 = chunked(rq_ref), chunked(rk_ref), chunked(rv_ref)
    scores = batched(q3, _stack_heads(k3), ((2,), (2,)))
    v_heads = _stack_heads(v3)
    seq_rows = n_chunks * RET_CHUNK
    both_ref[:, 0:width] = batched((scores * df_ref[...]).astype(BF16), v_heads, ((2,), (1,))).reshape(seq_rows, width)
    both_ref[:, width:] = batched((scores * db_ref[...]).astype(BF16), v_heads, ((2,), (1,))).reshape(seq_rows, width)
    k32 = k3.astype(F32)
    k_decayed = jnp.concatenate([k32, k32], axis=2) * kd_ref[...]
    upd_ref[...] = batched(jnp.swapaxes(k_decayed, 1, 2).astype(BF16), v3, ((2,), (1,)))

    for direction, order, cd_ref in ((0, range(n_chunks), cdf_ref), (1, reversed(range(n_chunks)), cdb_ref)):
        if has_state:
            s = s0_ref[direction].reshape(width, RET_DV)
            state = jnp.where(diag, jnp.concatenate([s] * RET_HEADS, axis=1), 0.0)
        else:
            state = jnp.zeros((width, width), F32)
        for c in order:
            st_ref[direction, c] = state.astype(BF16)
            update = upd_ref[c, direction * width:(direction + 1) * width, :]
            state = cd_ref[...] * state + jnp.where(diag, update, 0.0)
        if not has_state:
            folded = state[:, 0:width // 2] + state[:, width // 2:]
            sfin_ref[direction] = folded[:, 0:RET_DV] + folded[:, RET_DV:]

    q32 = q3.astype(F32)
    q_decayed = (jnp.concatenate([q32, q32], axis=2) * qd_ref[...]).astype(BF16)
    both_ref[:, 0:width] += batched(q_decayed[:, :, 0:width], st_ref[0], ((2,), (1,))).reshape(seq_rows, width)
    both_ref[:, width:] += batched(q_decayed[:, :, width:], st_ref[1], ((2,), (1,))).reshape(seq_rows, width)

    gated = _silu(g_ref[...]) * _head_norm(both_ref[...], gm_ref[...])
    o_ref[:, 0:width] = (gated[:, 0:width] + gated[:, width:]).astype(BF16)


def _retention_fourier(rq, rk, rv, gates, fx, group_mean, tables, dft, state, new_state, layer, batch, seq):
    n = rq.shape[0]
    has_state = state is not None
    n_chunks = seq // RET_CHUNK
    seq_blk = lambda width: pl.BlockSpec((seq, width), lambda b: (b, 0))
    whole = lambda a: pl.BlockSpec(a.shape, lambda b: (0,) * a.ndim)
    in_specs = ([seq_blk(RET_WIDTH)] * 3 + [seq_blk(2 * RET_WIDTH), whole(group_mean)] + [whole(t) for t in tables]
                + [seq_blk(FOURIER_WIDTH), whole(dft[0]), whole(dft[1])])
    args = [rq, rk, rv, gates, group_mean] + list(tables) + [fx, dft[0], dft[1]]
    aliases = {}
    out_shape = [jax.ShapeDtypeStruct((n, RET_WIDTH + FOURIER_WIDTH), BF16)]
    out_specs = [seq_blk(RET_WIDTH + FOURIER_WIDTH)]
    if has_state:
        in_specs.append(pl.BlockSpec((None, None, 2, RET_HEADS, RET_DK, RET_DV), lambda b: (b, layer, 0, 0, 0, 0)))
        args.append(state)
    else:
        state_blk = (2, RET_HEADS * RET_DK, RET_DV)
        out_shape.append(jax.ShapeDtypeStruct((batch, DEPTH) + state_blk, F32))
        if new_state is None:
            out_specs.append(pl.BlockSpec((None, DEPTH) + state_blk, lambda b: (b, 0, 0, 0, 0)))
        else:
            out_specs.append(pl.BlockSpec((None, None) + state_blk, lambda b: (b, layer, 0, 0, 0)))
            aliases[len(args)] = 1
            in_specs.append(pl.BlockSpec(memory_space=pl.ANY))
            args.append(new_state)
    return pl.pallas_call(
        functools.partial(_retention_kernel, has_state=has_state, n_chunks=n_chunks, layer=layer,
                          starts_state_output=new_state is None),
        out_shape=out_shape,
        grid=(batch,),
        in_specs=in_specs,
        out_specs=out_specs,
        input_output_aliases=aliases,
        scratch_shapes=[pltpu.VMEM((seq, 2 * RET_WIDTH), F32),
                        pltpu.VMEM((n_chunks, 2 * RET_WIDTH, RET_WIDTH), F32),
                        pltpu.VMEM((2, n_chunks, RET_WIDTH, RET_WIDTH), BF16)],
        compiler_params=_params("parallel"),
        name="retention_fourier_latent" if has_state else "retention_fourier_context",
    )(*args)


def _shift_rows(u, seq):
    rows = u.shape[0]
    pos = lax.broadcasted_iota(jnp.int32, u.shape, 0) % seq
    prev = jnp.where(pos == 0, 0.0, pltpu.roll(u, 1, 0))
    nxt = jnp.where(pos == seq - 1, 0.0, pltpu.roll(u, rows - 1, 0))
    return prev, nxt


def _mix_ffn_kernel(*refs, seq, halo, final):
    if halo:
        (x_ref, x_top_ref, x_bot_ref, attn_ref, attn_top_ref, attn_bot_ref, rf_ref, rf_top_ref, rf_bot_ref,
         mod_ref, wout_ref, n2_ref, wup_ref, cw_ref, cb_ref, wd_ref, fin_ref, o_ref) = refs
    else:
        x_ref, attn_ref, rf_ref, mod_ref, wout_ref, n2_ref, wup_ref, cw_ref, cb_ref, wd_ref, fin_ref, o_ref = refs
    rows = x_ref.shape[0]
    modulate = lambda x: _rms_rows(x, n2_ref[...]) * (1.0 + mod_ref[4:5, :]) + mod_ref[3:4, :]
    project_out = lambda attn, rf: (_dot(attn, wout_ref[MIX_OFF_ATTN:MIX_OFF_RET, :])
                                    + _dot(rf, wout_ref[MIX_OFF_RET:MIX_WIDTH, :]))
    if halo:
        pad = x_top_ref.shape[0]
        skip = attn_top_ref.shape[0] - pad
        tiles_per_seq = seq // rows
        place = pl.program_id(0) % tiles_per_seq
        mixed = project_out(jnp.concatenate([attn_top_ref[...], attn_ref[...], attn_bot_ref[...]], axis=0),
                            jnp.concatenate([rf_top_ref[...], rf_ref[...], rf_bot_ref[...]], axis=0))
        x_ext = jnp.concatenate([x_top_ref[...], x_ref[...], x_bot_ref[...]], axis=0)
        x1_ext = x_ext + mod_ref[2:3, :] * mixed[skip:skip + pad + rows + pad, :]
        x1 = x1_ext[pad:pad + rows, :]
        h2 = modulate(x1_ext)
        row = lax.broadcasted_iota(jnp.int32, h2.shape, 0)
        first_kept = jnp.where(place == 0, pad, 0)
        end_kept = jnp.where(place == tiles_per_seq - 1, pad + rows, pad + rows + pad)
        h2 = jnp.where((row >= first_kept) & (row < end_kept), h2, 0.0)
        u = _dot(h2.astype(BF16), wup_ref[...])
        prev, nxt = pltpu.roll(u, 1, 0), pltpu.roll(u, u.shape[0] - 1, 0)
    else:
        x1 = x_ref[...] + mod_ref[2:3, :] * project_out(attn_ref[...], rf_ref[...])
        u = _dot(modulate(x1).astype(BF16), wup_ref[...])
        prev, nxt = _shift_rows(u, seq)
    u = prev * cw_ref[0:1, :] + u * cw_ref[1:2, :] + nxt * cw_ref[2:3, :] + cb_ref[...]
    if halo:
        u = u[pad:pad + rows, :]
    act = _silu(u[:, 0:D_FF]) * u[:, D_FF:]
    y = x1 + mod_ref[5:6, :] * _dot(act.astype(BF16), wd_ref[...])
    o_ref[...] = _rms_rows(y, fin_ref[...]) if final else y


def _mix_ffn(x, attn, ret_four, mod, w_out, norm2_w, w_up, conv_w, conv_b, w_down, final_w, layer, seq, per_seq_mod,
             final):
    n = x.shape[0]
    rows = FFN_ROWS
    halo = seq > rows
    assert (seq % rows == 0) if halo else (rows % seq == 0)
    row_blk = lambda width: pl.BlockSpec((rows, width), lambda i: (i, 0))
    whole = lambda shape: pl.BlockSpec(shape, lambda i: (0,) * len(shape))
    resident = lambda r, c: pl.BlockSpec((None, r, c), lambda i: (layer, 0, 0), pipeline_mode=pl.Buffered(1))
    mod_idx = (lambda i: (i * rows // seq, 0, 0)) if per_seq_mod else (lambda i: (0, 0, 0))

    def with_halo(a, pad):
        width = a.shape[1]
        per_tile, last = rows // pad, n // pad - 1
        return ([row_blk(width),
                 pl.BlockSpec((pad, width), lambda i: (jnp.maximum(i * per_tile - 1, 0), 0)),
                 pl.BlockSpec((pad, width), lambda i: (jnp.minimum((i + 1) * per_tile, last), 0))], [a, a, a])

    row_specs, row_args = [], []
    for a, pad in ((x, F32_SUBLANES), (attn, BF16_SUBLANES), (ret_four, BF16_SUBLANES)):
        specs, arrays = with_halo(a, pad) if halo else ([row_blk(a.shape[1])], [a])
        row_specs += specs
        row_args += arrays
    in_specs = row_specs + [
        pl.BlockSpec((None, N_MOD, D_MODEL), mod_idx), resident(MIX_WIDTH, D_MODEL), whole((1, D_MODEL)),
        resident(D_MODEL, 2 * D_FF), whole((3, 2 * D_FF)), whole((1, 2 * D_FF)),
        resident(D_FF, D_MODEL), whole((1, D_MODEL))]
    args = row_args + [mod, w_out, norm2_w, w_up, conv_w, conv_b, w_down, final_w]
    return pl.pallas_call(
        functools.partial(_mix_ffn_kernel, seq=seq, halo=halo, final=final),
        out_shape=jax.ShapeDtypeStruct((n, D_MODEL), F32),
        grid=(n // rows,),
        in_specs=in_specs,
        out_specs=row_blk(D_MODEL),
        compiler_params=_params("parallel"),
        name="mix_ffn_seq%d" % seq,
    )(*args)


def _layer(x, mod, lp, consts, layer, batch, seq, ctx, new_ctx, final):
    latent = ctx is not None
    outs = _in_projection(x, mod, lp["norm1_w"], consts["w_in"], lp["q_norm_w"], lp["k_norm_w"], consts["group_mean"],
                          consts["rope"] if latent else None, None if new_ctx is None else new_ctx[:2], layer, seq)
    q, k, v, rq, rk, rv, gates, fx = outs[:8]
    attn = _attention(q, k, v, ctx[:2] if latent else None, layer, batch, seq)
    ret_out = _retention_fourier(rq, rk, rv, gates, fx, consts["group_mean"], consts["ret_tables"], consts["dft"][seq],
                                 ctx[2] if latent else None, None if new_ctx is None else new_ctx[2],
                                 layer, batch, seq)
    x = _mix_ffn(x, attn, ret_out[0], mod, consts["w_out"], lp["norm2_w"], consts["w_up"], lp["conv_w"], lp["conv_b"],
                 consts["w_down"], consts["final_w"], layer, seq, latent, final)
    if latent:
        return x, None
    return x, (outs[8], outs[9], ret_out[1])


def kernel(x_prompt, x_sample, c, cache_attn_k, cache_attn_v, state_ret, c_ctx, w_mod, b_mod, norm1_w, w_in,
           q_norm_w, k_norm_w, w_out, norm2_w, w_up, conv_w, conv_b, w_down, final_norm_w):
    batch, seq, d = x_prompt.shape
    dec_batch, dec_seq, _ = x_sample.shape
    past = cache_attn_k.shape[2]
    assert d == D_MODEL and w_in.shape == (DEPTH, D_MODEL, IN_WIDTH) and w_up.shape == (DEPTH, D_MODEL, 2 * D_FF)
    assert (batch * seq) % FFN_ROWS == 0 and dec_seq % FFN_ROWS == 0
    assert seq % min(ATTN_Q_ROWS, seq) == 0 and dec_seq % min(ATTN_Q_ROWS, dec_seq) == 0
    assert INPROJ_ROWS % seq == 0 and dec_seq % INPROJ_ROWS == 0 and seq % RET_CHUNK == 0 and dec_seq % RET_CHUNK == 0
    assert dec_batch + 1 <= MOD_ROWS and dec_seq % GRID_W == 0
    assert HEAD_DIM == RET_DV and GROUP_WIDTH == RET_WIDTH

    consts = {
        "group_mean": _group_mean_matrix(GROUP_WIDTH, HEAD_DIM),
        "rope": _rope_tables(dec_seq),
        "ret_tables": _retention_tables(),
        "dft": {s: _dft_tables(s) for s in {seq, dec_seq}},
        "final_w": final_norm_w.reshape(1, D_MODEL),
        "w_in": w_in.astype(BF16),
        "w_out": w_out.astype(BF16),
        "w_up": w_up.astype(BF16),
        "w_down": w_down.astype(BF16),
    }

    cvec = jnp.zeros((MOD_ROWS, D_MODEL), F32).at[0].set(c_ctx).at[1:1 + dec_batch].set(c)
    mod = _modulation(cvec, w_mod, b_mod).reshape(DEPTH, MOD_ROWS, N_MOD, D_MODEL)

    cache_k = cache_attn_k.reshape(dec_batch, DEPTH, past, KV_WIDTH)
    cache_v = cache_attn_v.reshape(dec_batch, DEPTH, past, KV_WIDTH)

    xp = x_prompt.reshape(batch * seq, D_MODEL)
    xs = x_sample.reshape(dec_batch * dec_seq, D_MODEL)
    new_ctx = None
    for layer in range(DEPTH):
        lp = {
            "norm1_w": norm1_w[layer].reshape(1, D_MODEL),
            "q_norm_w": jnp.tile(q_norm_w[layer], ATTN_HEADS).reshape(1, ATTN_WIDTH),
            "k_norm_w": jnp.tile(k_norm_w[layer], ATTN_KV_HEADS).reshape(1, KV_WIDTH),
            "norm2_w": norm2_w[layer].reshape(1, D_MODEL),
            "conv_w": conv_w[layer],
            "conv_b": conv_b[layer].reshape(1, 2 * D_FF),
        }
        final = layer == DEPTH - 1
        xp, new_ctx = _layer(xp, mod[layer, 0:1], lp, consts, layer, batch, seq, None, new_ctx, final)
        xs, _ = _layer(xs, mod[layer, 1:1 + dec_batch], lp, consts, layer, dec_batch, dec_seq,
                       (cache_k, cache_v, state_ret), None, final)
    new_k, new_v, new_s = new_ctx
    return (xp.reshape(batch, seq, D_MODEL), xs.reshape(dec_batch, dec_seq, D_MODEL),
            new_k.reshape(batch, DEPTH, seq, ATTN_KV_HEADS, HEAD_DIM),
            new_v.reshape(batch, DEPTH, seq, ATTN_KV_HEADS, HEAD_DIM),
            new_s.reshape(batch, DEPTH, 2, RET_HEADS, RET_DK, RET_DV))
```

```python
import functools

import jax
import jax.numpy as jnp
import numpy as np
from jax import lax
from jax.experimental import pallas as pl
from jax.experimental.pallas import tpu as pltpu

F32 = jnp.float32
BF16 = jnp.bfloat16

D_MODEL = 1024
DEPTH = 2
GRID_W = 64
NORM_EPS = 1e-6
ATTN_HEADS = 8
ATTN_KV_HEADS = 2
HEAD_DIM = 64
ATTN_WIDTH = ATTN_HEADS * HEAD_DIM
KV_WIDTH = ATTN_KV_HEADS * HEAD_DIM
HEADS_PER_KV = ATTN_HEADS // ATTN_KV_HEADS
GROUP_WIDTH = HEADS_PER_KV * HEAD_DIM
ATTN_LOGIT_SCALE = HEAD_DIM ** -0.5 * 1.4426950408889634
ROPE_THETA = 10000.0
ROPE_AXIS_DIM = HEAD_DIM // 2
ROPE_HALF = ROPE_AXIS_DIM // 2
RET_HEADS = 4
RET_DK = 64
RET_DV = 64
RET_WIDTH = RET_HEADS * RET_DV
RET_CHUNK = 128
RET_DECAY_EXP_FWD = 5.0
RET_DECAY_EXP_BWD = 5.5
FOURIER_GROUPS = 4
FOURIER_DIM = 64
FOURIER_WIDTH = FOURIER_GROUPS * FOURIER_DIM
D_FF = 2816
N_MOD = 6

OFF_Q = 0
OFF_K = OFF_Q + ATTN_WIDTH
OFF_V = OFF_K + KV_WIDTH
OFF_RQ = OFF_V + KV_WIDTH
OFF_RK = OFF_RQ + RET_WIDTH
OFF_RV = OFF_RK + RET_WIDTH
OFF_GF = OFF_RV + RET_WIDTH
OFF_GB = OFF_GF + RET_WIDTH
OFF_FX = OFF_GB + RET_WIDTH
IN_WIDTH = OFF_FX + FOURIER_WIDTH
MIX_OFF_ATTN = 0
MIX_OFF_RET = MIX_OFF_ATTN + ATTN_WIDTH
MIX_OFF_FOURIER = MIX_OFF_RET + RET_WIDTH
MIX_WIDTH = MIX_OFF_FOURIER + FOURIER_WIDTH

V7X_VMEM_BYTES = 64 * 1024 * 1024
VMEM_LIMIT = V7X_VMEM_BYTES - 8 * 1024 * 1024

MOD_ROWS = 16
MOD_COLS = 1536
INPROJ_ROWS = 1024
ATTN_Q_ROWS = 1024
ATTN_LOGIT_ELEMS = 1024 * 1024
FFN_ROWS = 512
F32_SUBLANES = 8
BF16_SUBLANES = 16


def _params(*semantics):
    return pltpu.CompilerParams(dimension_semantics=semantics, vmem_limit_bytes=VMEM_LIMIT)


def _dot(a, b):
    return jnp.dot(a, b, preferred_element_type=F32)


def _group_mean(x, gm):
    xb = x.astype(BF16)
    width, slab = x.shape[1], gm.shape[0]
    if width <= slab:
        return _dot(xb, gm[0:width, 0:width])
    return jnp.concatenate([_dot(xb[:, s:s + slab], gm) for s in range(0, width, slab)], axis=1)


def _dot_nt(a, b):
    return lax.dot_general(a, b, (((1,), (1,)), ((), ())), preferred_element_type=F32)


def _sigmoid(x):
    return 1.0 / (1.0 + jnp.exp(-x))


def _silu(x):
    return x * _sigmoid(x)


def _rms_rows(x, w):
    ms = jnp.mean(x * x, axis=-1, keepdims=True)
    return x * lax.rsqrt(ms + NORM_EPS) * w


def _lane_block(shape, width):
    return lax.broadcasted_iota(jnp.int32, shape, len(shape) - 1) // width


def _group_mean_matrix(width, group):
    idx = np.arange(width) // group
    return jnp.asarray((idx[:, None] == idx[None, :]).astype(np.float32) / group, dtype=BF16)


def _rope_tables(n_tokens):
    pos = np.arange(n_tokens)
    row = (pos // GRID_W).astype(np.float64)
    col = (pos % GRID_W).astype(np.float64)
    freqs = ROPE_THETA ** (-np.arange(ROPE_HALF, dtype=np.float64) / ROPE_HALF)
    d = np.arange(HEAD_DIM)
    coord = np.where((d // ROPE_AXIS_DIM)[None, :] == 0, row[:, None], col[:, None])
    ang = coord * freqs[d % ROPE_HALF][None, :]
    sign = np.where((d & ROPE_HALF) == 0, -1.0, 1.0)[None, :]
    cos = np.tile(np.cos(ang), (1, ATTN_HEADS))
    sin = np.tile(np.sin(ang) * sign, (1, ATTN_HEADS))
    return jnp.asarray(cos, F32), jnp.asarray(sin, F32)


def _dft_tables(seq):
    n = np.arange(seq)
    ang = 2.0 * np.pi * ((n[:, None] * n[None, :]) % seq) / seq
    scale = 1.0 / np.sqrt(seq * FOURIER_DIM)
    position = np.concatenate([np.cos(ang), -np.sin(ang)], axis=1)
    c = np.arange(FOURIER_WIDTH)
    same = (c[:, None] // FOURIER_DIM) == (c[None, :] // FOURIER_DIM)
    angc = 2.0 * np.pi * (((c % FOURIER_DIM)[:, None] * (c % FOURIER_DIM)[None, :]) % FOURIER_DIM) / FOURIER_DIM
    channel = np.concatenate([np.where(same, np.cos(angc), 0.0), np.where(same, np.sin(angc), 0.0)], axis=1)
    as_bf16 = lambda a: jnp.asarray(a, F32).astype(BF16)
    return as_bf16(channel), as_bf16(position * scale)


def _retention_tables():
    heads = jnp.arange(RET_HEADS, dtype=F32)
    idx = jnp.arange(RET_CHUNK, dtype=F32)
    diff = idx[:, None] - idx[None, :]
    out = []
    for exp0, backward in ((RET_DECAY_EXP_FWD, False), (RET_DECAY_EXP_BWD, True)):
        lg = jnp.log1p(-jnp.exp2(-(exp0 + heads)))
        dd = -diff if backward else diff
        inner = jnp.where(dd[None] >= 0, jnp.exp(jnp.maximum(dd, 0.0)[None] * lg[:, None, None]), 0.0)
        inner = inner.transpose(1, 0, 2).reshape(RET_CHUNK, RET_HEADS * RET_CHUNK)
        q_pow = (RET_CHUNK - idx) if backward else (idx + 1.0)
        k_pow = idx if backward else (RET_CHUNK - 1.0 - idx)
        spread = lambda p: jnp.repeat(jnp.exp(p[:, None] * lg[None, :]), RET_DK, axis=1)
        q_decay, k_decay = spread(q_pow), spread(k_pow)
        chunk_decay = jnp.repeat(jnp.exp(RET_CHUNK * lg), RET_DV)[None, :]
        out.append((inner, q_decay, k_decay, chunk_decay))
    (inner_f, qd_f, kd_f, cd_f), (inner_b, qd_b, kd_b, cd_b) = out
    return (inner_f, inner_b, jnp.concatenate([qd_f, qd_b], axis=1), jnp.concatenate([kd_f, kd_b], axis=1), cd_f, cd_b)


def _mod_kernel(c_ref, w_ref, b_ref, o_ref):
    act = _silu(c_ref[...]).astype(BF16)
    o_ref[...] = _dot(act, w_ref[...].astype(BF16)) + b_ref[...]


def _modulation(cvec, w_mod, b_mod):
    n_cols = w_mod.shape[-1]
    return pl.pallas_call(
        _mod_kernel,
        out_shape=jax.ShapeDtypeStruct((DEPTH, MOD_ROWS, n_cols), F32),
        grid=(DEPTH, n_cols // MOD_COLS),
        in_specs=[
            pl.BlockSpec((MOD_ROWS, D_MODEL), lambda l, j: (0, 0)),
            pl.BlockSpec((None, D_MODEL, MOD_COLS), lambda l, j: (l, 0, j)),
            pl.BlockSpec((None, 1, MOD_COLS), lambda l, j: (l, 0, j)),
        ],
        out_specs=pl.BlockSpec((None, MOD_ROWS, MOD_COLS), lambda l, j: (l, 0, j)),
        compiler_params=_params("parallel", "parallel"),
        name="modulation",
    )(cvec, w_mod, b_mod.reshape(DEPTH, 1, n_cols))


def _swap_rotary_pairs(x):
    width = x.shape[-1]
    lane = lax.broadcasted_iota(jnp.int32, x.shape, 1)
    from_below = pltpu.roll(x, ROPE_HALF, 1)
    from_above = pltpu.roll(x, width - ROPE_HALF, 1)
    return jnp.where((lane & ROPE_HALF) != 0, from_below, from_above)


def _inproj_kernel(*refs, latent, layer, starts_cache_output):
    x_ref, mod_ref, n1_ref, win_ref, qnw_ref, knw_ref, gm_ref = refs[:7]
    refs = refs[7:]
    if latent:
        cos_ref, sin_ref = refs[:2]
        refs = refs[2:]
        q_ref, k_ref, v_ref, rq_ref, rk_ref, rv_ref, g_ref, fx_ref = refs
    else:
        q_ref, k_ref, v_ref, rq_ref, rk_ref, rv_ref, g_ref, fx_ref, k32_ref, v32_ref = refs[-10:]

    shift, scale = mod_ref[0:1, :], mod_ref[1:2, :]
    h = _rms_rows(x_ref[...], n1_ref[...]) * (1.0 + scale) + shift
    all_proj = _dot(h.astype(BF16), win_ref[...])
    proj = lambda off, width: all_proj[:, off:off + width]

    q = proj(OFF_Q, ATTN_WIDTH)
    q = q * lax.rsqrt(_group_mean(q * q, gm_ref[...]) + NORM_EPS) * qnw_ref[...]
    k = proj(OFF_K, KV_WIDTH)
    k = k * lax.rsqrt(_group_mean(k * k, gm_ref[...]) + NORM_EPS) * knw_ref[...]
    v = proj(OFF_V, KV_WIDTH)
    if latent:
        cos, sin = cos_ref[...], sin_ref[...]
        q = q * cos + _swap_rotary_pairs(q) * sin
        k = k * cos[:, 0:KV_WIDTH] + _swap_rotary_pairs(k) * sin[:, 0:KV_WIDTH]
    else:
        for ref, val in ((k32_ref, k), (v32_ref, v)):
            if starts_cache_output:
                for other in range(ref.shape[1]):
                    if other != layer:
                        ref[:, other] = jnp.zeros((ref.shape[0],) + ref.shape[2:], F32)
                ref[:, layer] = val.reshape((ref.shape[0],) + ref.shape[2:])
            else:
                ref[...] = val.reshape(ref.shape)
    q_ref[...] = (q * ATTN_LOGIT_SCALE).astype(BF16)
    k_ref[...] = k.astype(BF16)
    v_ref[...] = v.astype(BF16)
    rq_ref[...] = proj(OFF_RQ, RET_WIDTH).astype(BF16)
    rk_ref[...] = (proj(OFF_RK, RET_WIDTH) * RET_DK ** -0.5).astype(BF16)
    rv_ref[...] = proj(OFF_RV, RET_WIDTH).astype(BF16)
    g_ref[...] = proj(OFF_GF, 2 * RET_WIDTH)
    fx_ref[...] = proj(OFF_FX, FOURIER_WIDTH).astype(BF16)


def _in_projection(x, mod, norm1_w, w_in, q_norm_w, k_norm_w, group_mean, rope, new_cache, layer, seq):
    n = x.shape[0]
    rows = INPROJ_ROWS
    latent = rope is not None
    row_blk = lambda width: pl.BlockSpec((rows, width), lambda i: (i, 0))
    whole = lambda shape: pl.BlockSpec(shape, lambda i: (0,) * len(shape))
    mod_idx = (lambda i: (i * rows // seq, 0, 0)) if latent else (lambda i: (0, 0, 0))
    in_specs = [
        row_blk(D_MODEL),
        pl.BlockSpec((None, N_MOD, D_MODEL), mod_idx),
        whole((1, D_MODEL)),
        pl.BlockSpec((None, D_MODEL, IN_WIDTH), lambda i: (layer, 0, 0), pipeline_mode=pl.Buffered(1)),
        whole((1, ATTN_WIDTH)),
        whole((1, KV_WIDTH)),
        whole(group_mean.shape),
    ]
    args = [x, mod, norm1_w, w_in, q_norm_w, k_norm_w, group_mean]
    outs = [(ATTN_WIDTH, BF16), (KV_WIDTH, BF16), (KV_WIDTH, BF16), (RET_WIDTH, BF16), (RET_WIDTH, BF16),
            (RET_WIDTH, BF16), (2 * RET_WIDTH, F32), (FOURIER_WIDTH, BF16)]
    out_shape = [jax.ShapeDtypeStruct((n, w), dt) for w, dt in outs]
    out_specs = [row_blk(w) for w, _ in outs]
    aliases = {}
    if latent:
        pos_blk = pl.BlockSpec((rows, ATTN_WIDTH), lambda i: (i % (seq // rows), 0))
        in_specs += [pos_blk, pos_blk]
        args += list(rope)
    else:
        seqs = rows // seq
        out_shape += [jax.ShapeDtypeStruct((n // seq, DEPTH, seq, KV_WIDTH), F32)] * 2
        if new_cache is None:
            out_specs += [pl.BlockSpec((seqs, DEPTH, seq, KV_WIDTH), lambda i: (i, 0, 0, 0))] * 2
        else:
            out_specs += [pl.BlockSpec((seqs, None, seq, KV_WIDTH), lambda i: (i, layer, 0, 0))] * 2
            for j, earlier in enumerate(new_cache):
                aliases[len(args)] = len(out_shape) - 2 + j
                in_specs.append(pl.BlockSpec(memory_space=pl.ANY))
                args.append(earlier)
    return pl.pallas_call(
        functools.partial(_inproj_kernel, latent=latent, layer=layer,
                          starts_cache_output=not latent and new_cache is None),
        out_shape=out_shape,
        grid=(n // rows,),
        in_specs=in_specs,
        out_specs=out_specs,
        input_output_aliases=aliases,
        compiler_params=_params("parallel"),
        name="in_projection_latent" if latent else "in_projection_context",
    )(*args)


def _spread_kv(x, group):
    lane = lax.broadcasted_iota(jnp.int32, x.shape, 1)
    other = pltpu.roll(x, HEAD_DIM, 1)
    own = (lane // HEAD_DIM) == group
    pair = jnp.where(own, x, other).astype(BF16)
    return jnp.concatenate([pair, pair], axis=1)


def _attention_kernel(*refs, past, heads_per_dot):
    if past:
        q_ref, k_ref, v_ref, ck_ref, cv_ref, o_ref, kt_ref, vt_ref = refs
    else:
        q_ref, k_ref, v_ref, o_ref, kt_ref, vt_ref = refs

    @pl.when(pl.program_id(1) == 0)
    def _():
        for g in range(ATTN_KV_HEADS):
            if past:
                kt_ref[g, 0:past, :] = _spread_kv(ck_ref[...], g)
                vt_ref[g, 0:past, :] = _spread_kv(cv_ref[...], g)
            kt_ref[g, past:, :] = _spread_kv(k_ref[...].astype(F32), g)
            vt_ref[g, past:, :] = _spread_kv(v_ref[...].astype(F32), g)

    rows = q_ref.shape[0]
    block = _lane_block((rows, GROUP_WIDTH), HEAD_DIM)

    def logits(unit):
        g, first = divmod(unit * heads_per_dot, HEADS_PER_KV)
        qg = q_ref[:, g * GROUP_WIDTH:(g + 1) * GROUP_WIDTH]
        zero = jnp.zeros_like(qg)
        stacked = jnp.concatenate([jnp.where(block == first + j, qg, zero) for j in range(heads_per_dot)], axis=0)
        return _dot_nt(stacked, kt_ref[g])

    n_units = ATTN_HEADS // heads_per_dot
    s_next = logits(0)
    out = None
    for unit in range(n_units):
        g, first = divmod(unit * heads_per_dot, HEADS_PER_KV)
        s = s_next
        if unit + 1 < n_units:
            s_next = logits(unit + 1)
        p = jnp.exp2(s - jnp.max(s, axis=-1, keepdims=True))
        denom = jnp.sum(p, axis=-1, keepdims=True)
        o = _dot(p.astype(BF16), vt_ref[g]) / denom
        for j in range(heads_per_dot):
            h = first + j
            piece = o[j * rows:(j + 1) * rows, :]
            out = piece if h == 0 else jnp.where(block == h, piece, out)
        if first + heads_per_dot == HEADS_PER_KV:
            o_ref[:, g * GROUP_WIDTH:(g + 1) * GROUP_WIDTH] = out.astype(BF16)


def _attention(q, k, v, cache, layer, batch, seq):
    n = q.shape[0]
    tq = min(ATTN_Q_ROWS, seq)
    nq = seq // tq
    past = 0 if cache is None else cache[0].shape[2]
    heads_per_dot = max(1, min(HEADS_PER_KV, ATTN_LOGIT_ELEMS // (tq * (past + seq))))
    assert HEADS_PER_KV % heads_per_dot == 0
    own_kv = pl.BlockSpec((seq, KV_WIDTH), lambda b, i: (b, 0))
    in_specs = [pl.BlockSpec((tq, ATTN_WIDTH), lambda b, i: (b * nq + i, 0)), own_kv, own_kv]
    args = [q, k, v]
    if past:
        cached_kv = pl.BlockSpec((None, None, past, KV_WIDTH), lambda b, i: (b, layer, 0, 0))
        in_specs += [cached_kv, cached_kv]
        args += list(cache)
    return pl.pallas_call(
        functools.partial(_attention_kernel, past=past, heads_per_dot=heads_per_dot),
        out_shape=jax.ShapeDtypeStruct((n, ATTN_WIDTH), BF16),
        grid=(batch, nq),
        in_specs=in_specs,
        out_specs=pl.BlockSpec((tq, ATTN_WIDTH), lambda b, i: (b * nq + i, 0)),
        scratch_shapes=[pltpu.VMEM((ATTN_KV_HEADS, past + seq, GROUP_WIDTH), BF16)] * 2,
        compiler_params=_params("parallel", "arbitrary"),
        name="attention_latent" if past else "attention_context",
    )(*args)


def _stack_heads(x):
    block = _lane_block(x.shape, RET_DK)
    zero = jnp.zeros_like(x)
    return jnp.concatenate([jnp.where(block == h, x, zero) for h in range(RET_HEADS)], axis=-2)


def _head_norm(o, gm):
    d = o - _group_mean(o, gm)
    return d * lax.rsqrt(_group_mean(d * d, gm) + NORM_EPS)


def _retention_kernel(*refs, has_state, n_chunks, layer, starts_state_output):
    (rq_ref, rk_ref, rv_ref, g_ref, gm_ref, df_ref, db_ref, qd_ref, kd_ref, cdf_ref, cdb_ref,
     fx_ref, dft_c_ref, dft_p_ref) = refs[:14]
    refs = refs[14:]
    if has_state:
        s0_ref, o_ref, both_ref, upd_ref, st_ref = refs
    else:
        o_ref, sfin_ref, both_ref, upd_ref, st_ref = refs[-5:]
        if starts_state_output:
            for other in range(sfin_ref.shape[0]):
                if other != layer:
                    sfin_ref[other] = jnp.zeros(sfin_ref.shape[1:], F32)
            sfin_ref = sfin_ref.at[layer]

    width = RET_WIDTH
    along = _dot(fx_ref[...], dft_c_ref[...])
    stacked = jnp.concatenate([along[:, 0:FOURIER_WIDTH], along[:, FOURIER_WIDTH:]], axis=0).astype(BF16)
    o_ref[:, width:] = _dot(dft_p_ref[...], stacked).astype(BF16)

    diag = (lax.broadcasted_iota(jnp.int32, (width, width), 0) // RET_DK
            == lax.broadcasted_iota(jnp.int32, (width, width), 1) // RET_DV)

    chunked = lambda ref: ref[...].reshape(n_chunks, RET_CHUNK, width)
    batched = lambda a, b, contract: lax.dot_general(a, b, (contract, ((0,), (0,))), preferred_element_type=F32)
    q3, k3, v3 = chunked(rq_ref), chunked(rk_ref), chunked(rv_ref)
    scores = batched(q3, _stack_heads(k3), ((2,), (2,)))
    v_heads = _stack_heads(v3)
    seq_rows = n_chunks * RET_CHUNK
    both_ref[:, 0:width] = batched((scores * df_ref[...]).astype(BF16), v_heads, ((2,), (1,))).reshape(seq_rows, width)
    both_ref[:, width:] = batched((scores * db_ref[...]).astype(BF16), v_heads, ((2,), (1,))).reshape(seq_rows, width)
    k32 = k3.astype(F32)
    k_decayed = jnp.concatenate([k32, k32], axis=2) * kd_ref[...]
    upd_ref[...] = batched(jnp.swapaxes(k_decayed, 1, 2).astype(BF16), v3, ((2,), (1,)))

    for direction, order, cd_ref in ((0, range(n_chunks), cdf_ref), (1, reversed(range(n_chunks)), cdb_ref)):
        if has_state:
            s = s0_ref[direction].reshape(width, RET_DV)
            state = jnp.where(diag, jnp.concatenate([s] * RET_HEADS, axis=1), 0.0)
        else:
            state = jnp.zeros((width, width), F32)
        for c in order:
            st_ref[direction, c] = state.astype(BF16)
            update = upd_ref[c, direction * width:(direction + 1) * width, :]
            state = cd_ref[...] * state + jnp.where(diag, update, 0.0)
        if not has_state:
            folded = state[:, 0:width // 2] + state[:, width // 2:]
            sfin_ref[direction] = folded[:, 0:RET_DV] + folded[:, RET_DV:]

    q32 = q3.astype(F32)
    q_decayed = (jnp.concatenate([q32, q32], axis=2) * qd_ref[...]).astype(BF16)
    both_ref[:, 0:width] += batched(q_decayed[:, :, 0:width], st_ref[0], ((2,), (1,))).reshape(seq_rows, width)
    both_ref[:, width:] += batched(q_decayed[:, :, width:], st_ref[1], ((2,), (1,))).reshape(seq_rows, width)

    gated = _silu(g_ref[...]) * _head_norm(both_ref[...], gm_ref[...])
    o_ref[:, 0:width] = (gated[:, 0:width] + gated[:, width:]).astype(BF16)


def _retention_fourier(rq, rk, rv, gates, fx, group_mean, tables, dft, state, new_state, layer, batch, seq):
    n = rq.shape[0]
    has_state = state is not None
    n_chunks = seq // RET_CHUNK
    seq_blk = lambda width: pl.BlockSpec((seq, width), lambda b: (b, 0))
    whole = lambda a: pl.BlockSpec(a.shape, lambda b: (0,) * a.ndim)
    in_specs = ([seq_blk(RET_WIDTH)] * 3 + [seq_blk(2 * RET_WIDTH), whole(group_mean)] + [whole(t) for t in tables]
                + [seq_blk(FOURIER_WIDTH), whole(dft[0]), whole(dft[1])])
    args = [rq, rk, rv, gates, group_mean] + list(tables) + [fx, dft[0], dft[1]]
    aliases = {}
    out_shape = [jax.ShapeDtypeStruct((n, RET_WIDTH + FOURIER_WIDTH), BF16)]
    out_specs = [seq_blk(RET_WIDTH + FOURIER_WIDTH)]
    if has_state:
        in_specs.append(pl.BlockSpec((None, None, 2, RET_HEADS, RET_DK, RET_DV), lambda b: (b, layer, 0, 0, 0, 0)))
        args.append(state)
    else:
        state_blk = (2, RET_HEADS * RET_DK, RET_DV)
        out_shape.append(jax.ShapeDtypeStruct((batch, DEPTH) + state_blk, F32))
        if new_state is None:
            out_specs.append(pl.BlockSpec((None, DEPTH) + state_blk, lambda b: (b, 0, 0, 0, 0)))
        else:
            out_specs.append(pl.BlockSpec((None, None) + state_blk, lambda b: (b, layer, 0, 0, 0)))
            aliases[len(args)] = 1
            in_specs.append(pl.BlockSpec(memory_space=pl.ANY))
            args.append(new_state)
    return pl.pallas_call(
        functools.partial(_retention_kernel, has_state=has_state, n_chunks=n_chunks, layer=layer,
                          starts_state_output=new_state is None),
        out_shape=out_shape,
        grid=(batch,),
        in_specs=in_specs,
        out_specs=out_specs,
        input_output_aliases=aliases,
        scratch_shapes=[pltpu.VMEM((seq, 2 * RET_WIDTH), F32),
                        pltpu.VMEM((n_chunks, 2 * RET_WIDTH, RET_WIDTH), F32),
                        pltpu.VMEM((2, n_chunks, RET_WIDTH, RET_WIDTH), BF16)],
        compiler_params=_params("parallel"),
        name="retention_fourier_latent" if has_state else "retention_fourier_context",
    )(*args)


def _shift_rows(u, seq):
    rows = u.shape[0]
    pos = lax.broadcasted_iota(jnp.int32, u.shape, 0) % seq
    prev = jnp.where(pos == 0, 0.0, pltpu.roll(u, 1, 0))
    nxt = jnp.where(pos == seq - 1, 0.0, pltpu.roll(u, rows - 1, 0))
    return prev, nxt


def _mix_ffn_kernel(*refs, groups, final):
    n_weights = 7
    weights = refs[-n_weights - len(groups):-len(groups)]
    outs = refs[-len(groups):]
    tile = pl.program_id(0)
    pos = 0
    for (first, tiles, seq, halo), o_ref in zip(groups, outs):
        n_rows_refs = 9 if halo else 3
        row_refs, mod_ref = refs[pos:pos + n_rows_refs], refs[pos + n_rows_refs]
        pos += n_rows_refs + 1

        @pl.when((tile >= first) & (tile < first + tiles))
        def _(row_refs=row_refs, mod_ref=mod_ref, o_ref=o_ref, first=first, seq=seq, halo=halo):
            _mix_ffn_tile(row_refs, mod_ref, weights, o_ref, tile - first, seq=seq, halo=halo, final=final)


def _mix_ffn_tile(row_refs, mod_ref, weights, o_ref, tile, *, seq, halo, final):
    wout_ref, n2_ref, wup_ref, cw_ref, cb_ref, wd_ref, fin_ref = weights
    if halo:
        x_ref, x_top_ref, x_bot_ref, attn_ref, attn_top_ref, attn_bot_ref, rf_ref, rf_top_ref, rf_bot_ref = row_refs
    else:
        x_ref, attn_ref, rf_ref = row_refs
    rows = x_ref.shape[0]
    modulate = lambda x: _rms_rows(x, n2_ref[...]) * (1.0 + mod_ref[4:5, :]) + mod_ref[3:4, :]
    project_out = lambda attn, rf: (_dot(attn, wout_ref[MIX_OFF_ATTN:MIX_OFF_RET, :])
                                    + _dot(rf, wout_ref[MIX_OFF_RET:MIX_WIDTH, :]))
    if halo:
        pad = x_top_ref.shape[0]
        skip = attn_top_ref.shape[0] - pad
        tiles_per_seq = seq // rows
        place = tile % tiles_per_seq
        mixed = project_out(jnp.concatenate([attn_top_ref[...], attn_ref[...], attn_bot_ref[...]], axis=0),
                            jnp.concatenate([rf_top_ref[...], rf_ref[...], rf_bot_ref[...]], axis=0))
        x_ext = jnp.concatenate([x_top_ref[...], x_ref[...], x_bot_ref[...]], axis=0)
        x1_ext = x_ext + mod_ref[2:3, :] * mixed[skip:skip + pad + rows + pad, :]
        x1 = x1_ext[pad:pad + rows, :]
        h2 = modulate(x1_ext)
        row = lax.broadcasted_iota(jnp.int32, h2.shape, 0)
        first_kept = jnp.where(place == 0, pad, 0)
        end_kept = jnp.where(place == tiles_per_seq - 1, pad + rows, pad + rows + pad)
        h2 = jnp.where((row >= first_kept) & (row < end_kept), h2, 0.0)
        u = _dot(h2.astype(BF16), wup_ref[...])
        prev, nxt = pltpu.roll(u, 1, 0), pltpu.roll(u, u.shape[0] - 1, 0)
    else:
        x1 = x_ref[...] + mod_ref[2:3, :] * project_out(attn_ref[...], rf_ref[...])
        u = _dot(modulate(x1).astype(BF16), wup_ref[...])
        prev, nxt = _shift_rows(u, seq)
    u = prev * cw_ref[0:1, :] + u * cw_ref[1:2, :] + nxt * cw_ref[2:3, :] + cb_ref[...]
    if halo:
        u = u[pad:pad + rows, :]
    act = _silu(u[:, 0:D_FF]) * u[:, D_FF:]
    y = x1 + mod_ref[5:6, :] * _dot(act.astype(BF16), wd_ref[...])
    o_ref[...] = _rms_rows(y, fin_ref[...]) if final else y


def _mix_ffn(token_groups, w_out, norm2_w, w_up, conv_w, conv_b, w_down, final_w, layer, final):
    rows = FFN_ROWS
    whole = lambda shape: pl.BlockSpec(shape, lambda i: (0,) * len(shape))
    resident = lambda r, c: pl.BlockSpec((None, r, c), lambda i: (layer, 0, 0), pipeline_mode=pl.Buffered(1))
    in_specs, args, out_specs, out_shape, groups = [], [], [], [], []
    first = 0
    for x, attn, ret_four, mod, seq, per_seq_mod in token_groups:
        n = x.shape[0]
        tiles = n // rows
        halo = seq > rows
        assert n % rows == 0 and ((seq % rows == 0) if halo else (rows % seq == 0))
        local = lambda i, first=first, tiles=tiles: jnp.clip(i - first, 0, tiles - 1)
        row_blk = lambda width, local=local: pl.BlockSpec((rows, width), lambda i: (local(i), 0))

        def with_halo(a, pad, local=local, n=n):
            width = a.shape[1]
            per_tile, last = rows // pad, n // pad - 1
            return ([pl.BlockSpec((rows, width), lambda i: (local(i), 0)),
                     pl.BlockSpec((pad, width), lambda i: (jnp.maximum(local(i) * per_tile - 1, 0), 0)),
                     pl.BlockSpec((pad, width), lambda i: (jnp.minimum((local(i) + 1) * per_tile, last), 0))],
                    [a, a, a])

        for a, pad in ((x, F32_SUBLANES), (attn, BF16_SUBLANES), (ret_four, BF16_SUBLANES)):
            specs, arrays = with_halo(a, pad) if halo else ([row_blk(a.shape[1])], [a])
            in_specs += specs
            args += arrays
        mod_idx = ((lambda i, local=local, seq=seq: (local(i) * rows // seq, 0, 0)) if per_seq_mod
                   else (lambda i: (0, 0, 0)))
        in_specs.append(pl.BlockSpec((None, N_MOD, D_MODEL), mod_idx))
        args.append(mod)
        out_specs.append(row_blk(D_MODEL))
        out_shape.append(jax.ShapeDtypeStruct((n, D_MODEL), F32))
        groups.append((first, tiles, seq, halo))
        first += tiles
    in_specs += [resident(MIX_WIDTH, D_MODEL), whole((1, D_MODEL)),
                 resident(D_MODEL, 2 * D_FF), whole((3, 2 * D_FF)), whole((1, 2 * D_FF)),
                 resident(D_FF, D_MODEL), whole((1, D_MODEL))]
    args += [w_out, norm2_w, w_up, conv_w, conv_b, w_down, final_w]
    return pl.pallas_call(
        functools.partial(_mix_ffn_kernel, groups=tuple(groups), final=final),
        out_shape=out_shape,
        grid=(first,),
        in_specs=in_specs,
        out_specs=out_specs,
        compiler_params=_params("arbitrary"),
        name="mix_ffn",
    )(*args)


def _mixers(x, mod, lp, consts, layer, batch, seq, ctx, new_ctx):
    latent = ctx is not None
    outs = _in_projection(x, mod, lp["norm1_w"], consts["w_in"], lp["q_norm_w"], lp["k_norm_w"], consts["group_mean"],
                          consts["rope"] if latent else None, None if new_ctx is None else new_ctx[:2], layer, seq)
    q, k, v, rq, rk, rv, gates, fx = outs[:8]
    attn = _attention(q, k, v, ctx[:2] if latent else None, layer, batch, seq)
    ret_out = _retention_fourier(rq, rk, rv, gates, fx, consts["group_mean"], consts["ret_tables"], consts["dft"][seq],
                                 ctx[2] if latent else None, None if new_ctx is None else new_ctx[2],
                                 layer, batch, seq)
    return attn, ret_out[0], (None if latent else (outs[8], outs[9], ret_out[1]))


def kernel(x_prompt, x_sample, c, cache_attn_k, cache_attn_v, state_ret, c_ctx, w_mod, b_mod, norm1_w, w_in,
           q_norm_w, k_norm_w, w_out, norm2_w, w_up, conv_w, conv_b, w_down, final_norm_w):
    batch, seq, d = x_prompt.shape
    dec_batch, dec_seq, _ = x_sample.shape
    past = cache_attn_k.shape[2]
    assert d == D_MODEL and w_in.shape == (DEPTH, D_MODEL, IN_WIDTH) and w_up.shape == (DEPTH, D_MODEL, 2 * D_FF)
    assert (batch * seq) % FFN_ROWS == 0 and dec_seq % FFN_ROWS == 0
    assert seq % min(ATTN_Q_ROWS, seq) == 0 and dec_seq % min(ATTN_Q_ROWS, dec_seq) == 0
    assert INPROJ_ROWS % seq == 0 and dec_seq % INPROJ_ROWS == 0 and seq % RET_CHUNK == 0 and dec_seq % RET_CHUNK == 0
    assert dec_batch + 1 <= MOD_ROWS and dec_seq % GRID_W == 0
    assert HEAD_DIM == RET_DV and GROUP_WIDTH == RET_WIDTH

    consts = {
        "group_mean": _group_mean_matrix(GROUP_WIDTH, HEAD_DIM),
        "rope": _rope_tables(dec_seq),
        "ret_tables": _retention_tables(),
        "dft": {s: _dft_tables(s) for s in {seq, dec_seq}},
        "final_w": final_norm_w.reshape(1, D_MODEL),
        "w_in": w_in.astype(BF16),
        "w_out": w_out.astype(BF16),
        "w_up": w_up.astype(BF16),
        "w_down": w_down.astype(BF16),
    }

    cvec = jnp.zeros((MOD_ROWS, D_MODEL), F32).at[0].set(c_ctx).at[1:1 + dec_batch].set(c)
    mod = _modulation(cvec, w_mod, b_mod).reshape(DEPTH, MOD_ROWS, N_MOD, D_MODEL)

    cache_k = cache_attn_k.reshape(dec_batch, DEPTH, past, KV_WIDTH)
    cache_v = cache_attn_v.reshape(dec_batch, DEPTH, past, KV_WIDTH)

    xp = x_prompt.reshape(batch * seq, D_MODEL)
    xs = x_sample.reshape(dec_batch * dec_seq, D_MODEL)
    new_ctx = None
    for layer in range(DEPTH):
        lp = {
            "norm1_w": norm1_w[layer].reshape(1, D_MODEL),
            "q_norm_w": jnp.tile(q_norm_w[layer], ATTN_HEADS).reshape(1, ATTN_WIDTH),
            "k_norm_w": jnp.tile(k_norm_w[layer], ATTN_KV_HEADS).reshape(1, KV_WIDTH),
            "norm2_w": norm2_w[layer].reshape(1, D_MODEL),
            "conv_w": conv_w[layer],
            "conv_b": conv_b[layer].reshape(1, 2 * D_FF),
        }
        mod_p, mod_s = mod[layer, 0:1], mod[layer, 1:1 + dec_batch]
        attn_p, rf_p, new_ctx = _mixers(xp, mod_p, lp, consts, layer, batch, seq, None, new_ctx)
        attn_s, rf_s, _ = _mixers(xs, mod_s, lp, consts, layer, dec_batch, dec_seq, (cache_k, cache_v, state_ret), None)
        xp, xs = _mix_ffn(((xp, attn_p, rf_p, mod_p, seq, False), (xs, attn_s, rf_s, mod_s, dec_seq, True)),
                          consts["w_out"], lp["norm2_w"], consts["w_up"], lp["conv_w"], lp["conv_b"], consts["w_down"],
                          consts["final_w"], layer, layer == DEPTH - 1)
    new_k, new_v, new_s = new_ctx
    return (xp.reshape(batch, seq, D_MODEL), xs.reshape(dec_batch, dec_seq, D_MODEL),
            new_k.reshape(batch, DEPTH, seq, ATTN_KV_HEADS, HEAD_DIM),
            new_v.reshape(batch, DEPTH, seq, ATTN_KV_HEADS, HEAD_DIM),
            new_s.reshape(batch, DEPTH, 2, RET_HEADS, RET_DK, RET_DV))
```

```python
import functools

import jax
import jax.numpy as jnp
import numpy as np
from jax import lax
from jax.experimental import pallas as pl
from jax.experimental.pallas import tpu as pltpu

F32 = jnp.float32
BF16 = jnp.bfloat16

D_MODEL = 1024
DEPTH = 2
GRID_W = 64
NORM_EPS = 1e-6
ATTN_HEADS = 8
ATTN_KV_HEADS = 2
HEAD_DIM = 64
ATTN_WIDTH = ATTN_HEADS * HEAD_DIM
KV_WIDTH = ATTN_KV_HEADS * HEAD_DIM
HEADS_PER_KV = ATTN_HEADS // ATTN_KV_HEADS
GROUP_WIDTH = HEADS_PER_KV * HEAD_DIM
ATTN_LOGIT_SCALE = HEAD_DIM ** -0.5 * 1.4426950408889634
ROPE_THETA = 10000.0
ROPE_AXIS_DIM = HEAD_DIM // 2
ROPE_HALF = ROPE_AXIS_DIM // 2
RET_HEADS = 4
RET_DK = 64
RET_DV = 64
RET_WIDTH = RET_HEADS * RET_DV
RET_CHUNK = 128
RET_DECAY_EXP_FWD = 5.0
RET_DECAY_EXP_BWD = 5.5
FOURIER_GROUPS = 4
FOURIER_DIM = 64
FOURIER_WIDTH = FOURIER_GROUPS * FOURIER_DIM
D_FF = 2816
N_MOD = 6

OFF_Q = 0
OFF_K = OFF_Q + ATTN_WIDTH
OFF_V = OFF_K + KV_WIDTH
OFF_RQ = OFF_V + KV_WIDTH
OFF_RK = OFF_RQ + RET_WIDTH
OFF_RV = OFF_RK + RET_WIDTH
OFF_GF = OFF_RV + RET_WIDTH
OFF_GB = OFF_GF + RET_WIDTH
OFF_FX = OFF_GB + RET_WIDTH
IN_WIDTH = OFF_FX + FOURIER_WIDTH
MIX_OFF_ATTN = 0
MIX_OFF_RET = MIX_OFF_ATTN + ATTN_WIDTH
MIX_OFF_FOURIER = MIX_OFF_RET + RET_WIDTH
MIX_WIDTH = MIX_OFF_FOURIER + FOURIER_WIDTH

V7X_VMEM_BYTES = 64 * 1024 * 1024
VMEM_LIMIT = V7X_VMEM_BYTES - 8 * 1024 * 1024

MOD_ROWS = 16
MOD_COLS = 1536
INPROJ_ROWS = 1024
ATTN_Q_ROWS = 1024
ATTN_LOGIT_ELEMS = 1024 * 1024
FFN_ROWS = 512
F32_SUBLANES = 8
BF16_SUBLANES = 16


def _params(*semantics):
    return pltpu.CompilerParams(dimension_semantics=semantics, vmem_limit_bytes=VMEM_LIMIT)


def _dot(a, b):
    return jnp.dot(a, b, preferred_element_type=F32)


def _group_mean(x, gm):
    xb = x.astype(BF16)
    width, slab = x.shape[1], gm.shape[0]
    if width <= slab:
        return _dot(xb, gm[0:width, 0:width])
    return jnp.concatenate([_dot(xb[:, s:s + slab], gm) for s in range(0, width, slab)], axis=1)


def _dot_nt(a, b):
    return lax.dot_general(a, b, (((1,), (1,)), ((), ())), preferred_element_type=F32)


def _sigmoid(x):
    return 1.0 / (1.0 + jnp.exp(-x))


def _silu(x):
    return x * _sigmoid(x)


def _rms_rows(x, w):
    ms = jnp.mean(x * x, axis=-1, keepdims=True)
    return x * lax.rsqrt(ms + NORM_EPS) * w


def _lane_block(shape, width):
    return lax.broadcasted_iota(jnp.int32, shape, len(shape) - 1) // width


def _group_mean_matrix(width, group):
    idx = np.arange(width) // group
    return jnp.asarray((idx[:, None] == idx[None, :]).astype(np.float32) / group, dtype=BF16)


def _rope_tables(n_tokens):
    pos = np.arange(n_tokens)
    row = (pos // GRID_W).astype(np.float64)
    col = (pos % GRID_W).astype(np.float64)
    freqs = ROPE_THETA ** (-np.arange(ROPE_HALF, dtype=np.float64) / ROPE_HALF)
    d = np.arange(HEAD_DIM)
    coord = np.where((d // ROPE_AXIS_DIM)[None, :] == 0, row[:, None], col[:, None])
    ang = coord * freqs[d % ROPE_HALF][None, :]
    sign = np.where((d & ROPE_HALF) == 0, -1.0, 1.0)[None, :]
    cos = np.tile(np.cos(ang), (1, ATTN_HEADS))
    sin = np.tile(np.sin(ang) * sign, (1, ATTN_HEADS))
    return jnp.asarray(cos, F32), jnp.asarray(sin, F32)


def _dft_tables(seq):
    n = np.arange(seq)
    ang = 2.0 * np.pi * ((n[:, None] * n[None, :]) % seq) / seq
    scale = 1.0 / np.sqrt(seq * FOURIER_DIM)
    position = np.concatenate([np.cos(ang), -np.sin(ang)], axis=1)
    c = np.arange(FOURIER_WIDTH)
    same = (c[:, None] // FOURIER_DIM) == (c[None, :] // FOURIER_DIM)
    angc = 2.0 * np.pi * (((c % FOURIER_DIM)[:, None] * (c % FOURIER_DIM)[None, :]) % FOURIER_DIM) / FOURIER_DIM
    channel = np.concatenate([np.where(same, np.cos(angc), 0.0), np.where(same, np.sin(angc), 0.0)], axis=1)
    as_bf16 = lambda a: jnp.asarray(a, F32).astype(BF16)
    return as_bf16(channel), as_bf16(position * scale)


def _retention_tables():
    heads = jnp.arange(RET_HEADS, dtype=F32)
    idx = jnp.arange(RET_CHUNK, dtype=F32)
    diff = idx[:, None] - idx[None, :]
    out = []
    for exp0, backward in ((RET_DECAY_EXP_FWD, False), (RET_DECAY_EXP_BWD, True)):
        lg = jnp.log1p(-jnp.exp2(-(exp0 + heads)))
        dd = -diff if backward else diff
        inner = jnp.where(dd[None] >= 0, jnp.exp(jnp.maximum(dd, 0.0)[None] * lg[:, None, None]), 0.0)
        inner = inner.transpose(1, 0, 2).reshape(RET_CHUNK, RET_HEADS * RET_CHUNK)
        q_pow = (RET_CHUNK - idx) if backward else (idx + 1.0)
        k_pow = idx if backward else (RET_CHUNK - 1.0 - idx)
        spread = lambda p: jnp.repeat(jnp.exp(p[:, None] * lg[None, :]), RET_DK, axis=1)
        q_decay, k_decay = spread(q_pow), spread(k_pow)
        chunk_decay = jnp.repeat(jnp.exp(RET_CHUNK * lg), RET_DV)[None, :]
        out.append((inner, q_decay, k_decay, chunk_decay))
    (inner_f, qd_f, kd_f, cd_f), (inner_b, qd_b, kd_b, cd_b) = out
    return (inner_f, inner_b, jnp.concatenate([qd_f, qd_b], axis=1), jnp.concatenate([kd_f, kd_b], axis=1), cd_f, cd_b)


def _mod_kernel(c_ref, w_ref, b_ref, o_ref):
    act = _silu(c_ref[...]).astype(BF16)
    o_ref[...] = _dot(act, w_ref[...].astype(BF16)) + b_ref[...]


def _modulation(cvec, w_mod, b_mod):
    n_cols = w_mod.shape[-1]
    return pl.pallas_call(
        _mod_kernel,
        out_shape=jax.ShapeDtypeStruct((DEPTH, MOD_ROWS, n_cols), F32),
        grid=(DEPTH, n_cols // MOD_COLS),
        in_specs=[
            pl.BlockSpec((MOD_ROWS, D_MODEL), lambda l, j: (0, 0)),
            pl.BlockSpec((None, D_MODEL, MOD_COLS), lambda l, j: (l, 0, j)),
            pl.BlockSpec((None, 1, MOD_COLS), lambda l, j: (l, 0, j)),
        ],
        out_specs=pl.BlockSpec((None, MOD_ROWS, MOD_COLS), lambda l, j: (l, 0, j)),
        compiler_params=_params("parallel", "parallel"),
        name="modulation",
    )(cvec, w_mod, b_mod.reshape(DEPTH, 1, n_cols))


def _swap_rotary_pairs(x):
    width = x.shape[-1]
    lane = lax.broadcasted_iota(jnp.int32, x.shape, 1)
    from_below = pltpu.roll(x, ROPE_HALF, 1)
    from_above = pltpu.roll(x, width - ROPE_HALF, 1)
    return jnp.where((lane & ROPE_HALF) != 0, from_below, from_above)


def _inproj_kernel(*refs, latent, layer, starts_cache_output):
    x_ref, mod_ref, n1_ref, win_ref, qnw_ref, knw_ref, gm_ref = refs[:7]
    refs = refs[7:]
    if latent:
        cos_ref, sin_ref = refs[:2]
        refs = refs[2:]
        q_ref, k_ref, v_ref, rq_ref, rk_ref, rv_ref, g_ref, fx_ref = refs
    else:
        q_ref, k_ref, v_ref, rq_ref, rk_ref, rv_ref, g_ref, fx_ref, k32_ref, v32_ref = refs[-10:]

    shift, scale = mod_ref[0:1, :], mod_ref[1:2, :]
    h = _rms_rows(x_ref[...], n1_ref[...]) * (1.0 + scale) + shift
    all_proj = _dot(h.astype(BF16), win_ref[...])
    proj = lambda off, width: all_proj[:, off:off + width]

    q = proj(OFF_Q, ATTN_WIDTH)
    q = q * lax.rsqrt(_group_mean(q * q, gm_ref[...]) + NORM_EPS) * qnw_ref[...]
    k = proj(OFF_K, KV_WIDTH)
    k = k * lax.rsqrt(_group_mean(k * k, gm_ref[...]) + NORM_EPS) * knw_ref[...]
    v = proj(OFF_V, KV_WIDTH)
    if latent:
        cos, sin = cos_ref[...], sin_ref[...]
        q = q * cos + _swap_rotary_pairs(q) * sin
        k = k * cos[:, 0:KV_WIDTH] + _swap_rotary_pairs(k) * sin[:, 0:KV_WIDTH]
    else:
        for ref, val in ((k32_ref, k), (v32_ref, v)):
            if starts_cache_output:
                for other in range(ref.shape[1]):
                    if other != layer:
                        ref[:, other] = jnp.zeros((ref.shape[0],) + ref.shape[2:], F32)
                ref[:, layer] = val.reshape((ref.shape[0],) + ref.shape[2:])
            else:
                ref[...] = val.reshape(ref.shape)
    q_ref[...] = (q * ATTN_LOGIT_SCALE).astype(BF16)
    k_ref[...] = k.astype(BF16)
    v_ref[...] = v.astype(BF16)
    rq_ref[...] = proj(OFF_RQ, RET_WIDTH).astype(BF16)
    rk_ref[...] = (proj(OFF_RK, RET_WIDTH) * RET_DK ** -0.5).astype(BF16)
    rv_ref[...] = proj(OFF_RV, RET_WIDTH).astype(BF16)
    g_ref[...] = proj(OFF_GF, 2 * RET_WIDTH)
    fx_ref[...] = proj(OFF_FX, FOURIER_WIDTH).astype(BF16)


def _in_projection(x, mod, norm1_w, w_in, q_norm_w, k_norm_w, group_mean, rope, new_cache, layer, seq):
    n = x.shape[0]
    rows = INPROJ_ROWS
    latent = rope is not None
    row_blk = lambda width: pl.BlockSpec((rows, width), lambda i: (i, 0))
    whole = lambda shape: pl.BlockSpec(shape, lambda i: (0,) * len(shape))
    mod_idx = (lambda i: (i * rows // seq, 0, 0)) if latent else (lambda i: (0, 0, 0))
    in_specs = [
        row_blk(D_MODEL),
        pl.BlockSpec((None, N_MOD, D_MODEL), mod_idx),
        whole((1, D_MODEL)),
        pl.BlockSpec((None, D_MODEL, IN_WIDTH), lambda i: (layer, 0, 0), pipeline_mode=pl.Buffered(1)),
        whole((1, ATTN_WIDTH)),
        whole((1, KV_WIDTH)),
        whole(group_mean.shape),
    ]
    args = [x, mod, norm1_w, w_in, q_norm_w, k_norm_w, group_mean]
    outs = [(ATTN_WIDTH, BF16), (KV_WIDTH, BF16), (KV_WIDTH, BF16), (RET_WIDTH, BF16), (RET_WIDTH, BF16),
            (RET_WIDTH, BF16), (2 * RET_WIDTH, F32), (FOURIER_WIDTH, BF16)]
    out_shape = [jax.ShapeDtypeStruct((n, w), dt) for w, dt in outs]
    out_specs = [row_blk(w) for w, _ in outs]
    aliases = {}
    if latent:
        pos_blk = pl.BlockSpec((rows, ATTN_WIDTH), lambda i: (i % (seq // rows), 0))
        in_specs += [pos_blk, pos_blk]
        args += list(rope)
    else:
        seqs = rows // seq
        out_shape += [jax.ShapeDtypeStruct((n // seq, DEPTH, seq, KV_WIDTH), F32)] * 2
        if new_cache is None:
            out_specs += [pl.BlockSpec((seqs, DEPTH, seq, KV_WIDTH), lambda i: (i, 0, 0, 0))] * 2
        else:
            out_specs += [pl.BlockSpec((seqs, None, seq, KV_WIDTH), lambda i: (i, layer, 0, 0))] * 2
            for j, earlier in enumerate(new_cache):
                aliases[len(args)] = len(out_shape) - 2 + j
                in_specs.append(pl.BlockSpec(memory_space=pl.ANY))
                args.append(earlier)
    return pl.pallas_call(
        functools.partial(_inproj_kernel, latent=latent, layer=layer,
                          starts_cache_output=not latent and new_cache is None),
        out_shape=out_shape,
        grid=(n // rows,),
        in_specs=in_specs,
        out_specs=out_specs,
        input_output_aliases=aliases,
        compiler_params=_params("parallel"),
        name="in_projection_latent" if latent else "in_projection_context",
    )(*args)


def _spread_kv(x, group):
    lane = lax.broadcasted_iota(jnp.int32, x.shape, 1)
    other = pltpu.roll(x, HEAD_DIM, 1)
    own = (lane // HEAD_DIM) == group
    pair = jnp.where(own, x, other).astype(BF16)
    return jnp.concatenate([pair, pair], axis=1)


def _attention_kernel(*refs, past, heads_per_dot):
    if past:
        q_ref, k_ref, v_ref, ck_ref, cv_ref, o_ref, kt_ref, vt_ref = refs
    else:
        q_ref, k_ref, v_ref, o_ref, kt_ref, vt_ref = refs

    @pl.when(pl.program_id(1) == 0)
    def _():
        for g in range(ATTN_KV_HEADS):
            if past:
                kt_ref[g, 0:past, :] = jnp.concatenate([ck_ref[:, g, :]] * HEADS_PER_KV, axis=1).astype(BF16)
                vt_ref[g, 0:past, :] = jnp.concatenate([cv_ref[:, g, :]] * HEADS_PER_KV, axis=1).astype(BF16)
            kt_ref[g, past:, :] = _spread_kv(k_ref[...].astype(F32), g)
            vt_ref[g, past:, :] = _spread_kv(v_ref[...].astype(F32), g)

    rows = q_ref.shape[0]
    block = _lane_block((rows, GROUP_WIDTH), HEAD_DIM)

    def logits(unit):
        g, first = divmod(unit * heads_per_dot, HEADS_PER_KV)
        qg = q_ref[:, g * GROUP_WIDTH:(g + 1) * GROUP_WIDTH]
        zero = jnp.zeros_like(qg)
        stacked = jnp.concatenate([jnp.where(block == first + j, qg, zero) for j in range(heads_per_dot)], axis=0)
        return _dot_nt(stacked, kt_ref[g])

    n_units = ATTN_HEADS // heads_per_dot
    s_next = logits(0)
    out = None
    for unit in range(n_units):
        g, first = divmod(unit * heads_per_dot, HEADS_PER_KV)
        s = s_next
        if unit + 1 < n_units:
            s_next = logits(unit + 1)
        p = jnp.exp2(s - jnp.max(s, axis=-1, keepdims=True))
        denom = jnp.sum(p, axis=-1, keepdims=True)
        o = _dot(p.astype(BF16), vt_ref[g]) / denom
        for j in range(heads_per_dot):
            h = first + j
            piece = o[j * rows:(j + 1) * rows, :]
            out = piece if h == 0 else jnp.where(block == h, piece, out)
        if first + heads_per_dot == HEADS_PER_KV:
            o_ref[:, g * GROUP_WIDTH:(g + 1) * GROUP_WIDTH] = out.astype(BF16)


def _attention(q, k, v, cache, layer, batch, seq):
    n = q.shape[0]
    tq = min(ATTN_Q_ROWS, seq)
    nq = seq // tq
    past = 0 if cache is None else cache[0].shape[2]
    heads_per_dot = max(1, min(HEADS_PER_KV, ATTN_LOGIT_ELEMS // (tq * (past + seq))))
    assert HEADS_PER_KV % heads_per_dot == 0
    own_kv = pl.BlockSpec((seq, KV_WIDTH), lambda b, i: (b, 0))
    in_specs = [pl.BlockSpec((tq, ATTN_WIDTH), lambda b, i: (b * nq + i, 0)), own_kv, own_kv]
    args = [q, k, v]
    if past:
        cached_kv = pl.BlockSpec((None, None, past, ATTN_KV_HEADS, HEAD_DIM), lambda b, i: (b, layer, 0, 0, 0))
        in_specs += [cached_kv, cached_kv]
        args += list(cache)
    return pl.pallas_call(
        functools.partial(_attention_kernel, past=past, heads_per_dot=heads_per_dot),
        out_shape=jax.ShapeDtypeStruct((n, ATTN_WIDTH), BF16),
        grid=(batch, nq),
        in_specs=in_specs,
        out_specs=pl.BlockSpec((tq, ATTN_WIDTH), lambda b, i: (b * nq + i, 0)),
        scratch_shapes=[pltpu.VMEM((ATTN_KV_HEADS, past + seq, GROUP_WIDTH), BF16)] * 2,
        compiler_params=_params("parallel", "arbitrary"),
        name="attention_latent" if past else "attention_context",
    )(*args)


def _stack_heads(x):
    block = _lane_block(x.shape, RET_DK)
    zero = jnp.zeros_like(x)
    return jnp.concatenate([jnp.where(block == h, x, zero) for h in range(RET_HEADS)], axis=-2)


def _head_norm(o, gm):
    d = o - _group_mean(o, gm)
    return d * lax.rsqrt(_group_mean(d * d, gm) + NORM_EPS)


def _retention_kernel(*refs, has_state, n_chunks, layer, starts_state_output):
    (rq_ref, rk_ref, rv_ref, g_ref, gm_ref, df_ref, db_ref, qd_ref, kd_ref, cdf_ref, cdb_ref,
     fx_ref, dft_c_ref, dft_p_ref) = refs[:14]
    refs = refs[14:]
    if has_state:
        s0_ref, o_ref, both_ref, upd_ref, st_ref = refs
    else:
        o_ref, sfin_ref, both_ref, upd_ref, st_ref = refs[-5:]
        if starts_state_output:
            for other in range(sfin_ref.shape[0]):
                if other != layer:
                    sfin_ref[other] = jnp.zeros(sfin_ref.shape[1:], F32)
            sfin_ref = sfin_ref.at[layer]

    width = RET_WIDTH
    along = _dot(fx_ref[...], dft_c_ref[...])
    stacked = jnp.concatenate([along[:, 0:FOURIER_WIDTH], along[:, FOURIER_WIDTH:]], axis=0).astype(BF16)
    o_ref[:, width:] = _dot(dft_p_ref[...], stacked).astype(BF16)

    diag = (lax.broadcasted_iota(jnp.int32, (width, width), 0) // RET_DK
            == lax.broadcasted_iota(jnp.int32, (width, width), 1) // RET_DV)

    chunked = lambda ref: ref[...].reshape(n_chunks, RET_CHUNK, width)
    batched = lambda a, b, contract: lax.dot_general(a, b, (contract, ((0,), (0,))), preferred_element_type=F32)
    q3, k3, v3 = chunked(rq_ref), chunked(rk_ref), chunked(rv_ref)
    scores = batched(q3, _stack_heads(k3), ((2,), (2,)))
    v_heads = _stack_heads(v3)
    seq_rows = n_chunks * RET_CHUNK
    both_ref[:, 0:width] = batched((scores * df_ref[...]).astype(BF16), v_heads, ((2,), (1,))).reshape(seq_rows, width)
    both_ref[:, width:] = batched((scores * db_ref[...]).astype(BF16), v_heads, ((2,), (1,))).reshape(seq_rows, width)
    k32 = k3.astype(F32)
    k_decayed = jnp.concatenate([k32, k32], axis=2) * kd_ref[...]
    upd_ref[...] = batched(jnp.swapaxes(k_decayed, 1, 2).astype(BF16), v3, ((2,), (1,)))

    for direction, order, cd_ref in ((0, range(n_chunks), cdf_ref), (1, reversed(range(n_chunks)), cdb_ref)):
        if has_state:
            s = s0_ref[direction].reshape(width, RET_DV)
            state = jnp.where(diag, jnp.concatenate([s] * RET_HEADS, axis=1), 0.0)
        else:
            state = jnp.zeros((width, width), F32)
        for c in order:
            st_ref[direction, c] = state.astype(BF16)
            update = upd_ref[c, direction * width:(direction + 1) * width, :]
            state = cd_ref[...] * state + jnp.where(diag, update, 0.0)
        if not has_state:
            folded = state[:, 0:width // 2] + state[:, width // 2:]
            sfin_ref[direction] = folded[:, 0:RET_DV] + folded[:, RET_DV:]

    q32 = q3.astype(F32)
    q_decayed = (jnp.concatenate([q32, q32], axis=2) * qd_ref[...]).astype(BF16)
    both_ref[:, 0:width] += batched(q_decayed[:, :, 0:width], st_ref[0], ((2,), (1,))).reshape(seq_rows, width)
    both_ref[:, width:] += batched(q_decayed[:, :, width:], st_ref[1], ((2,), (1,))).reshape(seq_rows, width)

    gated = _silu(g_ref[...]) * _head_norm(both_ref[...], gm_ref[...])
    o_ref[:, 0:width] = (gated[:, 0:width] + gated[:, width:]).astype(BF16)


def _retention_fourier(rq, rk, rv, gates, fx, group_mean, tables, dft, state, new_state, layer, batch, seq):
    n = rq.shape[0]
    has_state = state is not None
    n_chunks = seq // RET_CHUNK
    seq_blk = lambda width: pl.BlockSpec((seq, width), lambda b: (b, 0))
    whole = lambda a: pl.BlockSpec(a.shape, lambda b: (0,) * a.ndim)
    in_specs = ([seq_blk(RET_WIDTH)] * 3 + [seq_blk(2 * RET_WIDTH), whole(group_mean)] + [whole(t) for t in tables]
                + [seq_blk(FOURIER_WIDTH), whole(dft[0]), whole(dft[1])])
    args = [rq, rk, rv, gates, group_mean] + list(tables) + [fx, dft[0], dft[1]]
    aliases = {}
    out_shape = [jax.ShapeDtypeStruct((n, RET_WIDTH + FOURIER_WIDTH), BF16)]
    out_specs = [seq_blk(RET_WIDTH + FOURIER_WIDTH)]
    if has_state:
        in_specs.append(pl.BlockSpec((None, None, 2, RET_HEADS, RET_DK, RET_DV), lambda b: (b, layer, 0, 0, 0, 0)))
        args.append(state)
    else:
        state_blk = (2, RET_HEADS * RET_DK, RET_DV)
        out_shape.append(jax.ShapeDtypeStruct((batch, DEPTH) + state_blk, F32))
        if new_state is None:
            out_specs.append(pl.BlockSpec((None, DEPTH) + state_blk, lambda b: (b, 0, 0, 0, 0)))
        else:
            out_specs.append(pl.BlockSpec((None, None) + state_blk, lambda b: (b, layer, 0, 0, 0)))
            aliases[len(args)] = 1
            in_specs.append(pl.BlockSpec(memory_space=pl.ANY))
            args.append(new_state)
    return pl.pallas_call(
        functools.partial(_retention_kernel, has_state=has_state, n_chunks=n_chunks, layer=layer,
                          starts_state_output=new_state is None),
        out_shape=out_shape,
        grid=(batch,),
        in_specs=in_specs,
        out_specs=out_specs,
        input_output_aliases=aliases,
        scratch_shapes=[pltpu.VMEM((seq, 2 * RET_WIDTH), F32),
                        pltpu.VMEM((n_chunks, 2 * RET_WIDTH, RET_WIDTH), F32),
                        pltpu.VMEM((2, n_chunks, RET_WIDTH, RET_WIDTH), BF16)],
        compiler_params=_params("parallel"),
        name="retention_fourier_latent" if has_state else "retention_fourier_context",
    )(*args)


def _shift_rows(u, seq):
    rows = u.shape[0]
    pos = lax.broadcasted_iota(jnp.int32, u.shape, 0) % seq
    prev = jnp.where(pos == 0, 0.0, pltpu.roll(u, 1, 0))
    nxt = jnp.where(pos == seq - 1, 0.0, pltpu.roll(u, rows - 1, 0))
    return prev, nxt


def _mix_ffn_kernel(*refs, groups, final):
    n_weights = 7
    weights = refs[-n_weights - len(groups):-len(groups)]
    outs = refs[-len(groups):]
    tile = pl.program_id(0)
    pos = 0
    for (first, tiles, seq, halo), o_ref in zip(groups, outs):
        n_rows_refs = 9 if halo else 3
        row_refs, mod_ref = refs[pos:pos + n_rows_refs], refs[pos + n_rows_refs]
        pos += n_rows_refs + 1

        @pl.when((tile >= first) & (tile < first + tiles))
        def _(row_refs=row_refs, mod_ref=mod_ref, o_ref=o_ref, first=first, seq=seq, halo=halo):
            _mix_ffn_tile(row_refs, mod_ref, weights, o_ref, tile - first, seq=seq, halo=halo, final=final)


def _mix_ffn_tile(row_refs, mod_ref, weights, o_ref, tile, *, seq, halo, final):
    wout_ref, n2_ref, wup_ref, cw_ref, cb_ref, wd_ref, fin_ref = weights
    if halo:
        x_ref, x_top_ref, x_bot_ref, attn_ref, attn_top_ref, attn_bot_ref, rf_ref, rf_top_ref, rf_bot_ref = row_refs
    else:
        x_ref, attn_ref, rf_ref = row_refs
    rows = x_ref.shape[0]
    modulate = lambda x: _rms_rows(x, n2_ref[...]) * (1.0 + mod_ref[4:5, :]) + mod_ref[3:4, :]
    project_out = lambda attn, rf: (_dot(attn, wout_ref[MIX_OFF_ATTN:MIX_OFF_RET, :])
                                    + _dot(rf, wout_ref[MIX_OFF_RET:MIX_WIDTH, :]))
    if halo:
        pad = x_top_ref.shape[0]
        skip = attn_top_ref.shape[0] - pad
        tiles_per_seq = seq // rows
        place = tile % tiles_per_seq
        mixed = project_out(jnp.concatenate([attn_top_ref[...], attn_ref[...], attn_bot_ref[...]], axis=0),
                            jnp.concatenate([rf_top_ref[...], rf_ref[...], rf_bot_ref[...]], axis=0))
        x_ext = jnp.concatenate([x_top_ref[...], x_ref[...], x_bot_ref[...]], axis=0)
        x1_ext = x_ext + mod_ref[2:3, :] * mixed[skip:skip + pad + rows + pad, :]
        x1 = x1_ext[pad:pad + rows, :]
        h2 = modulate(x1_ext)
        row = lax.broadcasted_iota(jnp.int32, h2.shape, 0)
        first_kept = jnp.where(place == 0, pad, 0)
        end_kept = jnp.where(place == tiles_per_seq - 1, pad + rows, pad + rows + pad)
        h2 = jnp.where((row >= first_kept) & (row < end_kept), h2, 0.0)
        u = _dot(h2.astype(BF16), wup_ref[...])
        prev, nxt = pltpu.roll(u, 1, 0), pltpu.roll(u, u.shape[0] - 1, 0)
    else:
        x1 = x_ref[...] + mod_ref[2:3, :] * project_out(attn_ref[...], rf_ref[...])
        u = _dot(modulate(x1).astype(BF16), wup_ref[...])
        prev, nxt = _shift_rows(u, seq)
    u = prev * cw_ref[0:1, :] + u * cw_ref[1:2, :] + nxt * cw_ref[2:3, :] + cb_ref[...]
    if halo:
        u = u[pad:pad + rows, :]
    act = _silu(u[:, 0:D_FF]) * u[:, D_FF:]
    y = x1 + mod_ref[5:6, :] * _dot(act.astype(BF16), wd_ref[...])
    o_ref[...] = _rms_rows(y, fin_ref[...]) if final else y


def _mix_ffn(token_groups, w_out, norm2_w, w_up, conv_w, conv_b, w_down, final_w, layer, final):
    rows = FFN_ROWS
    whole = lambda shape: pl.BlockSpec(shape, lambda i: (0,) * len(shape))
    resident = lambda r, c: pl.BlockSpec((None, r, c), lambda i: (layer, 0, 0), pipeline_mode=pl.Buffered(1))
    in_specs, args, out_specs, out_shape, groups = [], [], [], [], []
    first = 0
    for x, attn, ret_four, mod, seq, per_seq_mod in token_groups:
        n = x.shape[0]
        tiles = n // rows
        halo = seq > rows
        assert n % rows == 0 and ((seq % rows == 0) if halo else (rows % seq == 0))
        local = lambda i, first=first, tiles=tiles: jnp.clip(i - first, 0, tiles - 1)
        row_blk = lambda width, local=local: pl.BlockSpec((rows, width), lambda i: (local(i), 0))

        def with_halo(a, pad, local=local, n=n):
            width = a.shape[1]
            per_tile, last = rows // pad, n // pad - 1
            return ([pl.BlockSpec((rows, width), lambda i: (local(i), 0)),
                     pl.BlockSpec((pad, width), lambda i: (jnp.maximum(local(i) * per_tile - 1, 0), 0)),
                     pl.BlockSpec((pad, width), lambda i: (jnp.minimum((local(i) + 1) * per_tile, last), 0))],
                    [a, a, a])

        for a, pad in ((x, F32_SUBLANES), (attn, BF16_SUBLANES), (ret_four, BF16_SUBLANES)):
            specs, arrays = with_halo(a, pad) if halo else ([row_blk(a.shape[1])], [a])
            in_specs += specs
            args += arrays
        mod_idx = ((lambda i, local=local, seq=seq: (local(i) * rows // seq, 0, 0)) if per_seq_mod
                   else (lambda i: (0, 0, 0)))
        in_specs.append(pl.BlockSpec((None, N_MOD, D_MODEL), mod_idx))
        args.append(mod)
        out_specs.append(row_blk(D_MODEL))
        out_shape.append(jax.ShapeDtypeStruct((n, D_MODEL), F32))
        groups.append((first, tiles, seq, halo))
        first += tiles
    in_specs += [resident(MIX_WIDTH, D_MODEL), whole((1, D_MODEL)),
                 resident(D_MODEL, 2 * D_FF), whole((3, 2 * D_FF)), whole((1, 2 * D_FF)),
                 resident(D_FF, D_MODEL), whole((1, D_MODEL))]
    args += [w_out, norm2_w, w_up, conv_w, conv_b, w_down, final_w]
    return pl.pallas_call(
        functools.partial(_mix_ffn_kernel, groups=tuple(groups), final=final),
        out_shape=out_shape,
        grid=(first,),
        in_specs=in_specs,
        out_specs=out_specs,
        compiler_params=_params("arbitrary"),
        name="mix_ffn",
    )(*args)


def _mixers(x, mod, lp, consts, layer, batch, seq, ctx, new_ctx):
    latent = ctx is not None
    outs = _in_projection(x, mod, lp["norm1_w"], consts["w_in"], lp["q_norm_w"], lp["k_norm_w"], consts["group_mean"],
                          consts["rope"] if latent else None, None if new_ctx is None else new_ctx[:2], layer, seq)
    q, k, v, rq, rk, rv, gates, fx = outs[:8]
    attn = _attention(q, k, v, ctx[:2] if latent else None, layer, batch, seq)
    ret_out = _retention_fourier(rq, rk, rv, gates, fx, consts["group_mean"], consts["ret_tables"], consts["dft"][seq],
                                 ctx[2] if latent else None, None if new_ctx is None else new_ctx[2],
                                 layer, batch, seq)
    return attn, ret_out[0], (None if latent else (outs[8], outs[9], ret_out[1]))


def kernel(x_prompt, x_sample, c, cache_attn_k, cache_attn_v, state_ret, c_ctx, w_mod, b_mod, norm1_w, w_in,
           q_norm_w, k_norm_w, w_out, norm2_w, w_up, conv_w, conv_b, w_down, final_norm_w):
    batch, seq, d = x_prompt.shape
    dec_batch, dec_seq, _ = x_sample.shape
    assert cache_attn_k.shape[1:2] + cache_attn_k.shape[3:] == (DEPTH, ATTN_KV_HEADS, HEAD_DIM)
    assert d == D_MODEL and w_in.shape == (DEPTH, D_MODEL, IN_WIDTH) and w_up.shape == (DEPTH, D_MODEL, 2 * D_FF)
    assert (batch * seq) % FFN_ROWS == 0 and dec_seq % FFN_ROWS == 0
    assert seq % min(ATTN_Q_ROWS, seq) == 0 and dec_seq % min(ATTN_Q_ROWS, dec_seq) == 0
    assert INPROJ_ROWS % seq == 0 and dec_seq % INPROJ_ROWS == 0 and seq % RET_CHUNK == 0 and dec_seq % RET_CHUNK == 0
    assert dec_batch + 1 <= MOD_ROWS and dec_seq % GRID_W == 0
    assert HEAD_DIM == RET_DV and GROUP_WIDTH == RET_WIDTH

    consts = {
        "group_mean": _group_mean_matrix(GROUP_WIDTH, HEAD_DIM),
        "rope": _rope_tables(dec_seq),
        "ret_tables": _retention_tables(),
        "dft": {s: _dft_tables(s) for s in {seq, dec_seq}},
        "final_w": final_norm_w.reshape(1, D_MODEL),
        "w_in": w_in.astype(BF16),
        "w_out": w_out.astype(BF16),
        "w_up": w_up.astype(BF16),
        "w_down": w_down.astype(BF16),
    }

    cvec = jnp.zeros((MOD_ROWS, D_MODEL), F32).at[0].set(c_ctx).at[1:1 + dec_batch].set(c)
    mod = _modulation(cvec, w_mod, b_mod).reshape(DEPTH, MOD_ROWS, N_MOD, D_MODEL)

    xp = x_prompt.reshape(batch * seq, D_MODEL)
    xs = x_sample.reshape(dec_batch * dec_seq, D_MODEL)
    new_ctx = None
    for layer in range(DEPTH):
        lp = {
            "norm1_w": norm1_w[layer].reshape(1, D_MODEL),
            "q_norm_w": jnp.tile(q_norm_w[layer], ATTN_HEADS).reshape(1, ATTN_WIDTH),
            "k_norm_w": jnp.tile(k_norm_w[layer], ATTN_KV_HEADS).reshape(1, KV_WIDTH),
            "norm2_w": norm2_w[layer].reshape(1, D_MODEL),
            "conv_w": conv_w[layer],
            "conv_b": conv_b[layer].reshape(1, 2 * D_FF),
        }
        mod_p, mod_s = mod[layer, 0:1], mod[layer, 1:1 + dec_batch]
        attn_p, rf_p, new_ctx = _mixers(xp, mod_p, lp, consts, layer, batch, seq, None, new_ctx)
        attn_s, rf_s, _ = _mixers(xs, mod_s, lp, consts, layer, dec_batch, dec_seq,
                                  (cache_attn_k, cache_attn_v, state_ret), None)
        xp, xs = _mix_ffn(((xp, attn_p, rf_p, mod_p, seq, False), (xs, attn_s, rf_s, mod_s, dec_seq, True)),
                          consts["w_out"], lp["norm2_w"], consts["w_up"], lp["conv_w"], lp["conv_b"], consts["w_down"],
                          consts["final_w"], layer, layer == DEPTH - 1)
    new_k, new_v, new_s = new_ctx
    return (xp.reshape(batch, seq, D_MODEL), xs.reshape(dec_batch, dec_seq, D_MODEL),
            new_k.reshape(batch, DEPTH, seq, ATTN_KV_HEADS, HEAD_DIM),
            new_v.reshape(batch, DEPTH, seq, ATTN_KV_HEADS, HEAD_DIM),
            new_s.reshape(batch, DEPTH, 2, RET_HEADS, RET_DK, RET_DV))
```

```python
import functools

import jax
import jax.numpy as jnp
import numpy as np
from jax import lax
from jax.experimental import pallas as pl
from jax.experimental.pallas import tpu as pltpu

F32 = jnp.float32
BF16 = jnp.bfloat16

D_MODEL = 1024
DEPTH = 2
GRID_W = 64
NORM_EPS = 1e-6
ATTN_HEADS = 8
ATTN_KV_HEADS = 2
HEAD_DIM = 64
ATTN_WIDTH = ATTN_HEADS * HEAD_DIM
KV_WIDTH = ATTN_KV_HEADS * HEAD_DIM
HEADS_PER_KV = ATTN_HEADS // ATTN_KV_HEADS
GROUP_WIDTH = HEADS_PER_KV * HEAD_DIM
ATTN_LOGIT_SCALE = HEAD_DIM ** -0.5 * 1.4426950408889634
ROPE_THETA = 10000.0
ROPE_AXIS_DIM = HEAD_DIM // 2
ROPE_HALF = ROPE_AXIS_DIM // 2
RET_HEADS = 4
RET_DK = 64
RET_DV = 64
RET_WIDTH = RET_HEADS * RET_DV
RET_CHUNK = 128
RET_DECAY_EXP_FWD = 5.0
RET_DECAY_EXP_BWD = 5.5
FOURIER_GROUPS = 4
FOURIER_DIM = 64
FOURIER_WIDTH = FOURIER_GROUPS * FOURIER_DIM
D_FF = 2816
N_MOD = 6

OFF_Q = 0
OFF_K = OFF_Q + ATTN_WIDTH
OFF_V = OFF_K + KV_WIDTH
OFF_RQ = OFF_V + KV_WIDTH
OFF_RK = OFF_RQ + RET_WIDTH
OFF_RV = OFF_RK + RET_WIDTH
OFF_GF = OFF_RV + RET_WIDTH
OFF_GB = OFF_GF + RET_WIDTH
OFF_FX = OFF_GB + RET_WIDTH
IN_WIDTH = OFF_FX + FOURIER_WIDTH
MIX_OFF_ATTN = 0
MIX_OFF_RET = MIX_OFF_ATTN + ATTN_WIDTH
MIX_OFF_FOURIER = MIX_OFF_RET + RET_WIDTH
MIX_WIDTH = MIX_OFF_FOURIER + FOURIER_WIDTH

V7X_VMEM_BYTES = 64 * 1024 * 1024
VMEM_LIMIT = V7X_VMEM_BYTES - 8 * 1024 * 1024

MOD_ROWS = 16
MOD_COLS = 1536
INPROJ_ROWS = 1024
ATTN_Q_ROWS = 1024
ATTN_LOGIT_ELEMS = 1024 * 1024
FFN_ROWS = 512
F32_SUBLANES = 8
BF16_SUBLANES = 16


def _params(*semantics):
    return pltpu.CompilerParams(dimension_semantics=semantics, vmem_limit_bytes=VMEM_LIMIT)


def _dot(a, b):
    return jnp.dot(a, b, preferred_element_type=F32)


def _group_mean(x, gm):
    xb = x.astype(BF16)
    width, slab = x.shape[1], gm.shape[0]
    if width <= slab:
        return _dot(xb, gm[0:width, 0:width])
    return jnp.concatenate([_dot(xb[:, s:s + slab], gm) for s in range(0, width, slab)], axis=1)


def _dot_nt(a, b):
    return lax.dot_general(a, b, (((1,), (1,)), ((), ())), preferred_element_type=F32)


def _sigmoid(x):
    return 1.0 / (1.0 + jnp.exp(-x))


def _silu(x):
    return x * _sigmoid(x)


def _rms_rows(x, w):
    ms = jnp.mean(x * x, axis=-1, keepdims=True)
    return x * lax.rsqrt(ms + NORM_EPS) * w


def _lane_block(shape, width):
    return lax.broadcasted_iota(jnp.int32, shape, len(shape) - 1) // width


def _group_mean_matrix(width, group):
    idx = np.arange(width) // group
    return jnp.asarray((idx[:, None] == idx[None, :]).astype(np.float32) / group, dtype=BF16)


def _rope_tables(n_tokens):
    pos = np.arange(n_tokens)
    row = (pos // GRID_W).astype(np.float64)
    col = (pos % GRID_W).astype(np.float64)
    freqs = ROPE_THETA ** (-np.arange(ROPE_HALF, dtype=np.float64) / ROPE_HALF)
    d = np.arange(HEAD_DIM)
    coord = np.where((d // ROPE_AXIS_DIM)[None, :] == 0, row[:, None], col[:, None])
    ang = coord * freqs[d % ROPE_HALF][None, :]
    sign = np.where((d & ROPE_HALF) == 0, -1.0, 1.0)[None, :]
    cos = np.tile(np.cos(ang), (1, ATTN_HEADS))
    sin = np.tile(np.sin(ang) * sign, (1, ATTN_HEADS))
    return jnp.asarray(cos, F32), jnp.asarray(sin, F32)


def _dft_tables(seq):
    n = np.arange(seq)
    ang = 2.0 * np.pi * ((n[:, None] * n[None, :]) % seq) / seq
    scale = 1.0 / np.sqrt(seq * FOURIER_DIM)
    position = np.concatenate([np.cos(ang), -np.sin(ang)], axis=1)
    c = np.arange(FOURIER_WIDTH)
    same = (c[:, None] // FOURIER_DIM) == (c[None, :] // FOURIER_DIM)
    angc = 2.0 * np.pi * (((c % FOURIER_DIM)[:, None] * (c % FOURIER_DIM)[None, :]) % FOURIER_DIM) / FOURIER_DIM
    channel = np.concatenate([np.where(same, np.cos(angc), 0.0), np.where(same, np.sin(angc), 0.0)], axis=1)
    as_bf16 = lambda a: jnp.asarray(a, F32).astype(BF16)
    return as_bf16(channel), as_bf16(position * scale)


def _retention_tables():
    heads = jnp.arange(RET_HEADS, dtype=F32)
    idx = jnp.arange(RET_CHUNK, dtype=F32)
    diff = idx[:, None] - idx[None, :]
    out = []
    for exp0, backward in ((RET_DECAY_EXP_FWD, False), (RET_DECAY_EXP_BWD, True)):
        lg = jnp.log1p(-jnp.exp2(-(exp0 + heads)))
        dd = -diff if backward else diff
        inner = jnp.where(dd[None] >= 0, jnp.exp(jnp.maximum(dd, 0.0)[None] * lg[:, None, None]), 0.0)
        inner = inner.transpose(1, 0, 2).reshape(RET_CHUNK, RET_HEADS * RET_CHUNK)
        q_pow = (RET_CHUNK - idx) if backward else (idx + 1.0)
        k_pow = idx if backward else (RET_CHUNK - 1.0 - idx)
        spread = lambda p: jnp.repeat(jnp.exp(p[:, None] * lg[None, :]), RET_DK, axis=1)
        q_decay, k_decay = spread(q_pow), spread(k_pow)
        chunk_decay = jnp.repeat(jnp.exp(RET_CHUNK * lg), RET_DV)[None, :]
        out.append((inner, q_decay, k_decay, chunk_decay))
    (inner_f, qd_f, kd_f, cd_f), (inner_b, qd_b, kd_b, cd_b) = out
    return (inner_f, inner_b, jnp.concatenate([qd_f, qd_b], axis=1), jnp.concatenate([kd_f, kd_b], axis=1), cd_f, cd_b)


def _mod_kernel(c_ref, w_ref, b_ref, o_ref):
    act = _silu(c_ref[...]).astype(BF16)
    o_ref[...] = _dot(act, w_ref[...].astype(BF16)) + b_ref[...]


def _modulation(cvec, w_mod, b_mod):
    n_cols = w_mod.shape[-1]
    return pl.pallas_call(
        _mod_kernel,
        out_shape=jax.ShapeDtypeStruct((DEPTH, MOD_ROWS, n_cols), F32),
        grid=(DEPTH, n_cols // MOD_COLS),
        in_specs=[
            pl.BlockSpec((MOD_ROWS, D_MODEL), lambda l, j: (0, 0)),
            pl.BlockSpec((None, D_MODEL, MOD_COLS), lambda l, j: (l, 0, j)),
            pl.BlockSpec((None, 1, MOD_COLS), lambda l, j: (l, 0, j)),
        ],
        out_specs=pl.BlockSpec((None, MOD_ROWS, MOD_COLS), lambda l, j: (l, 0, j)),
        compiler_params=_params("parallel", "parallel"),
        name="modulation",
    )(cvec, w_mod, b_mod.reshape(DEPTH, 1, n_cols))


def _swap_rotary_pairs(x):
    width = x.shape[-1]
    lane = lax.broadcasted_iota(jnp.int32, x.shape, 1)
    from_below = pltpu.roll(x, ROPE_HALF, 1)
    from_above = pltpu.roll(x, width - ROPE_HALF, 1)
    return jnp.where((lane & ROPE_HALF) != 0, from_below, from_above)


def _inproj_kernel(*refs, latent, layer, starts_cache_output):
    x_ref, mod_ref, n1_ref, win_ref, qnw_ref, knw_ref, gm_ref = refs[:7]
    refs = refs[7:]
    if latent:
        cos_ref, sin_ref = refs[:2]
        refs = refs[2:]
        q_ref, k_ref, v_ref, rq_ref, rk_ref, rv_ref, g_ref, fx_ref = refs
    else:
        q_ref, k_ref, v_ref, rq_ref, rk_ref, rv_ref, g_ref, fx_ref, k32_ref, v32_ref = refs[-10:]

    shift, scale = mod_ref[0:1, :], mod_ref[1:2, :]
    h = _rms_rows(x_ref[...], n1_ref[...]) * (1.0 + scale) + shift
    all_proj = _dot(h.astype(BF16), win_ref[...])
    proj = lambda off, width: all_proj[:, off:off + width]

    q = proj(OFF_Q, ATTN_WIDTH)
    q = q * lax.rsqrt(_group_mean(q * q, gm_ref[...]) + NORM_EPS) * qnw_ref[...]
    k = proj(OFF_K, KV_WIDTH)
    k = k * lax.rsqrt(_group_mean(k * k, gm_ref[...]) + NORM_EPS) * knw_ref[...]
    v = proj(OFF_V, KV_WIDTH)
    if latent:
        cos, sin = cos_ref[...], sin_ref[...]
        q = q * cos + _swap_rotary_pairs(q) * sin
        k = k * cos[:, 0:KV_WIDTH] + _swap_rotary_pairs(k) * sin[:, 0:KV_WIDTH]
    else:
        for ref, val in ((k32_ref, k), (v32_ref, v)):
            if starts_cache_output:
                for other in range(ref.shape[1]):
                    if other != layer:
                        ref[:, other] = jnp.zeros((ref.shape[0],) + ref.shape[2:], F32)
                ref[:, layer] = val.reshape((ref.shape[0],) + ref.shape[2:])
            else:
                ref[...] = val.reshape(ref.shape)
    q_ref[...] = (q * ATTN_LOGIT_SCALE).astype(BF16)
    k_ref[...] = k.astype(BF16)
    v_ref[...] = v.astype(BF16)
    rq_ref[...] = proj(OFF_RQ, RET_WIDTH).astype(BF16)
    rk_ref[...] = (proj(OFF_RK, RET_WIDTH) * RET_DK ** -0.5).astype(BF16)
    rv_ref[...] = proj(OFF_RV, RET_WIDTH).astype(BF16)
    g_ref[...] = proj(OFF_GF, 2 * RET_WIDTH)
    fx_ref[...] = proj(OFF_FX, FOURIER_WIDTH).astype(BF16)


def _in_projection(x, mod, norm1_w, w_in, q_norm_w, k_norm_w, group_mean, rope, new_cache, layer, seq):
    n = x.shape[0]
    rows = INPROJ_ROWS
    latent = rope is not None
    row_blk = lambda width: pl.BlockSpec((rows, width), lambda i: (i, 0))
    whole = lambda shape: pl.BlockSpec(shape, lambda i: (0,) * len(shape))
    mod_idx = (lambda i: (i * rows // seq, 0, 0)) if latent else (lambda i: (0, 0, 0))
    in_specs = [
        row_blk(D_MODEL),
        pl.BlockSpec((None, N_MOD, D_MODEL), mod_idx),
        whole((1, D_MODEL)),
        pl.BlockSpec((None, D_MODEL, IN_WIDTH), lambda i: (layer, 0, 0)),
        whole((1, ATTN_WIDTH)),
        whole((1, KV_WIDTH)),
        whole(group_mean.shape),
    ]
    args = [x, mod, norm1_w, w_in, q_norm_w, k_norm_w, group_mean]
    outs = [(ATTN_WIDTH, BF16), (KV_WIDTH, BF16), (KV_WIDTH, BF16), (RET_WIDTH, BF16), (RET_WIDTH, BF16),
            (RET_WIDTH, BF16), (2 * RET_WIDTH, F32), (FOURIER_WIDTH, BF16)]
    out_shape = [jax.ShapeDtypeStruct((n, w), dt) for w, dt in outs]
    out_specs = [row_blk(w) for w, _ in outs]
    aliases = {}
    if latent:
        pos_blk = pl.BlockSpec((rows, ATTN_WIDTH), lambda i: (i % (seq // rows), 0))
        in_specs += [pos_blk, pos_blk]
        args += list(rope)
    else:
        seqs = rows // seq
        out_shape += [jax.ShapeDtypeStruct((n // seq, DEPTH, seq, KV_WIDTH), F32)] * 2
        if new_cache is None:
            out_specs += [pl.BlockSpec((seqs, DEPTH, seq, KV_WIDTH), lambda i: (i, 0, 0, 0))] * 2
        else:
            out_specs += [pl.BlockSpec((seqs, None, seq, KV_WIDTH), lambda i: (i, layer, 0, 0))] * 2
            for j, earlier in enumerate(new_cache):
                aliases[len(args)] = len(out_shape) - 2 + j
                in_specs.append(pl.BlockSpec(memory_space=pl.ANY))
                args.append(earlier)
    return pl.pallas_call(
        functools.partial(_inproj_kernel, latent=latent, layer=layer,
                          starts_cache_output=not latent and new_cache is None),
        out_shape=out_shape,
        grid=(n // rows,),
        in_specs=in_specs,
        out_specs=out_specs,
        input_output_aliases=aliases,
        compiler_params=_params("parallel"),
        name="in_projection_latent" if latent else "in_projection_context",
    )(*args)


def _spread_kv(x, group):
    lane = lax.broadcasted_iota(jnp.int32, x.shape, 1)
    other = pltpu.roll(x, HEAD_DIM, 1)
    own = (lane // HEAD_DIM) == group
    pair = jnp.where(own, x, other).astype(BF16)
    return jnp.concatenate([pair, pair], axis=1)


def _attention_kernel(*refs, past, heads_per_dot):
    if past:
        q_ref, k_ref, v_ref, ck_ref, cv_ref, o_ref, kt_ref, vt_ref = refs
    else:
        q_ref, k_ref, v_ref, o_ref, kt_ref, vt_ref = refs

    @pl.when(pl.program_id(1) == 0)
    def _():
        for g in range(ATTN_KV_HEADS):
            if past:
                kt_ref[g, 0:past, :] = _spread_kv(ck_ref[...], g)
                vt_ref[g, 0:past, :] = _spread_kv(cv_ref[...], g)
            kt_ref[g, past:, :] = _spread_kv(k_ref[...].astype(F32), g)
            vt_ref[g, past:, :] = _spread_kv(v_ref[...].astype(F32), g)

    rows = q_ref.shape[0]
    block = _lane_block((rows, GROUP_WIDTH), HEAD_DIM)

    def logits(unit):
        g, first = divmod(unit * heads_per_dot, HEADS_PER_KV)
        qg = q_ref[:, g * GROUP_WIDTH:(g + 1) * GROUP_WIDTH]
        zero = jnp.zeros_like(qg)
        stacked = jnp.concatenate([jnp.where(block == first + j, qg, zero) for j in range(heads_per_dot)], axis=0)
        return _dot_nt(stacked, kt_ref[g])

    n_units = ATTN_HEADS // heads_per_dot
    s_next = logits(0)
    out = None
    for unit in range(n_units):
        g, first = divmod(unit * heads_per_dot, HEADS_PER_KV)
        s = s_next
        if unit + 1 < n_units:
            s_next = logits(unit + 1)
        p = jnp.exp2(s - jnp.max(s, axis=-1, keepdims=True))
        denom = jnp.sum(p, axis=-1, keepdims=True)
        o = _dot(p.astype(BF16), vt_ref[g]) / denom
        for j in range(heads_per_dot):
            h = first + j
            piece = o[j * rows:(j + 1) * rows, :]
            out = piece if h == 0 else jnp.where(block == h, piece, out)
        if first + heads_per_dot == HEADS_PER_KV:
            o_ref[:, g * GROUP_WIDTH:(g + 1) * GROUP_WIDTH] = out.astype(BF16)


def _attention(q, k, v, cache, layer, batch, seq):
    n = q.shape[0]
    tq = min(ATTN_Q_ROWS, seq)
    nq = seq // tq
    past = 0 if cache is None else cache[0].shape[2]
    heads_per_dot = max(1, min(HEADS_PER_KV, ATTN_LOGIT_ELEMS // (tq * (past + seq))))
    assert HEADS_PER_KV % heads_per_dot == 0
    own_kv = pl.BlockSpec((seq, KV_WIDTH), lambda b, i: (b, 0))
    in_specs = [pl.BlockSpec((tq, ATTN_WIDTH), lambda b, i: (b * nq + i, 0)), own_kv, own_kv]
    args = [q, k, v]
    if past:
        cached_kv = pl.BlockSpec((None, None, past, KV_WIDTH), lambda b, i: (b, layer, 0, 0))
        in_specs += [cached_kv, cached_kv]
        args += list(cache)
    return pl.pallas_call(
        functools.partial(_attention_kernel, past=past, heads_per_dot=heads_per_dot),
        out_shape=jax.ShapeDtypeStruct((n, ATTN_WIDTH), BF16),
        grid=(batch, nq),
        in_specs=in_specs,
        out_specs=pl.BlockSpec((tq, ATTN_WIDTH), lambda b, i: (b * nq + i, 0)),
        scratch_shapes=[pltpu.VMEM((ATTN_KV_HEADS, past + seq, GROUP_WIDTH), BF16)] * 2,
        compiler_params=_params("parallel", "arbitrary"),
        name="attention_latent" if past else "attention_context",
    )(*args)


def _stack_heads(x):
    block = _lane_block(x.shape, RET_DK)
    zero = jnp.zeros_like(x)
    return jnp.concatenate([jnp.where(block == h, x, zero) for h in range(RET_HEADS)], axis=-2)


def _head_norm(o, gm):
    d = o - _group_mean(o, gm)
    return d * lax.rsqrt(_group_mean(d * d, gm) + NORM_EPS)


def _retention_kernel(*refs, has_state, n_chunks, layer, starts_state_output):
    (rq_ref, rk_ref, rv_ref, g_ref, gm_ref, df_ref, db_ref, qd_ref, kd_ref, cdf_ref, cdb_ref,
     fx_ref, dft_c_ref, dft_p_ref) = refs[:14]
    refs = refs[14:]
    if has_state:
        s0_ref, o_ref, both_ref, upd_ref, st_ref = refs
    else:
        o_ref, sfin_ref, both_ref, upd_ref, st_ref = refs[-5:]
        if starts_state_output:
            for other in range(sfin_ref.shape[0]):
                if other != layer:
                    sfin_ref[other] = jnp.zeros(sfin_ref.shape[1:], F32)
            sfin_ref = sfin_ref.at[layer]

    width = RET_WIDTH
    along = _dot(fx_ref[...], dft_c_ref[...])
    stacked = jnp.concatenate([along[:, 0:FOURIER_WIDTH], along[:, FOURIER_WIDTH:]], axis=0).astype(BF16)
    o_ref[:, width:] = _dot(dft_p_ref[...], stacked).astype(BF16)

    diag = (lax.broadcasted_iota(jnp.int32, (width, width), 0) // RET_DK
            == lax.broadcasted_iota(jnp.int32, (width, width), 1) // RET_DV)

    chunked = lambda ref: ref[...].reshape(n_chunks, RET_CHUNK, width)
    batched = lambda a, b, contract: lax.dot_general(a, b, (contract, ((0,), (0,))), preferred_element_type=F32)
    q3, k3, v3 = chunked(rq_ref), chunked(rk_ref), chunked(rv_ref)
    scores = batched(q3, _stack_heads(k3), ((2,), (2,)))
    v_heads = _stack_heads(v3)
    seq_rows = n_chunks * RET_CHUNK
    both_ref[:, 0:width] = batched((scores * df_ref[...]).astype(BF16), v_heads, ((2,), (1,))).reshape(seq_rows, width)
    both_ref[:, width:] = batched((scores * db_ref[...]).astype(BF16), v_heads, ((2,), (1,))).reshape(seq_rows, width)
    k32 = k3.astype(F32)
    k_decayed = jnp.concatenate([k32, k32], axis=2) * kd_ref[...]
    upd_ref[...] = batched(jnp.swapaxes(k_decayed, 1, 2).astype(BF16), v3, ((2,), (1,)))

    for direction, order, cd_ref in ((0, range(n_chunks), cdf_ref), (1, reversed(range(n_chunks)), cdb_ref)):
        if has_state:
            s = s0_ref[direction].reshape(width, RET_DV)
            state = jnp.where(diag, jnp.concatenate([s] * RET_HEADS, axis=1), 0.0)
        else:
            state = jnp.zeros((width, width), F32)
        for c in order:
            st_ref[direction, c] = state.astype(BF16)
            update = upd_ref[c, direction * width:(direction + 1) * width, :]
            state = cd_ref[...] * state + jnp.where(diag, update, 0.0)
        if not has_state:
            folded = state[:, 0:width // 2] + state[:, width // 2:]
            sfin_ref[direction] = folded[:, 0:RET_DV] + folded[:, RET_DV:]

    q32 = q3.astype(F32)
    q_decayed = (jnp.concatenate([q32, q32], axis=2) * qd_ref[...]).astype(BF16)
    both_ref[:, 0:width] += batched(q_decayed[:, :, 0:width], st_ref[0], ((2,), (1,))).reshape(seq_rows, width)
    both_ref[:, width:] += batched(q_decayed[:, :, width:], st_ref[1], ((2,), (1,))).reshape(seq_rows, width)

    gated = _silu(g_ref[...]) * _head_norm(both_ref[...], gm_ref[...])
    o_ref[:, 0:width] = (gated[:, 0:width] + gated[:, width:]).astype(BF16)


def _retention_fourier(rq, rk, rv, gates, fx, group_mean, tables, dft, state, new_state, layer, batch, seq):
    n = rq.shape[0]
    has_state = state is not None
    n_chunks = seq // RET_CHUNK
    seq_blk = lambda width: pl.BlockSpec((seq, width), lambda b: (b, 0))
    whole = lambda a: pl.BlockSpec(a.shape, lambda b: (0,) * a.ndim)
    in_specs = ([seq_blk(RET_WIDTH)] * 3 + [seq_blk(2 * RET_WIDTH), whole(group_mean)] + [whole(t) for t in tables]
                + [seq_blk(FOURIER_WIDTH), whole(dft[0]), whole(dft[1])])
    args = [rq, rk, rv, gates, group_mean] + list(tables) + [fx, dft[0], dft[1]]
    aliases = {}
    out_shape = [jax.ShapeDtypeStruct((n, RET_WIDTH + FOURIER_WIDTH), BF16)]
    out_specs = [seq_blk(RET_WIDTH + FOURIER_WIDTH)]
    if has_state:
        in_specs.append(pl.BlockSpec((None, None, 2, RET_HEADS, RET_DK, RET_DV), lambda b: (b, layer, 0, 0, 0, 0)))
        args.append(state)
    else:
        state_blk = (2, RET_HEADS * RET_DK, RET_DV)
        out_shape.append(jax.ShapeDtypeStruct((batch, DEPTH) + state_blk, F32))
        if new_state is None:
            out_specs.append(pl.BlockSpec((None, DEPTH) + state_blk, lambda b: (b, 0, 0, 0, 0)))
        else:
            out_specs.append(pl.BlockSpec((None, None) + state_blk, lambda b: (b, layer, 0, 0, 0)))
            aliases[len(args)] = 1
            in_specs.append(pl.BlockSpec(memory_space=pl.ANY))
            args.append(new_state)
    return pl.pallas_call(
        functools.partial(_retention_kernel, has_state=has_state, n_chunks=n_chunks, layer=layer,
                          starts_state_output=new_state is None),
        out_shape=out_shape,
        grid=(batch,),
        in_specs=in_specs,
        out_specs=out_specs,
        input_output_aliases=aliases,
        scratch_shapes=[pltpu.VMEM((seq, 2 * RET_WIDTH), F32),
                        pltpu.VMEM((n_chunks, 2 * RET_WIDTH, RET_WIDTH), F32),
                        pltpu.VMEM((2, n_chunks, RET_WIDTH, RET_WIDTH), BF16)],
        compiler_params=_params("parallel"),
        name="retention_fourier_latent" if has_state else "retention_fourier_context",
    )(*args)


def _shift_rows(u, seq):
    rows = u.shape[0]
    pos = lax.broadcasted_iota(jnp.int32, u.shape, 0) % seq
    prev = jnp.where(pos == 0, 0.0, pltpu.roll(u, 1, 0))
    nxt = jnp.where(pos == seq - 1, 0.0, pltpu.roll(u, rows - 1, 0))
    return prev, nxt


def _mix_ffn_kernel(*refs, groups, layer, final):
    n_scratch = 4
    wout_hbm, n2_ref, wup_hbm, cw_ref, cb_ref, wd_hbm, fin_ref = refs[-7 - len(groups) - n_scratch:-len(groups) - n_scratch]
    outs = refs[-len(groups) - n_scratch:-n_scratch]
    wout_ref, wup_ref, wd_ref, sem = refs[-n_scratch:]
    tile = pl.program_id(0)

    @pl.when(tile == 0)
    def _():
        copies = [pltpu.make_async_copy(src.at[layer], dst, sem.at[j])
                  for j, (src, dst) in enumerate(((wout_hbm, wout_ref), (wup_hbm, wup_ref), (wd_hbm, wd_ref)))]
        for copy in copies:
            copy.start()
        for copy in copies:
            copy.wait()

    weights = (wout_ref, n2_ref, wup_ref, cw_ref, cb_ref, wd_ref, fin_ref)
    pos = 0
    for (first, tiles, seq, halo), o_ref in zip(groups, outs):
        n_rows_refs = 9 if halo else 3
        row_refs, mod_ref = refs[pos:pos + n_rows_refs], refs[pos + n_rows_refs]
        pos += n_rows_refs + 1

        @pl.when((tile >= first) & (tile < first + tiles))
        def _(row_refs=row_refs, mod_ref=mod_ref, o_ref=o_ref, first=first, seq=seq, halo=halo):
            _mix_ffn_tile(row_refs, mod_ref, weights, o_ref, tile - first, seq=seq, halo=halo, final=final)


def _mix_ffn_tile(row_refs, mod_ref, weights, o_ref, tile, *, seq, halo, final):
    wout_ref, n2_ref, wup_ref, cw_ref, cb_ref, wd_ref, fin_ref = weights
    if halo:
        x_ref, x_top_ref, x_bot_ref, attn_ref, attn_top_ref, attn_bot_ref, rf_ref, rf_top_ref, rf_bot_ref = row_refs
    else:
        x_ref, attn_ref, rf_ref = row_refs
    rows = x_ref.shape[0]
    modulate = lambda x: _rms_rows(x, n2_ref[...]) * (1.0 + mod_ref[4:5, :]) + mod_ref[3:4, :]
    project_out = lambda attn, rf: (_dot(attn, wout_ref[MIX_OFF_ATTN:MIX_OFF_RET, :])
                                    + _dot(rf, wout_ref[MIX_OFF_RET:MIX_WIDTH, :]))
    if halo:
        pad = x_top_ref.shape[0]
        skip = attn_top_ref.shape[0] - pad
        tiles_per_seq = seq // rows
        place = tile % tiles_per_seq
        mixed = project_out(jnp.concatenate([attn_top_ref[...], attn_ref[...], attn_bot_ref[...]], axis=0),
                            jnp.concatenate([rf_top_ref[...], rf_ref[...], rf_bot_ref[...]], axis=0))
        x_ext = jnp.concatenate([x_top_ref[...], x_ref[...], x_bot_ref[...]], axis=0)
        x1_ext = x_ext + mod_ref[2:3, :] * mixed[skip:skip + pad + rows + pad, :]
        x1 = x1_ext[pad:pad + rows, :]
        h2 = modulate(x1_ext)
        row = lax.broadcasted_iota(jnp.int32, h2.shape, 0)
        first_kept = jnp.where(place == 0, pad, 0)
        end_kept = jnp.where(place == tiles_per_seq - 1, pad + rows, pad + rows + pad)
        h2 = jnp.where((row >= first_kept) & (row < end_kept), h2, 0.0)
        u = _dot(h2.astype(BF16), wup_ref[...])
        prev, nxt = pltpu.roll(u, 1, 0), pltpu.roll(u, u.shape[0] - 1, 0)
    else:
        x1 = x_ref[...] + mod_ref[2:3, :] * project_out(attn_ref[...], rf_ref[...])
        u = _dot(modulate(x1).astype(BF16), wup_ref[...])
        prev, nxt = _shift_rows(u, seq)
    u = prev * cw_ref[0:1, :] + u * cw_ref[1:2, :] + nxt * cw_ref[2:3, :] + cb_ref[...]
    if halo:
        u = u[pad:pad + rows, :]
    act = _silu(u[:, 0:D_FF]) * u[:, D_FF:]
    y = x1 + mod_ref[5:6, :] * _dot(act.astype(BF16), wd_ref[...])
    o_ref[...] = _rms_rows(y, fin_ref[...]) if final else y


def _mix_ffn(token_groups, w_out, norm2_w, w_up, conv_w, conv_b, w_down, final_w, layer, final):
    rows = FFN_ROWS
    whole = lambda shape: pl.BlockSpec(shape, lambda i: (0,) * len(shape))
    in_specs, args, out_specs, out_shape, groups = [], [], [], [], []
    first = 0
    for x, attn, ret_four, mod, seq, per_seq_mod in token_groups:
        n = x.shape[0]
        tiles = n // rows
        halo = seq > rows
        assert n % rows == 0 and ((seq % rows == 0) if halo else (rows % seq == 0))
        local = lambda i, first=first, tiles=tiles: jnp.clip(i - first, 0, tiles - 1)
        row_blk = lambda width, local=local: pl.BlockSpec((rows, width), lambda i: (local(i), 0))

        def with_halo(a, pad, local=local, n=n):
            width = a.shape[1]
            per_tile, last = rows // pad, n // pad - 1
            return ([pl.BlockSpec((rows, width), lambda i: (local(i), 0)),
                     pl.BlockSpec((pad, width), lambda i: (jnp.maximum(local(i) * per_tile - 1, 0), 0)),
                     pl.BlockSpec((pad, width), lambda i: (jnp.minimum((local(i) + 1) * per_tile, last), 0))],
                    [a, a, a])

        for a, pad in ((x, F32_SUBLANES), (attn, BF16_SUBLANES), (ret_four, BF16_SUBLANES)):
            specs, arrays = with_halo(a, pad) if halo else ([row_blk(a.shape[1])], [a])
            in_specs += specs
            args += arrays
        mod_idx = ((lambda i, local=local, seq=seq: (local(i) * rows // seq, 0, 0)) if per_seq_mod
                   else (lambda i: (0, 0, 0)))
        in_specs.append(pl.BlockSpec((None, N_MOD, D_MODEL), mod_idx))
        args.append(mod)
        out_specs.append(row_blk(D_MODEL))
        out_shape.append(jax.ShapeDtypeStruct((n, D_MODEL), F32))
        groups.append((first, tiles, seq, halo))
        first += tiles
    in_hbm = pl.BlockSpec(memory_space=pl.ANY)
    in_specs += [in_hbm, whole((1, D_MODEL)), in_hbm, whole((3, 2 * D_FF)), whole((1, 2 * D_FF)), in_hbm,
                 whole((1, D_MODEL))]
    args += [w_out, norm2_w, w_up, conv_w, conv_b, w_down, final_w]
    return pl.pallas_call(
        functools.partial(_mix_ffn_kernel, groups=tuple(groups), layer=layer, final=final),
        out_shape=out_shape,
        grid=(first,),
        in_specs=in_specs,
        out_specs=out_specs,
        scratch_shapes=[pltpu.VMEM(w.shape[1:], w.dtype) for w in (w_out, w_up, w_down)]
        + [pltpu.SemaphoreType.DMA((3,))],
        compiler_params=_params("arbitrary"),
        name="mix_ffn",
    )(*args)


def _mixers(x, mod, lp, consts, layer, batch, seq, ctx, new_ctx):
    latent = ctx is not None
    outs = _in_projection(x, mod, lp["norm1_w"], consts["w_in"], lp["q_norm_w"], lp["k_norm_w"], consts["group_mean"],
                          consts["rope"] if latent else None, None if new_ctx is None else new_ctx[:2], layer, seq)
    q, k, v, rq, rk, rv, gates, fx = outs[:8]
    attn = _attention(q, k, v, ctx[:2] if latent else None, layer, batch, seq)
    ret_out = _retention_fourier(rq, rk, rv, gates, fx, consts["group_mean"], consts["ret_tables"], consts["dft"][seq],
                                 ctx[2] if latent else None, None if new_ctx is None else new_ctx[2],
                                 layer, batch, seq)
    return attn, ret_out[0], (None if latent else (outs[8], outs[9], ret_out[1]))


def kernel(x_prompt, x_sample, c, cache_attn_k, cache_attn_v, state_ret, c_ctx, w_mod, b_mod, norm1_w, w_in,
           q_norm_w, k_norm_w, w_out, norm2_w, w_up, conv_w, conv_b, w_down, final_norm_w):
    batch, seq, d = x_prompt.shape
    dec_batch, dec_seq, _ = x_sample.shape
    past = cache_attn_k.shape[2]
    assert d == D_MODEL and w_in.shape == (DEPTH, D_MODEL, IN_WIDTH) and w_up.shape == (DEPTH, D_MODEL, 2 * D_FF)
    assert (batch * seq) % FFN_ROWS == 0 and dec_seq % FFN_ROWS == 0
    assert seq % min(ATTN_Q_ROWS, seq) == 0 and dec_seq % min(ATTN_Q_ROWS, dec_seq) == 0
    assert INPROJ_ROWS % seq == 0 and dec_seq % INPROJ_ROWS == 0 and seq % RET_CHUNK == 0 and dec_seq % RET_CHUNK == 0
    assert dec_batch + 1 <= MOD_ROWS and dec_seq % GRID_W == 0
    assert HEAD_DIM == RET_DV and GROUP_WIDTH == RET_WIDTH

    consts = {
        "group_mean": _group_mean_matrix(GROUP_WIDTH, HEAD_DIM),
        "rope": _rope_tables(dec_seq),
        "ret_tables": _retention_tables(),
        "dft": {s: _dft_tables(s) for s in {seq, dec_seq}},
        "final_w": final_norm_w.reshape(1, D_MODEL),
        "w_in": w_in.astype(BF16),
        "w_out": w_out.astype(BF16),
        "w_up": w_up.astype(BF16),
        "w_down": w_down.astype(BF16),
    }

    cvec = jnp.zeros((MOD_ROWS, D_MODEL), F32).at[0].set(c_ctx).at[1:1 + dec_batch].set(c)
    mod = _modulation(cvec, w_mod, b_mod).reshape(DEPTH, MOD_ROWS, N_MOD, D_MODEL)

    cache_k = cache_attn_k.reshape(dec_batch, DEPTH, past, KV_WIDTH)
    cache_v = cache_attn_v.reshape(dec_batch, DEPTH, past, KV_WIDTH)

    xp = x_prompt.reshape(batch * seq, D_MODEL)
    xs = x_sample.reshape(dec_batch * dec_seq, D_MODEL)
    new_ctx = None
    for layer in range(DEPTH):
        lp = {
            "norm1_w": norm1_w[layer].reshape(1, D_MODEL),
            "q_norm_w": jnp.tile(q_norm_w[layer], ATTN_HEADS).reshape(1, ATTN_WIDTH),
            "k_norm_w": jnp.tile(k_norm_w[layer], ATTN_KV_HEADS).reshape(1, KV_WIDTH),
            "norm2_w": norm2_w[layer].reshape(1, D_MODEL),
            "conv_w": conv_w[layer],
            "conv_b": conv_b[layer].reshape(1, 2 * D_FF),
        }
        mod_p, mod_s = mod[layer, 0:1], mod[layer, 1:1 + dec_batch]
        attn_p, rf_p, new_ctx = _mixers(xp, mod_p, lp, consts, layer, batch, seq, None, new_ctx)
        attn_s, rf_s, _ = _mixers(xs, mod_s, lp, consts, layer, dec_batch, dec_seq, (cache_k, cache_v, state_ret), None)
        xp, xs = _mix_ffn(((xp, attn_p, rf_p, mod_p, seq, False), (xs, attn_s, rf_s, mod_s, dec_seq, True)),
                          consts["w_out"], lp["norm2_w"], consts["w_up"], lp["conv_w"], lp["conv_b"], consts["w_down"],
                          consts["final_w"], layer, layer == DEPTH - 1)
    new_k, new_v, new_s = new_ctx
    return (xp.reshape(batch, seq, D_MODEL), xs.reshape(dec_batch, dec_seq, D_MODEL),
            new_k.reshape(batch, DEPTH, seq, ATTN_KV_HEADS, HEAD_DIM),
            new_v.reshape(batch, DEPTH, seq, ATTN_KV_HEADS, HEAD_DIM),
            new_s.reshape(batch, DEPTH, 2, RET_HEADS, RET_DK, RET_DV))
```
